```python
import math
import jax, jax.numpy as jnp
from jax import lax
import numpy as np

D_MODEL = 2048
BATCH = 1
SEQ = 8192
DEPTH = 1

CHUNK = 64
MIX_WIDTH = D_MODEL
A_WIDTH = MIX_WIDTH // 2
A_HEADS = 8
A_HEAD_DIM = A_WIDTH // A_HEADS
GMLP_BLOCK = 128
B_WIDTH = MIX_WIDTH - A_WIDTH
S5_GROUP_CH = 16
S5_GROUPS = B_WIDTH // S5_GROUP_CH
S5_STATE = 64
IN_WIDTH = 2 * A_WIDTH + B_WIDTH
D_FF = int(math.ceil((8 * D_MODEL / 3) / 256) * 256)
EPS = 1e-6

kernel_name = "hybrid_gmlp_s5_parallel_block"


def rmsnorm(x, g):
    xf = x.astype(jnp.float32)
    y = xf * lax.rsqrt(jnp.mean(xf * xf, axis=-1, keepdims=True) + EPS)
    return (y * g.astype(jnp.float32)).astype(x.dtype)


def layernorm(x, g, b):
    xf = x.astype(jnp.float32)
    mu = jnp.mean(xf, axis=-1, keepdims=True)
    xc = xf - mu
    var = jnp.mean(xc * xc, axis=-1, keepdims=True)
    y = xc * lax.rsqrt(var + EPS) * g.astype(jnp.float32) + b.astype(jnp.float32)
    return y.astype(x.dtype)


def spatial_gating(z_u, z_v, ln_g, ln_b, w_s, b_s):
    bsz, seq, _ = z_u.shape
    nb = seq // GMLP_BLOCK
    v = layernorm(z_v, ln_g, ln_b).reshape(bsz, nb, GMLP_BLOCK, A_HEADS, A_HEAD_DIM)
    chunk_id = jnp.arange(GMLP_BLOCK) // CHUNK
    mask = chunk_id[:, None] >= chunk_id[None, :]
    w = jnp.where(mask[None], w_s, 0.0)
    v = jnp.einsum('hij,bnjhc->bnihc', w, v) + jnp.transpose(b_s)[None, None, :, :, None]
    return z_u * v.reshape(bsz, seq, A_WIDTH)


def _complex_linear_combine(e1, e2):
    a1r, a1i, b1r, b1i = e1
    a2r, a2i, b2r, b2i = e2
    return (a1r * a2r - a1i * a2i,
            a1r * a2i + a1i * a2r,
            a2r * b1r - a2i * b1i + b2r,
            a2r * b1i + a2i * b1r + b2i)


def s5_mixer(u, lam_re, lam_im, log_dt, b_re, b_im, c_re, c_im, d, w_glu):
    f32 = jnp.float32
    bsz, seq, _ = u.shape
    uf = u.astype(f32).reshape(bsz, seq, S5_GROUPS, S5_GROUP_CH)
    dt = jnp.exp(log_dt.astype(f32))[:, None]
    lr = lam_re.astype(f32)
    li = lam_im.astype(f32)
    mag = jnp.exp(lr * dt)
    ab_r = mag * jnp.cos(li * dt)
    ab_i = mag * jnp.sin(li * dt)
    den = lr * lr + li * li
    nr = ab_r - 1.0
    ni = ab_i
    co_r = (nr * lr + ni * li) / den
    co_i = (ni * lr - nr * li) / den
    br = b_re.astype(f32)
    bi = b_im.astype(f32)
    bb_r = co_r[..., None] * br - co_i[..., None] * bi
    bb_i = co_r[..., None] * bi + co_i[..., None] * br
    bu_r = jnp.einsum('blgh,gph->blgp', uf, bb_r)
    bu_i = jnp.einsum('blgh,gph->blgp', uf, bb_i)
    a_r = jnp.broadcast_to(ab_r, bu_r.shape)
    a_i = jnp.broadcast_to(ab_i, bu_i.shape)
    _, _, s_r, s_i = lax.associative_scan(_complex_linear_combine, (a_r, a_i, bu_r, bu_i), axis=1)
    y = (jnp.einsum('blgp,ghp->blgh', s_r, c_re.astype(f32))
         - jnp.einsum('blgp,ghp->blgh', s_i, c_im.astype(f32))
         + d.astype(f32).reshape(S5_GROUPS, S5_GROUP_CH) * uf)
    y = jax.nn.gelu(y, approximate=False)
    y = y * jax.nn.sigmoid(jnp.einsum('blgh,ghk->blgk', y, w_glu.astype(f32)))
    return y.reshape(bsz, seq, B_WIDTH).astype(u.dtype)


def setup_inputs(seed: int = 0) -> dict:
    key = jax.random.key(seed)
    ks = jax.random.split(key, 25)
    f32 = jnp.float32

    def nrm(k, shape, scale):
        return jax.random.normal(k, shape, f32) * scale

    L = DEPTH
    n = jnp.arange(S5_STATE, dtype=f32)
    return {
        "x": nrm(ks[0], (BATCH, SEQ, D_MODEL), 1.0),
        "norm_mix_g": 1.0 + nrm(ks[1], (L, D_MODEL), 0.02),
        "w_in": nrm(ks[2], (L, D_MODEL, IN_WIDTH), D_MODEL ** -0.5),
        "a_ln_g": 1.0 + nrm(ks[3], (L, A_WIDTH), 0.02),
        "a_ln_b": nrm(ks[4], (L, A_WIDTH), 0.02),
        "a_w_s": nrm(ks[5], (L, A_HEADS, GMLP_BLOCK, GMLP_BLOCK), GMLP_BLOCK ** -0.5),
        "a_b_s": 1.0 + nrm(ks[6], (L, A_HEADS, GMLP_BLOCK), 0.02),
        "s5_lambda_re": -0.5 + nrm(ks[7], (L, S5_GROUPS, S5_STATE), 0.01),
        "s5_lambda_im": jnp.pi * n + nrm(ks[8], (L, S5_GROUPS, S5_STATE), 0.01),
        "s5_log_dt": jax.random.uniform(ks[9], (L, S5_GROUPS), f32, math.log(1e-3), math.log(1e-1)),
        "s5_b_re": nrm(ks[10], (L, S5_GROUPS, S5_STATE, S5_GROUP_CH), (2 * S5_GROUP_CH) ** -0.5),
        "s5_b_im": nrm(ks[11], (L, S5_GROUPS, S5_STATE, S5_GROUP_CH), (2 * S5_GROUP_CH) ** -0.5),
        "s5_c_re": nrm(ks[12], (L, S5_GROUPS, S5_GROUP_CH, S5_STATE), S5_STATE ** -0.5),
        "s5_c_im": nrm(ks[13], (L, S5_GROUPS, S5_GROUP_CH, S5_STATE), S5_STATE ** -0.5),
        "s5_d": nrm(ks[14], (L, B_WIDTH), 1.0),
        "s5_w_glu": nrm(ks[15], (L, S5_GROUPS, S5_GROUP_CH, S5_GROUP_CH), S5_GROUP_CH ** -0.5),
        "out_norm_a_g": 1.0 + nrm(ks[16], (L, A_WIDTH), 0.02),
        "out_norm_b_g": 1.0 + nrm(ks[17], (L, B_WIDTH), 0.02),
        "w_out": nrm(ks[18], (L, MIX_WIDTH, D_MODEL), MIX_WIDTH ** -0.5),
        "norm_ffn_g": 1.0 + nrm(ks[19], (L, D_MODEL), 0.02),
        "w_gate": nrm(ks[20], (L, D_MODEL, D_FF), D_MODEL ** -0.5),
        "w_up": nrm(ks[21], (L, D_MODEL, D_FF), D_MODEL ** -0.5),
        "w_down": nrm(ks[22], (L, D_FF, D_MODEL), D_FF ** -0.5),
        "final_norm_g": 1.0 + nrm(ks[23], (D_MODEL,), 0.02),
    }


def reference(x, norm_mix_g, w_in, a_ln_g, a_ln_b, a_w_s, a_b_s, s5_lambda_re, s5_lambda_im,
              s5_log_dt, s5_b_re, s5_b_im, s5_c_re, s5_c_im, s5_d, s5_w_glu, out_norm_a_g,
              out_norm_b_g, w_out, norm_ffn_g, w_gate, w_up, w_down, final_norm_g):
    for l in range(DEPTH):
        h = jnp.einsum('bsd,de->bse', rmsnorm(x, norm_mix_g[l]), w_in[l])
        z = jax.nn.gelu(h[..., :2 * A_WIDTH], approximate=False)
        y_a = spatial_gating(z[..., :A_WIDTH], z[..., A_WIDTH:], a_ln_g[l], a_ln_b[l],
                             a_w_s[l], a_b_s[l])
        y_b = s5_mixer(h[..., 2 * A_WIDTH:], s5_lambda_re[l], s5_lambda_im[l], s5_log_dt[l],
                       s5_b_re[l], s5_b_im[l], s5_c_re[l], s5_c_im[l], s5_d[l],
                       s5_w_glu[l])
        mix = jnp.concatenate([rmsnorm(y_a, out_norm_a_g[l]),
                               rmsnorm(y_b, out_norm_b_g[l])], axis=-1)
        x = x + jnp.einsum('bse,ed->bsd', mix, w_out[l])
        hn = rmsnorm(x, norm_ffn_g[l])
        g = jnp.einsum('bsd,df->bsf', hn, w_gate[l])
        up = jnp.einsum('bsd,df->bsf', hn, w_up[l])
        x = x + jnp.einsum('bsf,fd->bsd', jax.nn.silu(g) * up, w_down[l])
    return rmsnorm(x, final_norm_g)
```

```python
import functools
import math

import jax
import jax.numpy as jnp
from jax import lax
from jax.experimental import pallas as pl
from jax.experimental.pallas import tpu as pltpu

F32 = jnp.float32
BF16 = jnp.bfloat16

D_MODEL = 2048
SEQ = 8192
CHUNK = 64
A_WIDTH = 1024
A_HEADS = 8
A_HEAD_DIM = 128
GMLP_BLOCK = 128
B_WIDTH = 1024
S5_GROUP_CH = 16
S5_GROUPS = 64
S5_STATE = 64
IN_WIDTH = 3072
D_FF = 5632
EPS = 1e-6

LANES = 128
SUBLANES = 8
VMEM_LIMIT = 56 * 1024 * 1024

S5_LANE_TILES = B_WIDTH // LANES
S5_GROUPS_PER_TILE = LANES // S5_GROUP_CH
S5_STATE_TILES = S5_GROUPS * S5_STATE // LANES
S5_TILES_PER_LANE_TILE = S5_STATE_TILES // S5_LANE_TILES
S5_STATE_VREGS = S5_STATE_TILES // SUBLANES
S5_T = 256
S5_PITCH = S5_T + 4


def _gelu(x):
    return 0.5 * x * (1.0 + lax.erf(x * (1.0 / math.sqrt(2.0))))


def _rms_scale(x):
    return lax.rsqrt(jnp.mean(x * x, axis=-1, keepdims=True) + EPS)


def _inproj_kernel(x_ref, g_ref, w_ref, o_ref, xn_ref):
    j = pl.program_id(1)

    @pl.when(j == 0)
    def _():
        x = x_ref[...]
        xn_ref[...] = (x * _rms_scale(x) * g_ref[...]).astype(BF16)

    h = jnp.dot(xn_ref[...], w_ref[...], preferred_element_type=F32)

    @pl.when(j < 2)
    def _():
        o_ref[...] = _gelu(h)

    @pl.when(j == 2)
    def _():
        o_ref[...] = h


def _inproj(x, g, w_bf16, tm=1024, tn=1024):
    return pl.pallas_call(
        _inproj_kernel,
        grid=(SEQ // tm, IN_WIDTH // tn),
        in_specs=[
            pl.BlockSpec((tm, D_MODEL), lambda i, j: (i, 0)),
            pl.BlockSpec((1, D_MODEL), lambda i, j: (0, 0)),
            pl.BlockSpec((D_MODEL, tn), lambda i, j: (0, j)),
        ],
        out_specs=pl.BlockSpec((tm, tn), lambda i, j: (i, j)),
        out_shape=jax.ShapeDtypeStruct((SEQ, IN_WIDTH), F32),
        scratch_shapes=[pltpu.VMEM((tm, D_MODEL), BF16)],
        compiler_params=pltpu.CompilerParams(
            dimension_semantics=("arbitrary", "arbitrary"),
            vmem_limit_bytes=VMEM_LIMIT),
        name="inproj",
    )(x, g, w_bf16)


def _gmlp_kernel(zu_ref, zv_ref, lng_ref, lnb_ref, w_ref, bias_ref, og_ref, o_ref, sg_ref):
    rows = zv_ref.shape[0]
    zv = zv_ref[...]
    mu = jnp.mean(zv, axis=-1, keepdims=True)
    xc = zv - mu
    var = jnp.mean(xc * xc, axis=-1, keepdims=True)
    v = (xc * lax.rsqrt(var + EPS) * lng_ref[...] + lnb_ref[...]).astype(BF16)

    ci = lax.broadcasted_iota(jnp.int32, (GMLP_BLOCK, GMLP_BLOCK), 0) // CHUNK
    cj = lax.broadcasted_iota(jnp.int32, (GMLP_BLOCK, GMLP_BLOCK), 1) // CHUNK
    mask = ci >= cj
    for h in range(A_HEADS):
        w = jnp.where(mask, w_ref[h], 0.0).astype(BF16)
        cols = slice(h * A_HEAD_DIM, (h + 1) * A_HEAD_DIM)
        for n in range(rows // GMLP_BLOCK):
            rws = slice(n * GMLP_BLOCK, (n + 1) * GMLP_BLOCK)
            sg_ref[rws, cols] = (
                jnp.dot(w, v[rws, cols], preferred_element_type=F32) + bias_ref[:, cols])

    ya = zu_ref[...] * sg_ref[...]
    o_ref[...] = (ya * _rms_scale(ya) * og_ref[...]).astype(BF16)


def _gmlp(h, ln_g, ln_b, w_s, bias_full, out_g, rows=512):
    return pl.pallas_call(
        _gmlp_kernel,
        grid=(SEQ // rows,),
        in_specs=[
            pl.BlockSpec((rows, A_WIDTH), lambda i: (i, 0)),
            pl.BlockSpec((rows, A_WIDTH), lambda i: (i, 1)),
            pl.BlockSpec((1, A_WIDTH), lambda i: (0, 0)),
            pl.BlockSpec((1, A_WIDTH), lambda i: (0, 0)),
            pl.BlockSpec((A_HEADS, GMLP_BLOCK, GMLP_BLOCK), lambda i: (0, 0, 0)),
            pl.BlockSpec((GMLP_BLOCK, A_WIDTH), lambda i: (0, 0)),
            pl.BlockSpec((1, A_WIDTH), lambda i: (0, 0)),
        ],
        out_specs=pl.BlockSpec((rows, A_WIDTH), lambda i: (i, 0)),
        out_shape=jax.ShapeDtypeStruct((SEQ, A_WIDTH), BF16),
        scratch_shapes=[pltpu.VMEM((rows, A_WIDTH), F32)],
        compiler_params=pltpu.CompilerParams(
            dimension_semantics=("arbitrary",), vmem_limit_bytes=VMEM_LIMIT),
        name="gmlp",
    )(h, h, ln_g, ln_b, w_s, bias_full, out_g)


def _s5_state_tiles(k):
    re = [k * S5_TILES_PER_LANE_TILE + c for c in range(S5_TILES_PER_LANE_TILE)]
    return re + [S5_STATE_TILES + q for q in re]


def _s5_kernel(u_ref, wb_ref, wc_ref, wglu_ref, are_ref, aim_ref, d_ref, og_ref,
               o_ref, st_ref, bu_ref, y_ref):
    t_rows = u_ref.shape[0]

    @pl.when(pl.program_id(0) == 0)
    def _():
        st_ref[...] = jnp.zeros_like(st_ref)

    for k in range(S5_LANE_TILES):
        ub = u_ref[:, k * LANES:(k + 1) * LANES].astype(BF16)
        res = jnp.dot(ub, wb_ref[k], preferred_element_type=F32)
        for c, q in enumerate(_s5_state_tiles(k)):
            bu_ref[q * S5_PITCH:q * S5_PITCH + t_rows, :] = res[:, c * LANES:(c + 1) * LANES]

    a_re = [are_ref[m] for m in range(S5_STATE_VREGS)]
    a_im = [aim_ref[m] for m in range(S5_STATE_VREGS)]
    im_base = S5_STATE_TILES * S5_PITCH

    def step(t, carry):
        s_re, s_im = carry
        n_re, n_im = [], []
        for m in range(S5_STATE_VREGS):
            re_rows = pl.ds(t + SUBLANES * m * S5_PITCH, SUBLANES, stride=S5_PITCH)
            im_rows = pl.ds(t + im_base + SUBLANES * m * S5_PITCH, SUBLANES, stride=S5_PITCH)
            nr = a_re[m] * s_re[m] - a_im[m] * s_im[m] + bu_ref[re_rows, :]
            ni = a_re[m] * s_im[m] + a_im[m] * s_re[m] + bu_ref[im_rows, :]
            bu_ref[re_rows, :] = nr
            bu_ref[im_rows, :] = ni
            n_re.append(nr)
            n_im.append(ni)
        return tuple(n_re), tuple(n_im)

    s0_re = tuple(st_ref[m] for m in range(S5_STATE_VREGS))
    s0_im = tuple(st_ref[S5_STATE_VREGS + m] for m in range(S5_STATE_VREGS))
    s_re, s_im = lax.fori_loop(0, t_rows, step, (s0_re, s0_im), unroll=8)
    for m in range(S5_STATE_VREGS):
        st_ref[m] = s_re[m]
        st_ref[S5_STATE_VREGS + m] = s_im[m]

    ss = jnp.zeros((t_rows, 1), F32)
    for k in range(S5_LANE_TILES):
        cols = slice(k * LANES, (k + 1) * LANES)
        lhs = jnp.concatenate(
            [bu_ref[q * S5_PITCH:q * S5_PITCH + t_rows, :] for q in _s5_state_tiles(k)],
            axis=1).astype(BF16)
        y = jnp.dot(lhs, wc_ref[k], preferred_element_type=F32) + d_ref[:, cols] * u_ref[:, cols]
        y = _gelu(y)
        gate = jax.nn.sigmoid(jnp.dot(y.astype(BF16), wglu_ref[k], preferred_element_type=F32))
        y = y * gate
        ss = ss + jnp.sum(y * y, axis=-1, keepdims=True)
        y_ref[:, cols] = y

    scale = lax.rsqrt(ss * (1.0 / B_WIDTH) + EPS)
    o_ref[...] = (y_ref[...] * scale * og_ref[...]).astype(BF16)


def _s5(h, wb, wc, wglu, a_re, a_im, d, out_g):
    n_state_cols = 2 * S5_GROUPS_PER_TILE * S5_STATE
    return pl.pallas_call(
        _s5_kernel,
        grid=(SEQ // S5_T,),
        in_specs=[
            pl.BlockSpec((S5_T, B_WIDTH), lambda i: (i, 2)),
            pl.BlockSpec((S5_LANE_TILES, LANES, n_state_cols), lambda i: (0, 0, 0)),
            pl.BlockSpec((S5_LANE_TILES, n_state_cols, LANES), lambda i: (0, 0, 0)),
            pl.BlockSpec((S5_LANE_TILES, LANES, LANES), lambda i: (0, 0, 0)),
            pl.BlockSpec((S5_STATE_VREGS, SUBLANES, LANES), lambda i: (0, 0, 0)),
            pl.BlockSpec((S5_STATE_VREGS, SUBLANES, LANES), lambda i: (0, 0, 0)),
            pl.BlockSpec((1, B_WIDTH), lambda i: (0, 0)),
            pl.BlockSpec((1, B_WIDTH), lambda i: (0, 0)),
        ],
        out_specs=pl.BlockSpec((S5_T, B_WIDTH), lambda i: (i, 0)),
        out_shape=jax.ShapeDtypeStruct((SEQ, B_WIDTH), BF16),
        scratch_shapes=[
            pltpu.VMEM((2 * S5_STATE_VREGS, SUBLANES, LANES), F32),
            pltpu.VMEM((2 * S5_STATE_TILES * S5_PITCH, LANES), F32),
            pltpu.VMEM((S5_T, B_WIDTH), F32),
        ],
        compiler_params=pltpu.CompilerParams(
            dimension_semantics=("arbitrary",), vmem_limit_bytes=VMEM_LIMIT),
        name="s5",
    )(h, wb, wc, wglu, a_re, a_im, d, out_g)


def _s5_params(lam_re, lam_im, log_dt, b_re, b_im, c_re, c_im, w_glu):
    dt = jnp.exp(log_dt)[:, None]
    mag = jnp.exp(lam_re * dt)
    ab_r = mag * jnp.cos(lam_im * dt)
    ab_i = mag * jnp.sin(lam_im * dt)
    den = lam_re * lam_re + lam_im * lam_im
    nr = ab_r - 1.0
    ni = ab_i
    co_r = (nr * lam_re + ni * lam_im) / den
    co_i = (ni * lam_re - nr * lam_im) / den
    bb_r = co_r[..., None] * b_re - co_i[..., None] * b_im
    bb_i = co_r[..., None] * b_im + co_i[..., None] * b_re

    gt = S5_GROUPS_PER_TILE
    eye = jnp.eye(gt, dtype=F32)

    def blockdiag(blocks):
        _, r, c = blocks.shape
        b = blocks.reshape(S5_LANE_TILES, gt, r, c)
        out = jnp.einsum('kgrc,gh->kgrhc', b, eye)
        return out.reshape(S5_LANE_TILES, gt * r, gt * c)

    wb = jnp.concatenate([blockdiag(jnp.swapaxes(bb_r, 1, 2)),
                          blockdiag(jnp.swapaxes(bb_i, 1, 2))], axis=2)
    wc = jnp.concatenate([blockdiag(jnp.swapaxes(c_re, 1, 2)),
                          blockdiag(-jnp.swapaxes(c_im, 1, 2))], axis=1)
    wglu = blockdiag(w_glu)
    a_re = ab_r.reshape(S5_STATE_VREGS, SUBLANES, LANES)
    a_im = ab_i.reshape(S5_STATE_VREGS, SUBLANES, LANES)
    return wb.astype(BF16), wc.astype(BF16), wglu.astype(BF16), a_re, a_im


def _outproj_kernel(x_ref, ma_ref, mb_ref, wa_ref, wb_ref, g_ref, x1_ref, hn_ref):
    acc = jnp.dot(ma_ref[...], wa_ref[...], preferred_element_type=F32)
    acc = acc + jnp.dot(mb_ref[...], wb_ref[...], preferred_element_type=F32)
    x1 = x_ref[...] + acc
    x1_ref[...] = x1
    hn_ref[...] = (x1 * _rms_scale(x1) * g_ref[...]).astype(BF16)


def _outproj(x, mix_a, mix_b, w_out_bf16, g, tm=512):
    return pl.pallas_call(
        _outproj_kernel,
        grid=(SEQ // tm,),
        in_specs=[
            pl.BlockSpec((tm, D_MODEL), lambda i: (i, 0)),
            pl.BlockSpec((tm, A_WIDTH), lambda i: (i, 0)),
            pl.BlockSpec((tm, B_WIDTH), lambda i: (i, 0)),
            pl.BlockSpec((A_WIDTH, D_MODEL), lambda i: (0, 0)),
            pl.BlockSpec((B_WIDTH, D_MODEL), lambda i: (1, 0)),
            pl.BlockSpec((1, D_MODEL), lambda i: (0, 0)),
        ],
        out_specs=[
            pl.BlockSpec((tm, D_MODEL), lambda i: (i, 0)),
            pl.BlockSpec((tm, D_MODEL), lambda i: (i, 0)),
        ],
        out_shape=[jax.ShapeDtypeStruct((SEQ, D_MODEL), F32),
                   jax.ShapeDtypeStruct((SEQ, D_MODEL), BF16)],
        compiler_params=pltpu.CompilerParams(
            dimension_semantics=("arbitrary",), vmem_limit_bytes=VMEM_LIMIT),
        name="outproj",
    )(x, mix_a, mix_b, w_out_bf16, w_out_bf16, g)


def _ffn_kernel(hn_ref, x1_ref, wg_ref, wu_ref, wd_ref, g_ref, o_ref, acc_ref):
    f = pl.program_id(1)
    hn = hn_ref[...]
    gate = jnp.dot(hn, wg_ref[...], preferred_element_type=F32)
    up = jnp.dot(hn, wu_ref[...], preferred_element_type=F32)
    act = (gate * jax.nn.sigmoid(gate) * up).astype(BF16)
    part = jnp.dot(act, wd_ref[...], preferred_element_type=F32)

    @pl.when(f == 0)
    def _():
        acc_ref[...] = x1_ref[...] + part

    @pl.when(f > 0)
    def _():
        acc_ref[...] += part

    @pl.when(f == pl.num_programs(1) - 1)
    def _():
        x2 = acc_ref[...]
        o_ref[...] = x2 * _rms_scale(x2) * g_ref[...]


def _ffn(hn, x1, wg, wu, wd, g, tm=512, tf=512):
    return pl.pallas_call(
        _ffn_kernel,
        grid=(SEQ // tm, D_FF // tf),
        in_specs=[
            pl.BlockSpec((tm, D_MODEL), lambda i, f: (i, 0)),
            pl.BlockSpec((tm, D_MODEL), lambda i, f: (i, 0)),
            pl.BlockSpec((D_MODEL, tf), lambda i, f: (0, f)),
            pl.BlockSpec((D_MODEL, tf), lambda i, f: (0, f)),
            pl.BlockSpec((tf, D_MODEL), lambda i, f: (f, 0)),
            pl.BlockSpec((1, D_MODEL), lambda i, f: (0, 0)),
        ],
        out_specs=pl.BlockSpec((tm, D_MODEL), lambda i, f: (i, 0)),
        out_shape=jax.ShapeDtypeStruct((SEQ, D_MODEL), F32),
        scratch_shapes=[pltpu.VMEM((tm, D_MODEL), F32)],
        compiler_params=pltpu.CompilerParams(
            dimension_semantics=("arbitrary", "arbitrary"),
            vmem_limit_bytes=VMEM_LIMIT),
        name="ffn",
    )(hn, x1, wg, wu, wd, g)


def kernel(x, norm_mix_g, w_in, a_ln_g, a_ln_b, a_w_s, a_b_s, s5_lambda_re, s5_lambda_im,
           s5_log_dt, s5_b_re, s5_b_im, s5_c_re, s5_c_im, s5_d, s5_w_glu, out_norm_a_g,
           out_norm_b_g, w_out, norm_ffn_g, w_gate, w_up, w_down, final_norm_g):
    assert x.shape == (1, SEQ, D_MODEL) and norm_mix_g.shape[0] == 1
    xs = x.reshape(SEQ, D_MODEL)
    l = 0

    h = _inproj(xs, norm_mix_g[l][None], w_in[l].astype(BF16))

    bias_full = jnp.repeat(jnp.transpose(a_b_s[l]), A_HEAD_DIM, axis=1)
    mix_a = _gmlp(h, a_ln_g[l][None], a_ln_b[l][None], a_w_s[l], bias_full,
                  out_norm_a_g[l][None])

    wb, wc, wglu, a_re, a_im = _s5_params(
        s5_lambda_re[l], s5_lambda_im[l], s5_log_dt[l], s5_b_re[l], s5_b_im[l],
        s5_c_re[l], s5_c_im[l], s5_w_glu[l])
    mix_b = _s5(h, wb, wc, wglu, a_re, a_im, s5_d[l][None], out_norm_b_g[l][None])

    x1, hn = _outproj(xs, mix_a, mix_b, w_out[l].astype(BF16), norm_ffn_g[l][None])

    out = _ffn(hn, x1, w_gate[l].astype(BF16), w_up[l].astype(BF16),
               w_down[l].astype(BF16), final_norm_g[None])
    return out.reshape(1, SEQ, D_MODEL)
```

```python
import functools
import math

import jax
import jax.numpy as jnp
from jax import lax
from jax.experimental import pallas as pl
from jax.experimental.pallas import tpu as pltpu

F32 = jnp.float32
BF16 = jnp.bfloat16

D_MODEL = 2048
SEQ = 8192
CHUNK = 64
A_WIDTH = 1024
A_HEADS = 8
A_HEAD_DIM = 128
GMLP_BLOCK = 128
B_WIDTH = 1024
S5_GROUP_CH = 16
S5_GROUPS = 64
S5_STATE = 64
IN_WIDTH = 3072
D_FF = 5632
EPS = 1e-6

LANES = 128
SUBLANES = 8
VMEM_LIMIT = 56 * 1024 * 1024

S5_LANE_TILES = B_WIDTH // LANES
S5_GROUPS_PER_TILE = LANES // S5_GROUP_CH
S5_STATE_TILES = S5_GROUPS * S5_STATE // LANES
S5_TILES_PER_LANE_TILE = S5_STATE_TILES // S5_LANE_TILES
S5_STATE_VREGS = S5_STATE_TILES // SUBLANES
S5_T = 256
S5_PITCH = S5_T + 4


def _gelu(x):
    return 0.5 * x * (1.0 + lax.erf(x * (1.0 / math.sqrt(2.0))))


def _rms_scale(x):
    return lax.rsqrt(jnp.mean(x * x, axis=-1, keepdims=True) + EPS)


def _inproj_gmlp_kernel(x_ref, g_ref, w_ref, lng_ref, lnb_ref, ws_ref, bias_ref, og_ref,
                        mixa_ref, u_ref, sg_ref):
    rows = x_ref.shape[0]
    x = x_ref[...]
    xn = (x * _rms_scale(x) * g_ref[...]).astype(BF16)
    zu = _gelu(jnp.dot(xn, w_ref[:, 0:A_WIDTH], preferred_element_type=F32))
    zv = _gelu(jnp.dot(xn, w_ref[:, A_WIDTH:2 * A_WIDTH], preferred_element_type=F32))
    u_ref[...] = jnp.dot(xn, w_ref[:, 2 * A_WIDTH:], preferred_element_type=F32)

    mu = jnp.mean(zv, axis=-1, keepdims=True)
    xc = zv - mu
    var = jnp.mean(xc * xc, axis=-1, keepdims=True)
    v = (xc * lax.rsqrt(var + EPS) * lng_ref[...] + lnb_ref[...]).astype(BF16)

    ci = lax.broadcasted_iota(jnp.int32, (GMLP_BLOCK, GMLP_BLOCK), 0) // CHUNK
    cj = lax.broadcasted_iota(jnp.int32, (GMLP_BLOCK, GMLP_BLOCK), 1) // CHUNK
    mask = ci >= cj
    for h in range(A_HEADS):
        w = jnp.where(mask, ws_ref[h], 0.0).astype(BF16)
        cols = slice(h * A_HEAD_DIM, (h + 1) * A_HEAD_DIM)
        for n in range(rows // GMLP_BLOCK):
            rws = slice(n * GMLP_BLOCK, (n + 1) * GMLP_BLOCK)
            sg_ref[rws, cols] = (
                jnp.dot(w, v[rws, cols], preferred_element_type=F32) + bias_ref[:, cols])

    ya = zu * sg_ref[...]
    mixa_ref[...] = (ya * _rms_scale(ya) * og_ref[...]).astype(BF16)


def _inproj_gmlp(x, g, w_bf16, ln_g, ln_b, w_s, bias_full, out_g, tm=512):
    const2 = lambda i: (0, 0)
    return pl.pallas_call(
        _inproj_gmlp_kernel,
        grid=(SEQ // tm,),
        in_specs=[
            pl.BlockSpec((tm, D_MODEL), lambda i: (i, 0)),
            pl.BlockSpec((1, D_MODEL), const2),
            pl.BlockSpec((D_MODEL, IN_WIDTH), const2),
            pl.BlockSpec((1, A_WIDTH), const2),
            pl.BlockSpec((1, A_WIDTH), const2),
            pl.BlockSpec((A_HEADS, GMLP_BLOCK, GMLP_BLOCK), lambda i: (0, 0, 0)),
            pl.BlockSpec((GMLP_BLOCK, A_WIDTH), const2),
            pl.BlockSpec((1, A_WIDTH), const2),
        ],
        out_specs=[
            pl.BlockSpec((tm, A_WIDTH), lambda i: (i, 0)),
            pl.BlockSpec((tm, B_WIDTH), lambda i: (i, 0)),
        ],
        out_shape=[jax.ShapeDtypeStruct((SEQ, A_WIDTH), BF16),
                   jax.ShapeDtypeStruct((SEQ, B_WIDTH), F32)],
        scratch_shapes=[pltpu.VMEM((tm, A_WIDTH), F32)],
        compiler_params=pltpu.CompilerParams(
            dimension_semantics=("arbitrary",), vmem_limit_bytes=VMEM_LIMIT),
        name="inproj_gmlp",
    )(x, g, w_bf16, ln_g, ln_b, w_s, bias_full, out_g)


def _s5_state_tiles(k):
    re = [k * S5_TILES_PER_LANE_TILE + c for c in range(S5_TILES_PER_LANE_TILE)]
    return re + [S5_STATE_TILES + q for q in re]


def _s5_kernel(u_ref, wb_ref, wc_ref, wglu_ref, are_ref, aim_ref, d_ref, og_ref,
               o_ref, st_ref, bu_ref, y_ref):
    t_rows = u_ref.shape[0]

    @pl.when(pl.program_id(0) == 0)
    def _():
        st_ref[...] = jnp.zeros_like(st_ref)

    for k in range(S5_LANE_TILES):
        ub = u_ref[:, k * LANES:(k + 1) * LANES].astype(BF16)
        res = jnp.dot(ub, wb_ref[k], preferred_element_type=F32)
        for c, q in enumerate(_s5_state_tiles(k)):
            bu_ref[q * S5_PITCH:q * S5_PITCH + t_rows, :] = res[:, c * LANES:(c + 1) * LANES]

    a_re = [are_ref[m] for m in range(S5_STATE_VREGS)]
    a_im = [aim_ref[m] for m in range(S5_STATE_VREGS)]
    im_base = S5_STATE_TILES * S5_PITCH

    def step(t, carry):
        s_re, s_im = carry
        n_re, n_im = [], []
        for m in range(S5_STATE_VREGS):
            re_rows = pl.ds(t + SUBLANES * m * S5_PITCH, SUBLANES, stride=S5_PITCH)
            im_rows = pl.ds(t + im_base + SUBLANES * m * S5_PITCH, SUBLANES, stride=S5_PITCH)
            nr = a_re[m] * s_re[m] - a_im[m] * s_im[m] + bu_ref[re_rows, :]
            ni = a_re[m] * s_im[m] + a_im[m] * s_re[m] + bu_ref[im_rows, :]
            bu_ref[re_rows, :] = nr
            bu_ref[im_rows, :] = ni
            n_re.append(nr)
            n_im.append(ni)
        return tuple(n_re), tuple(n_im)

    s0_re = tuple(st_ref[m] for m in range(S5_STATE_VREGS))
    s0_im = tuple(st_ref[S5_STATE_VREGS + m] for m in range(S5_STATE_VREGS))
    s_re, s_im = lax.fori_loop(0, t_rows, step, (s0_re, s0_im), unroll=8)
    for m in range(S5_STATE_VREGS):
        st_ref[m] = s_re[m]
        st_ref[S5_STATE_VREGS + m] = s_im[m]

    ss = jnp.zeros((t_rows, 1), F32)
    for k in range(S5_LANE_TILES):
        cols = slice(k * LANES, (k + 1) * LANES)
        lhs = jnp.concatenate(
            [bu_ref[q * S5_PITCH:q * S5_PITCH + t_rows, :] for q in _s5_state_tiles(k)],
            axis=1).astype(BF16)
        y = jnp.dot(lhs, wc_ref[k], preferred_element_type=F32) + d_ref[:, cols] * u_ref[:, cols]
        y = _gelu(y)
        gate = jax.nn.sigmoid(jnp.dot(y.astype(BF16), wglu_ref[k], preferred_element_type=F32))
        y = y * gate
        ss = ss + jnp.sum(y * y, axis=-1, keepdims=True)
        y_ref[:, cols] = y

    scale = lax.rsqrt(ss * (1.0 / B_WIDTH) + EPS)
    o_ref[...] = (y_ref[...] * scale * og_ref[...]).astype(BF16)


def _s5(u, wb, wc, wglu, a_re, a_im, d, out_g):
    n_state_cols = 2 * S5_GROUPS_PER_TILE * S5_STATE
    return pl.pallas_call(
        _s5_kernel,
        grid=(SEQ // S5_T,),
        in_specs=[
            pl.BlockSpec((S5_T, B_WIDTH), lambda i: (i, 0)),
            pl.BlockSpec((S5_LANE_TILES, LANES, n_state_cols), lambda i: (0, 0, 0)),
            pl.BlockSpec((S5_LANE_TILES, n_state_cols, LANES), lambda i: (0, 0, 0)),
            pl.BlockSpec((S5_LANE_TILES, LANES, LANES), lambda i: (0, 0, 0)),
            pl.BlockSpec((S5_STATE_VREGS, SUBLANES, LANES), lambda i: (0, 0, 0)),
            pl.BlockSpec((S5_STATE_VREGS, SUBLANES, LANES), lambda i: (0, 0, 0)),
            pl.BlockSpec((1, B_WIDTH), lambda i: (0, 0)),
            pl.BlockSpec((1, B_WIDTH), lambda i: (0, 0)),
        ],
        out_specs=pl.BlockSpec((S5_T, B_WIDTH), lambda i: (i, 0)),
        out_shape=jax.ShapeDtypeStruct((SEQ, B_WIDTH), BF16),
        scratch_shapes=[
            pltpu.VMEM((2 * S5_STATE_VREGS, SUBLANES, LANES), F32),
            pltpu.VMEM((2 * S5_STATE_TILES * S5_PITCH, LANES), F32),
            pltpu.VMEM((S5_T, B_WIDTH), F32),
        ],
        compiler_params=pltpu.CompilerParams(
            dimension_semantics=("arbitrary",), vmem_limit_bytes=VMEM_LIMIT),
        name="s5",
    )(u, wb, wc, wglu, a_re, a_im, d, out_g)


def _s5_params(lam_re, lam_im, log_dt, b_re, b_im, c_re, c_im, w_glu):
    dt = jnp.exp(log_dt)[:, None]
    mag = jnp.exp(lam_re * dt)
    ab_r = mag * jnp.cos(lam_im * dt)
    ab_i = mag * jnp.sin(lam_im * dt)
    den = lam_re * lam_re + lam_im * lam_im
    nr = ab_r - 1.0
    ni = ab_i
    co_r = (nr * lam_re + ni * lam_im) / den
    co_i = (ni * lam_re - nr * lam_im) / den
    bb_r = co_r[..., None] * b_re - co_i[..., None] * b_im
    bb_i = co_r[..., None] * b_im + co_i[..., None] * b_re

    gt = S5_GROUPS_PER_TILE
    eye = jnp.eye(gt, dtype=F32)

    def blockdiag(blocks):
        _, r, c = blocks.shape
        b = blocks.reshape(S5_LANE_TILES, gt, r, c)
        out = jnp.einsum('kgrc,gh->kgrhc', b, eye)
        return out.reshape(S5_LANE_TILES, gt * r, gt * c)

    wb = jnp.concatenate([blockdiag(jnp.swapaxes(bb_r, 1, 2)),
                          blockdiag(jnp.swapaxes(bb_i, 1, 2))], axis=2)
    wc = jnp.concatenate([blockdiag(jnp.swapaxes(c_re, 1, 2)),
                          blockdiag(-jnp.swapaxes(c_im, 1, 2))], axis=1)
    wglu = blockdiag(w_glu)
    a_re = ab_r.reshape(S5_STATE_VREGS, SUBLANES, LANES)
    a_im = ab_i.reshape(S5_STATE_VREGS, SUBLANES, LANES)
    return wb.astype(BF16), wc.astype(BF16), wglu.astype(BF16), a_re, a_im


def _outproj_kernel(x_ref, ma_ref, mb_ref, wa_ref, wb_ref, g_ref, x1_ref, hn_ref):
    acc = jnp.dot(ma_ref[...], wa_ref[...], preferred_element_type=F32)
    acc = acc + jnp.dot(mb_ref[...], wb_ref[...], preferred_element_type=F32)
    x1 = x_ref[...] + acc
    x1_ref[...] = x1
    hn_ref[...] = (x1 * _rms_scale(x1) * g_ref[...]).astype(BF16)


def _outproj(x, mix_a, mix_b, w_out_bf16, g, tm=512):
    return pl.pallas_call(
        _outproj_kernel,
        grid=(SEQ // tm,),
        in_specs=[
            pl.BlockSpec((tm, D_MODEL), lambda i: (i, 0)),
            pl.BlockSpec((tm, A_WIDTH), lambda i: (i, 0)),
            pl.BlockSpec((tm, B_WIDTH), lambda i: (i, 0)),
            pl.BlockSpec((A_WIDTH, D_MODEL), lambda i: (0, 0)),
            pl.BlockSpec((B_WIDTH, D_MODEL), lambda i: (1, 0)),
            pl.BlockSpec((1, D_MODEL), lambda i: (0, 0)),
        ],
        out_specs=[
            pl.BlockSpec((tm, D_MODEL), lambda i: (i, 0)),
            pl.BlockSpec((tm, D_MODEL), lambda i: (i, 0)),
        ],
        out_shape=[jax.ShapeDtypeStruct((SEQ, D_MODEL), F32),
                   jax.ShapeDtypeStruct((SEQ, D_MODEL), BF16)],
        compiler_params=pltpu.CompilerParams(
            dimension_semantics=("arbitrary",), vmem_limit_bytes=VMEM_LIMIT),
        name="outproj",
    )(x, mix_a, mix_b, w_out_bf16, w_out_bf16, g)


def _ffn_kernel(hn_ref, x1_ref, wg_ref, wu_ref, wd_ref, g_ref, o_ref, acc_ref):
    f = pl.program_id(1)

    @pl.when(f == 0)
    def _():
        acc_ref[...] = x1_ref[...]

    hn = hn_ref[...]
    gate = jnp.dot(hn, wg_ref[...], preferred_element_type=F32)
    up = jnp.dot(hn, wu_ref[...], preferred_element_type=F32)
    act = (gate * jax.nn.sigmoid(gate) * up).astype(BF16)
    acc_ref[...] += jnp.dot(act, wd_ref[...], preferred_element_type=F32)

    @pl.when(f == pl.num_programs(1) - 1)
    def _():
        x2 = acc_ref[...]
        o_ref[...] = x2 * _rms_scale(x2) * g_ref[...]


def _ffn(hn, x1, wg, wu, wd, g, tm=512, tf=512):
    return pl.pallas_call(
        _ffn_kernel,
        grid=(SEQ // tm, D_FF // tf),
        in_specs=[
            pl.BlockSpec((tm, D_MODEL), lambda i, f: (i, 0)),
            pl.BlockSpec((tm, D_MODEL), lambda i, f: (i, 0)),
            pl.BlockSpec((D_MODEL, tf), lambda i, f: (0, f)),
            pl.BlockSpec((D_MODEL, tf), lambda i, f: (0, f)),
            pl.BlockSpec((tf, D_MODEL), lambda i, f: (f, 0)),
            pl.BlockSpec((1, D_MODEL), lambda i, f: (0, 0)),
        ],
        out_specs=pl.BlockSpec((tm, D_MODEL), lambda i, f: (i, 0)),
        out_shape=jax.ShapeDtypeStruct((SEQ, D_MODEL), F32),
        scratch_shapes=[pltpu.VMEM((tm, D_MODEL), F32)],
        compiler_params=pltpu.CompilerParams(
            dimension_semantics=("arbitrary", "arbitrary"),
            vmem_limit_bytes=VMEM_LIMIT),
        name="ffn",
    )(hn, x1, wg, wu, wd, g)


def kernel(x, norm_mix_g, w_in, a_ln_g, a_ln_b, a_w_s, a_b_s, s5_lambda_re, s5_lambda_im,
           s5_log_dt, s5_b_re, s5_b_im, s5_c_re, s5_c_im, s5_d, s5_w_glu, out_norm_a_g,
           out_norm_b_g, w_out, norm_ffn_g, w_gate, w_up, w_down, final_norm_g):
    assert x.shape == (1, SEQ, D_MODEL) and norm_mix_g.shape[0] == 1
    xs = x.reshape(SEQ, D_MODEL)
    l = 0

    bias_full = jnp.repeat(jnp.transpose(a_b_s[l]), A_HEAD_DIM, axis=1)
    mix_a, u = _inproj_gmlp(xs, norm_mix_g[l][None], w_in[l].astype(BF16), a_ln_g[l][None],
                            a_ln_b[l][None], a_w_s[l], bias_full, out_norm_a_g[l][None])

    wb, wc, wglu, a_re, a_im = _s5_params(
        s5_lambda_re[l], s5_lambda_im[l], s5_log_dt[l], s5_b_re[l], s5_b_im[l],
        s5_c_re[l], s5_c_im[l], s5_w_glu[l])
    mix_b = _s5(u, wb, wc, wglu, a_re, a_im, s5_d[l][None], out_norm_b_g[l][None])

    x1, hn = _outproj(xs, mix_a, mix_b, w_out[l].astype(BF16), norm_ffn_g[l][None])

    out = _ffn(hn, x1, w_gate[l].astype(BF16), w_up[l].astype(BF16),
               w_down[l].astype(BF16), final_norm_g[None])
    return out.reshape(1, SEQ, D_MODEL)
```

```python
import functools
import math

import jax
import jax.numpy as jnp
from jax import lax
from jax.experimental import pallas as pl
from jax.experimental.pallas import tpu as pltpu

F32 = jnp.float32
BF16 = jnp.bfloat16

D_MODEL = 2048
SEQ = 8192
CHUNK = 64
A_WIDTH = 1024
A_HEADS = 8
A_HEAD_DIM = 128
GMLP_BLOCK = 128
B_WIDTH = 1024
S5_GROUP_CH = 16
S5_GROUPS = 64
S5_STATE = 64
IN_WIDTH = 3072
D_FF = 5632
EPS = 1e-6

LANES = 128
SUBLANES = 8
VMEM_LIMIT = 56 * 1024 * 1024

S5_LANE_TILES = B_WIDTH // LANES
S5_GROUPS_PER_TILE = LANES // S5_GROUP_CH
S5_STATE_TILES = S5_GROUPS * S5_STATE // LANES
S5_TILES_PER_LANE_TILE = S5_STATE_TILES // S5_LANE_TILES
S5_STATE_VREGS = S5_STATE_TILES // SUBLANES
S5_T = 256
S5_PITCH = S5_T + 4


def _gelu(x):
    return 0.5 * x * (1.0 + lax.erf(x * (1.0 / math.sqrt(2.0))))


def _rms_scale(x):
    return lax.rsqrt(jnp.mean(x * x, axis=-1, keepdims=True) + EPS)


def _inproj_gmlp_kernel(x_ref, g_ref, w_ref, lng_ref, lnb_ref, ws_ref, bias_ref, og_ref,
                        wo_ref, mixa_ref, u_ref, wob_ref, sg_ref):
    rows = x_ref.shape[0]
    wob_ref[...] = wo_ref[...].astype(BF16)
    x = x_ref[...]
    xn = (x * _rms_scale(x) * g_ref[...]).astype(BF16)
    zu = _gelu(jnp.dot(xn, w_ref[:, 0:A_WIDTH], preferred_element_type=F32))
    zv = _gelu(jnp.dot(xn, w_ref[:, A_WIDTH:2 * A_WIDTH], preferred_element_type=F32))
    u_ref[...] = jnp.dot(xn, w_ref[:, 2 * A_WIDTH:], preferred_element_type=F32)

    mu = jnp.mean(zv, axis=-1, keepdims=True)
    xc = zv - mu
    var = jnp.mean(xc * xc, axis=-1, keepdims=True)
    v = (xc * lax.rsqrt(var + EPS) * lng_ref[...] + lnb_ref[...]).astype(BF16)

    ci = lax.broadcasted_iota(jnp.int32, (GMLP_BLOCK, GMLP_BLOCK), 0) // CHUNK
    cj = lax.broadcasted_iota(jnp.int32, (GMLP_BLOCK, GMLP_BLOCK), 1) // CHUNK
    mask = ci >= cj
    for h in range(A_HEADS):
        w = jnp.where(mask, ws_ref[h], 0.0).astype(BF16)
        cols = slice(h * A_HEAD_DIM, (h + 1) * A_HEAD_DIM)
        for n in range(rows // GMLP_BLOCK):
            rws = slice(n * GMLP_BLOCK, (n + 1) * GMLP_BLOCK)
            sg_ref[rws, cols] = (
                jnp.dot(w, v[rws, cols], preferred_element_type=F32) + bias_ref[:, cols])

    ya = zu * sg_ref[...]
    mixa_ref[...] = (ya * _rms_scale(ya) * og_ref[...]).astype(BF16)


def _inproj_gmlp(x, g, w_bf16, ln_g, ln_b, w_s, bias_full, out_g, w_out, tm=512):
    const2 = lambda i: (0, 0)
    n_steps = SEQ // tm
    wo_rows = w_out.shape[0] // n_steps
    return pl.pallas_call(
        _inproj_gmlp_kernel,
        grid=(n_steps,),
        in_specs=[
            pl.BlockSpec((tm, D_MODEL), lambda i: (i, 0)),
            pl.BlockSpec((1, D_MODEL), const2),
            pl.BlockSpec((D_MODEL, IN_WIDTH), const2),
            pl.BlockSpec((1, A_WIDTH), const2),
            pl.BlockSpec((1, A_WIDTH), const2),
            pl.BlockSpec((A_HEADS, GMLP_BLOCK, GMLP_BLOCK), lambda i: (0, 0, 0)),
            pl.BlockSpec((GMLP_BLOCK, A_WIDTH), const2),
            pl.BlockSpec((1, A_WIDTH), const2),
            pl.BlockSpec((wo_rows, D_MODEL), lambda i: (i, 0)),
        ],
        out_specs=[
            pl.BlockSpec((tm, A_WIDTH), lambda i: (i, 0)),
            pl.BlockSpec((tm, B_WIDTH), lambda i: (i, 0)),
            pl.BlockSpec((wo_rows, D_MODEL), lambda i: (i, 0)),
        ],
        out_shape=[jax.ShapeDtypeStruct((SEQ, A_WIDTH), BF16),
                   jax.ShapeDtypeStruct((SEQ, B_WIDTH), F32),
                   jax.ShapeDtypeStruct(w_out.shape, BF16)],
        scratch_shapes=[pltpu.VMEM((tm, A_WIDTH), F32)],
        compiler_params=pltpu.CompilerParams(
            dimension_semantics=("arbitrary",), vmem_limit_bytes=VMEM_LIMIT),
        name="inproj_gmlp",
    )(x, g, w_bf16, ln_g, ln_b, w_s, bias_full, out_g, w_out)


def _s5_state_tiles(k):
    re = [k * S5_TILES_PER_LANE_TILE + c for c in range(S5_TILES_PER_LANE_TILE)]
    return re + [S5_STATE_TILES + q for q in re]


def _s5_kernel(u_ref, wb_ref, wc_ref, wglu_ref, are_ref, aim_ref, d_ref, og_ref,
               o_ref, st_ref, bu_ref, y_ref):
    t_rows = u_ref.shape[0]

    @pl.when(pl.program_id(0) == 0)
    def _():
        st_ref[...] = jnp.zeros_like(st_ref)

    for k in range(S5_LANE_TILES):
        ub = u_ref[:, k * LANES:(k + 1) * LANES].astype(BF16)
        res = jnp.dot(ub, wb_ref[k], preferred_element_type=F32)
        for c, q in enumerate(_s5_state_tiles(k)):
            bu_ref[q * S5_PITCH:q * S5_PITCH + t_rows, :] = res[:, c * LANES:(c + 1) * LANES]

    a_re = [are_ref[m] for m in range(S5_STATE_VREGS)]
    a_im = [aim_ref[m] for m in range(S5_STATE_VREGS)]
    im_base = S5_STATE_TILES * S5_PITCH

    def step(t, carry):
        s_re, s_im = carry
        n_re, n_im = [], []
        for m in range(S5_STATE_VREGS):
            re_rows = pl.ds(t + SUBLANES * m * S5_PITCH, SUBLANES, stride=S5_PITCH)
            im_rows = pl.ds(t + im_base + SUBLANES * m * S5_PITCH, SUBLANES, stride=S5_PITCH)
            nr = a_re[m] * s_re[m] - a_im[m] * s_im[m] + bu_ref[re_rows, :]
            ni = a_re[m] * s_im[m] + a_im[m] * s_re[m] + bu_ref[im_rows, :]
            bu_ref[re_rows, :] = nr
            bu_ref[im_rows, :] = ni
            n_re.append(nr)
            n_im.append(ni)
        return tuple(n_re), tuple(n_im)

    s0_re = tuple(st_ref[m] for m in range(S5_STATE_VREGS))
    s0_im = tuple(st_ref[S5_STATE_VREGS + m] for m in range(S5_STATE_VREGS))
    s_re, s_im = lax.fori_loop(0, t_rows, step, (s0_re, s0_im), unroll=8)
    for m in range(S5_STATE_VREGS):
        st_ref[m] = s_re[m]
        st_ref[S5_STATE_VREGS + m] = s_im[m]

    ss = jnp.zeros((t_rows, 1), F32)
    for k in range(S5_LANE_TILES):
        cols = slice(k * LANES, (k + 1) * LANES)
        lhs = jnp.concatenate(
            [bu_ref[q * S5_PITCH:q * S5_PITCH + t_rows, :] for q in _s5_state_tiles(k)],
            axis=1).astype(BF16)
        y = jnp.dot(lhs, wc_ref[k], preferred_element_type=F32) + d_ref[:, cols] * u_ref[:, cols]
        y = _gelu(y)
        gate = jax.nn.sigmoid(jnp.dot(y.astype(BF16), wglu_ref[k], preferred_element_type=F32))
        y = y * gate
        ss = ss + jnp.sum(y * y, axis=-1, keepdims=True)
        y_ref[:, cols] = y

    scale = lax.rsqrt(ss * (1.0 / B_WIDTH) + EPS)
    o_ref[...] = (y_ref[...] * scale * og_ref[...]).astype(BF16)


def _s5(u, wb, wc, wglu, a_re, a_im, d, out_g):
    n_state_cols = 2 * S5_GROUPS_PER_TILE * S5_STATE
    return pl.pallas_call(
        _s5_kernel,
        grid=(SEQ // S5_T,),
        in_specs=[
            pl.BlockSpec((S5_T, B_WIDTH), lambda i: (i, 0)),
            pl.BlockSpec((S5_LANE_TILES, LANES, n_state_cols), lambda i: (0, 0, 0)),
            pl.BlockSpec((S5_LANE_TILES, n_state_cols, LANES), lambda i: (0, 0, 0)),
            pl.BlockSpec((S5_LANE_TILES, LANES, LANES), lambda i: (0, 0, 0)),
            pl.BlockSpec((S5_STATE_VREGS, SUBLANES, LANES), lambda i: (0, 0, 0)),
            pl.BlockSpec((S5_STATE_VREGS, SUBLANES, LANES), lambda i: (0, 0, 0)),
            pl.BlockSpec((1, B_WIDTH), lambda i: (0, 0)),
            pl.BlockSpec((1, B_WIDTH), lambda i: (0, 0)),
        ],
        out_specs=pl.BlockSpec((S5_T, B_WIDTH), lambda i: (i, 0)),
        out_shape=jax.ShapeDtypeStruct((SEQ, B_WIDTH), BF16),
        scratch_shapes=[
            pltpu.VMEM((2 * S5_STATE_VREGS, SUBLANES, LANES), F32),
            pltpu.VMEM((2 * S5_STATE_TILES * S5_PITCH, LANES), F32),
            pltpu.VMEM((S5_T, B_WIDTH), F32),
        ],
        compiler_params=pltpu.CompilerParams(
            dimension_semantics=("arbitrary",), vmem_limit_bytes=VMEM_LIMIT),
        name="s5",
    )(u, wb, wc, wglu, a_re, a_im, d, out_g)


def _s5_params(lam_re, lam_im, log_dt, b_re, b_im, c_re, c_im, w_glu):
    dt = jnp.exp(log_dt)[:, None]
    mag = jnp.exp(lam_re * dt)
    ab_r = mag * jnp.cos(lam_im * dt)
    ab_i = mag * jnp.sin(lam_im * dt)
    den = lam_re * lam_re + lam_im * lam_im
    nr = ab_r - 1.0
    ni = ab_i
    co_r = (nr * lam_re + ni * lam_im) / den
    co_i = (ni * lam_re - nr * lam_im) / den
    bb_r = co_r[..., None] * b_re - co_i[..., None] * b_im
    bb_i = co_r[..., None] * b_im + co_i[..., None] * b_re

    gt = S5_GROUPS_PER_TILE
    eye = jnp.eye(gt, dtype=F32)

    def blockdiag(blocks):
        _, r, c = blocks.shape
        b = blocks.reshape(S5_LANE_TILES, gt, r, c)
        out = jnp.einsum('kgrc,gh->kgrhc', b, eye)
        return out.reshape(S5_LANE_TILES, gt * r, gt * c)

    wb = jnp.concatenate([blockdiag(jnp.swapaxes(bb_r, 1, 2)),
                          blockdiag(jnp.swapaxes(bb_i, 1, 2))], axis=2)
    wc = jnp.concatenate([blockdiag(jnp.swapaxes(c_re, 1, 2)),
                          blockdiag(-jnp.swapaxes(c_im, 1, 2))], axis=1)
    wglu = blockdiag(w_glu)
    a_re = ab_r.reshape(S5_STATE_VREGS, SUBLANES, LANES)
    a_im = ab_i.reshape(S5_STATE_VREGS, SUBLANES, LANES)
    return wb.astype(BF16), wc.astype(BF16), wglu.astype(BF16), a_re, a_im


def _outproj_kernel(x_ref, ma_ref, mb_ref, wa_ref, wb_ref, g_ref, x1_ref, hn_ref):
    acc = jnp.dot(ma_ref[...], wa_ref[...], preferred_element_type=F32)
    acc = acc + jnp.dot(mb_ref[...], wb_ref[...], preferred_element_type=F32)
    x1 = x_ref[...] + acc
    x1_ref[...] = x1
    hn_ref[...] = (x1 * _rms_scale(x1) * g_ref[...]).astype(BF16)


def _outproj(x, mix_a, mix_b, w_out_bf16, g, tm=512):
    return pl.pallas_call(
        _outproj_kernel,
        grid=(SEQ // tm,),
        in_specs=[
            pl.BlockSpec((tm, D_MODEL), lambda i: (i, 0)),
            pl.BlockSpec((tm, A_WIDTH), lambda i: (i, 0)),
            pl.BlockSpec((tm, B_WIDTH), lambda i: (i, 0)),
            pl.BlockSpec((A_WIDTH, D_MODEL), lambda i: (0, 0)),
            pl.BlockSpec((B_WIDTH, D_MODEL), lambda i: (1, 0)),
            pl.BlockSpec((1, D_MODEL), lambda i: (0, 0)),
        ],
        out_specs=[
            pl.BlockSpec((tm, D_MODEL), lambda i: (i, 0)),
            pl.BlockSpec((tm, D_MODEL), lambda i: (i, 0)),
        ],
        out_shape=[jax.ShapeDtypeStruct((SEQ, D_MODEL), F32),
                   jax.ShapeDtypeStruct((SEQ, D_MODEL), BF16)],
        compiler_params=pltpu.CompilerParams(
            dimension_semantics=("arbitrary",), vmem_limit_bytes=VMEM_LIMIT),
        name="outproj",
    )(x, mix_a, mix_b, w_out_bf16, w_out_bf16, g)


def _ffn_up_kernel(hn_ref, wg_ref, wu_ref, wd_ref, act_ref, wdb_ref):
    hn = hn_ref[...]
    gate = jnp.dot(hn, wg_ref[...].astype(BF16), preferred_element_type=F32)
    up = jnp.dot(hn, wu_ref[...].astype(BF16), preferred_element_type=F32)
    act_ref[...] = (gate * jax.nn.sigmoid(gate) * up).astype(BF16)
    wdb_ref[...] = wd_ref[...].astype(BF16)


def _ffn_up(hn, wg, wu, wd, tm=1024, tf=512):
    return pl.pallas_call(
        _ffn_up_kernel,
        grid=(D_FF // tf, SEQ // tm),
        in_specs=[
            pl.BlockSpec((tm, D_MODEL), lambda f, i: (i, 0)),
            pl.BlockSpec((D_MODEL, tf), lambda f, i: (0, f)),
            pl.BlockSpec((D_MODEL, tf), lambda f, i: (0, f)),
            pl.BlockSpec((tf, D_MODEL), lambda f, i: (f, 0)),
        ],
        out_specs=[
            pl.BlockSpec((tm, tf), lambda f, i: (i, f)),
            pl.BlockSpec((tf, D_MODEL), lambda f, i: (f, 0)),
        ],
        out_shape=[jax.ShapeDtypeStruct((SEQ, D_FF), BF16),
                   jax.ShapeDtypeStruct((D_FF, D_MODEL), BF16)],
        compiler_params=pltpu.CompilerParams(
            dimension_semantics=("arbitrary", "arbitrary"),
            vmem_limit_bytes=VMEM_LIMIT),
        name="ffn_up",
    )(hn, wg, wu, wd)


def _ffn_down_kernel(act_ref, wd_ref, x1_ref, g_ref, o_ref):
    k = pl.program_id(1)

    @pl.when(k == 0)
    def _():
        o_ref[...] = x1_ref[...]

    o_ref[...] += jnp.dot(act_ref[...], wd_ref[...], preferred_element_type=F32)

    @pl.when(k == pl.num_programs(1) - 1)
    def _():
        x2 = o_ref[...]
        o_ref[...] = x2 * _rms_scale(x2) * g_ref[...]


def _ffn_down(act, wd_bf16, x1, g, tm=1024, tk=512):
    return pl.pallas_call(
        _ffn_down_kernel,
        grid=(SEQ // tm, D_FF // tk),
        in_specs=[
            pl.BlockSpec((tm, tk), lambda i, k: (i, k)),
            pl.BlockSpec((tk, D_MODEL), lambda i, k: (k, 0)),
            pl.BlockSpec((tm, D_MODEL), lambda i, k: (i, 0)),
            pl.BlockSpec((1, D_MODEL), lambda i, k: (0, 0)),
        ],
        out_specs=pl.BlockSpec((tm, D_MODEL), lambda i, k: (i, 0)),
        out_shape=jax.ShapeDtypeStruct((SEQ, D_MODEL), F32),
        compiler_params=pltpu.CompilerParams(
            dimension_semantics=("arbitrary", "arbitrary"),
            vmem_limit_bytes=VMEM_LIMIT),
        name="ffn_down",
    )(act, wd_bf16, x1, g)


def kernel(x, norm_mix_g, w_in, a_ln_g, a_ln_b, a_w_s, a_b_s, s5_lambda_re, s5_lambda_im,
           s5_log_dt, s5_b_re, s5_b_im, s5_c_re, s5_c_im, s5_d, s5_w_glu, out_norm_a_g,
           out_norm_b_g, w_out, norm_ffn_g, w_gate, w_up, w_down, final_norm_g):
    assert x.shape == (1, SEQ, D_MODEL) and norm_mix_g.shape[0] == 1
    xs = x.reshape(SEQ, D_MODEL)
    l = 0

    bias_full = jnp.repeat(jnp.transpose(a_b_s[l]), A_HEAD_DIM, axis=1)
    mix_a, u, w_out_bf16 = _inproj_gmlp(
        xs, norm_mix_g[l][None], w_in[l].astype(BF16), a_ln_g[l][None], a_ln_b[l][None],
        a_w_s[l], bias_full, out_norm_a_g[l][None], w_out[l])

    wb, wc, wglu, a_re, a_im = _s5_params(
        s5_lambda_re[l], s5_lambda_im[l], s5_log_dt[l], s5_b_re[l], s5_b_im[l],
        s5_c_re[l], s5_c_im[l], s5_w_glu[l])
    mix_b = _s5(u, wb, wc, wglu, a_re, a_im, s5_d[l][None], out_norm_b_g[l][None])

    x1, hn = _outproj(xs, mix_a, mix_b, w_out_bf16, norm_ffn_g[l][None])

    act, w_down_bf16 = _ffn_up(hn, w_gate[l], w_up[l], w_down[l])
    out = _ffn_down(act, w_down_bf16, x1, final_norm_g[None])
    return out.reshape(1, SEQ, D_MODEL)
```

```python
import math

import jax
import jax.numpy as jnp
from jax import lax
from jax.experimental import pallas as pl
from jax.experimental.pallas import tpu as pltpu

F32 = jnp.float32
BF16 = jnp.bfloat16

D_MODEL = 2048
SEQ = 8192
CHUNK = 64
A_WIDTH = 1024
A_HEADS = 8
A_HEAD_DIM = 128
GMLP_BLOCK = 128
B_WIDTH = 1024
S5_GROUP_CH = 16
S5_GROUPS = 64
S5_STATE = 64
IN_WIDTH = 3072
D_FF = 5632
EPS = 1e-6

LANES = 128
SUBLANES = 8
VMEM_LIMIT = 56 * 1024 * 1024

S5_LANE_TILES = B_WIDTH // LANES
S5_GROUPS_PER_TILE = LANES // S5_GROUP_CH
S5_STATE_TILES = S5_GROUPS * S5_STATE // LANES
S5_TILES_PER_LANE_TILE = S5_STATE_TILES // S5_LANE_TILES
S5_STATE_VREGS = S5_STATE_TILES // SUBLANES
S5_T = 256
S5_PITCH = S5_T + 4


def _gelu(x):
    return 0.5 * x * (1.0 + lax.erf(x * (1.0 / math.sqrt(2.0))))


def _rms_scale(x):
    return lax.rsqrt(jnp.mean(x * x, axis=-1, keepdims=True) + EPS)


def _s5_state_tiles(k):
    re = [k * S5_TILES_PER_LANE_TILE + c for c in range(S5_TILES_PER_LANE_TILE)]
    return re + [S5_STATE_TILES + q for q in re]


def _layernorm_bf16(z, g_ref, b_ref):
    mu = jnp.mean(z, axis=-1, keepdims=True)
    zc = z - mu
    var = jnp.mean(zc * zc, axis=-1, keepdims=True)
    return (zc * lax.rsqrt(var + EPS) * g_ref[...] + b_ref[...]).astype(BF16)


def _spatial_mix(v, ws_ref):
    n_blocks = v.shape[0] // GMLP_BLOCK
    ci = lax.broadcasted_iota(jnp.int32, (GMLP_BLOCK, GMLP_BLOCK), 0) // CHUNK
    cj = lax.broadcasted_iota(jnp.int32, (GMLP_BLOCK, GMLP_BLOCK), 1) // CHUNK
    mask = ci >= cj
    heads = []
    for h in range(A_HEADS):
        w = jnp.where(mask, ws_ref[h], 0.0).astype(BF16)
        cols = slice(h * A_HEAD_DIM, (h + 1) * A_HEAD_DIM)
        rhs = jnp.concatenate(
            [v[n * GMLP_BLOCK:(n + 1) * GMLP_BLOCK, cols] for n in range(n_blocks)], axis=1)
        heads.append(jnp.dot(w, rhs, preferred_element_type=F32))
    return jnp.concatenate(
        [jnp.concatenate([hd[:, n * A_HEAD_DIM:(n + 1) * A_HEAD_DIM] for hd in heads], axis=1)
         for n in range(n_blocks)], axis=0)


def _mixer_kernel(x_ref, g_ref, w_ref, lng_ref, lnb_ref, ws_ref, bias_ref, oga_ref, wo_ref,
                  wb_ref, wc_ref, wglu_ref, are_ref, aim_ref, d_ref, ogb_ref,
                  mixa_ref, mixb_ref, wob_ref, st_ref, bu_ref):
    t_rows = x_ref.shape[0]

    @pl.when(pl.program_id(0) == 0)
    def _():
        st_ref[...] = jnp.zeros_like(st_ref)

    wob_ref[...] = wo_ref[...].astype(BF16)

    x = x_ref[...]
    xn = (x * _rms_scale(x) * g_ref[...]).astype(BF16)

    u = jnp.dot(xn, w_ref[:, 2 * A_WIDTH:], preferred_element_type=F32)
    for k in range(S5_LANE_TILES):
        ub = u[:, k * LANES:(k + 1) * LANES].astype(BF16)
        res = jnp.dot(ub, wb_ref[k], preferred_element_type=F32)
        for c, q in enumerate(_s5_state_tiles(k)):
            bu_ref[q * S5_PITCH:q * S5_PITCH + t_rows, :] = res[:, c * LANES:(c + 1) * LANES]

    hu = jnp.dot(xn, w_ref[:, 0:A_WIDTH], preferred_element_type=F32)
    hv = jnp.dot(xn, w_ref[:, A_WIDTH:2 * A_WIDTH], preferred_element_type=F32)

    a_re = [are_ref[m] for m in range(S5_STATE_VREGS)]
    a_im = [aim_ref[m] for m in range(S5_STATE_VREGS)]
    s_re = [st_ref[m] for m in range(S5_STATE_VREGS)]
    s_im = [st_ref[S5_STATE_VREGS + m] for m in range(S5_STATE_VREGS)]
    im_base = S5_STATE_TILES * S5_PITCH
    for t in range(t_rows):
        for m in range(S5_STATE_VREGS):
            re_rows = pl.ds(t + SUBLANES * m * S5_PITCH, SUBLANES, stride=S5_PITCH)
            im_rows = pl.ds(t + im_base + SUBLANES * m * S5_PITCH, SUBLANES, stride=S5_PITCH)
            nr = a_re[m] * s_re[m] - a_im[m] * s_im[m] + bu_ref[re_rows, :]
            ni = a_re[m] * s_im[m] + a_im[m] * s_re[m] + bu_ref[im_rows, :]
            bu_ref[re_rows, :] = nr
            bu_ref[im_rows, :] = ni
            s_re[m], s_im[m] = nr, ni
    for m in range(S5_STATE_VREGS):
        st_ref[m] = s_re[m]
        st_ref[S5_STATE_VREGS + m] = s_im[m]

    yc = []
    for k in range(S5_LANE_TILES):
        lhs = jnp.concatenate(
            [bu_ref[q * S5_PITCH:q * S5_PITCH + t_rows, :] for q in _s5_state_tiles(k)],
            axis=1).astype(BF16)
        yc.append(jnp.dot(lhs, wc_ref[k], preferred_element_type=F32))

    zu = _gelu(hu)
    v = _layernorm_bf16(_gelu(hv), lng_ref, lnb_ref)

    ys = []
    for k in range(S5_LANE_TILES):
        cols = slice(k * LANES, (k + 1) * LANES)
        ys.append(_gelu(yc[k] + d_ref[:, cols] * u[:, cols]))
    gates = [jnp.dot(ys[k].astype(BF16), wglu_ref[k], preferred_element_type=F32)
             for k in range(S5_LANE_TILES)]
    yb = jnp.concatenate([ys[k] * jax.nn.sigmoid(gates[k]) for k in range(S5_LANE_TILES)], axis=1)
    mixb_ref[...] = (yb * _rms_scale(yb) * ogb_ref[...]).astype(BF16)

    bias = jnp.concatenate([bias_ref[...]] * (t_rows // GMLP_BLOCK), axis=0)
    ya = zu * (_spatial_mix(v, ws_ref) + bias)
    mixa_ref[...] = (ya * _rms_scale(ya) * oga_ref[...]).astype(BF16)


def _mixer(x, g, w_bf16, ln_g, ln_b, w_s, bias_full, out_a_g, w_out,
           wb, wc, wglu, a_re, a_im, d, out_b_g):
    tm = S5_T
    n_steps = SEQ // tm
    wo_rows = w_out.shape[0] // n_steps
    n_state_cols = 2 * S5_GROUPS_PER_TILE * S5_STATE
    const2 = lambda i: (0, 0)
    const3 = lambda i: (0, 0, 0)
    once = dict(pipeline_mode=pl.Buffered(1))
    return pl.pallas_call(
        _mixer_kernel,
        grid=(n_steps,),
        in_specs=[
            pl.BlockSpec((tm, D_MODEL), lambda i: (i, 0)),
            pl.BlockSpec((1, D_MODEL), const2),
            pl.BlockSpec((D_MODEL, IN_WIDTH), const2, **once),
            pl.BlockSpec((1, A_WIDTH), const2),
            pl.BlockSpec((1, A_WIDTH), const2),
            pl.BlockSpec((A_HEADS, GMLP_BLOCK, GMLP_BLOCK), const3),
            pl.BlockSpec((GMLP_BLOCK, A_WIDTH), const2),
            pl.BlockSpec((1, A_WIDTH), const2),
            pl.BlockSpec((wo_rows, D_MODEL), lambda i: (i, 0)),
            pl.BlockSpec((S5_LANE_TILES, LANES, n_state_cols), const3, **once),
            pl.BlockSpec((S5_LANE_TILES, n_state_cols, LANES), const3, **once),
            pl.BlockSpec((S5_LANE_TILES, LANES, LANES), const3),
            pl.BlockSpec((S5_STATE_VREGS, SUBLANES, LANES), const3),
            pl.BlockSpec((S5_STATE_VREGS, SUBLANES, LANES), const3),
            pl.BlockSpec((1, B_WIDTH), const2),
            pl.BlockSpec((1, B_WIDTH), const2),
        ],
        out_specs=[
            pl.BlockSpec((tm, A_WIDTH), lambda i: (i, 0)),
            pl.BlockSpec((tm, B_WIDTH), lambda i: (i, 0)),
            pl.BlockSpec((wo_rows, D_MODEL), lambda i: (i, 0)),
        ],
        out_shape=[jax.ShapeDtypeStruct((SEQ, A_WIDTH), BF16),
                   jax.ShapeDtypeStruct((SEQ, B_WIDTH), BF16),
                   jax.ShapeDtypeStruct(w_out.shape, BF16)],
        scratch_shapes=[
            pltpu.VMEM((2 * S5_STATE_VREGS, SUBLANES, LANES), F32),
            pltpu.VMEM((2 * S5_STATE_TILES * S5_PITCH, LANES), F32),
        ],
        compiler_params=pltpu.CompilerParams(
            dimension_semantics=("arbitrary",), vmem_limit_bytes=VMEM_LIMIT),
        name="mixer",
    )(x, g, w_bf16, ln_g, ln_b, w_s, bias_full, out_a_g, w_out,
      wb, wc, wglu, a_re, a_im, d, out_b_g)


def _s5_params(lam_re, lam_im, log_dt, b_re, b_im, c_re, c_im, w_glu):
    dt = jnp.exp(log_dt)[:, None]
    mag = jnp.exp(lam_re * dt)
    ab_r = mag * jnp.cos(lam_im * dt)
    ab_i = mag * jnp.sin(lam_im * dt)
    den = lam_re * lam_re + lam_im * lam_im
    nr = ab_r - 1.0
    ni = ab_i
    co_r = (nr * lam_re + ni * lam_im) / den
    co_i = (ni * lam_re - nr * lam_im) / den
    bb_r = co_r[..., None] * b_re - co_i[..., None] * b_im
    bb_i = co_r[..., None] * b_im + co_i[..., None] * b_re

    gt = S5_GROUPS_PER_TILE
    eye = jnp.eye(gt, dtype=F32)

    def blockdiag(blocks):
        _, r, c = blocks.shape
        b = blocks.reshape(S5_LANE_TILES, gt, r, c)
        out = jnp.einsum('kgrc,gh->kgrhc', b, eye)
        return out.reshape(S5_LANE_TILES, gt * r, gt * c)

    wb = jnp.concatenate([blockdiag(jnp.swapaxes(bb_r, 1, 2)),
                          blockdiag(jnp.swapaxes(bb_i, 1, 2))], axis=2)
    wc = jnp.concatenate([blockdiag(jnp.swapaxes(c_re, 1, 2)),
                          blockdiag(-jnp.swapaxes(c_im, 1, 2))], axis=1)
    wglu = blockdiag(w_glu)
    a_re = ab_r.reshape(S5_STATE_VREGS, SUBLANES, LANES)
    a_im = ab_i.reshape(S5_STATE_VREGS, SUBLANES, LANES)
    return wb.astype(BF16), wc.astype(BF16), wglu.astype(BF16), a_re, a_im


def _outproj_kernel(x_ref, ma_ref, mb_ref, wa_ref, wb_ref, g_ref, x1_ref, hn_ref):
    acc = jnp.dot(ma_ref[...], wa_ref[...], preferred_element_type=F32)
    acc = acc + jnp.dot(mb_ref[...], wb_ref[...], preferred_element_type=F32)
    x1 = x_ref[...] + acc
    x1_ref[...] = x1
    hn_ref[...] = (x1 * _rms_scale(x1) * g_ref[...]).astype(BF16)


def _outproj(x, mix_a, mix_b, w_out_bf16, g, tm=512):
    return pl.pallas_call(
        _outproj_kernel,
        grid=(SEQ // tm,),
        in_specs=[
            pl.BlockSpec((tm, D_MODEL), lambda i: (i, 0)),
            pl.BlockSpec((tm, A_WIDTH), lambda i: (i, 0)),
            pl.BlockSpec((tm, B_WIDTH), lambda i: (i, 0)),
            pl.BlockSpec((A_WIDTH, D_MODEL), lambda i: (0, 0)),
            pl.BlockSpec((B_WIDTH, D_MODEL), lambda i: (1, 0)),
            pl.BlockSpec((1, D_MODEL), lambda i: (0, 0)),
        ],
        out_specs=[
            pl.BlockSpec((tm, D_MODEL), lambda i: (i, 0)),
            pl.BlockSpec((tm, D_MODEL), lambda i: (i, 0)),
        ],
        out_shape=[jax.ShapeDtypeStruct((SEQ, D_MODEL), F32),
                   jax.ShapeDtypeStruct((SEQ, D_MODEL), BF16)],
        compiler_params=pltpu.CompilerParams(
            dimension_semantics=("arbitrary",), vmem_limit_bytes=VMEM_LIMIT),
        name="outproj",
    )(x, mix_a, mix_b, w_out_bf16, w_out_bf16, g)


def _ffn_up_kernel(hn_ref, wg_ref, wu_ref, wd_ref, act_ref, wdb_ref):
    hn = hn_ref[...]
    gate = jnp.dot(hn, wg_ref[...].astype(BF16), preferred_element_type=F32)
    up = jnp.dot(hn, wu_ref[...].astype(BF16), preferred_element_type=F32)
    act_ref[...] = (gate * jax.nn.sigmoid(gate) * up).astype(BF16)
    wdb_ref[...] = wd_ref[...].astype(BF16)


def _ffn_up(hn, wg, wu, wd, tm=1024, tf=512):
    return pl.pallas_call(
        _ffn_up_kernel,
        grid=(D_FF // tf, SEQ // tm),
        in_specs=[
            pl.BlockSpec((tm, D_MODEL), lambda f, i: (i, 0)),
            pl.BlockSpec((D_MODEL, tf), lambda f, i: (0, f)),
            pl.BlockSpec((D_MODEL, tf), lambda f, i: (0, f)),
            pl.BlockSpec((tf, D_MODEL), lambda f, i: (f, 0)),
        ],
        out_specs=[
            pl.BlockSpec((tm, tf), lambda f, i: (i, f)),
            pl.BlockSpec((tf, D_MODEL), lambda f, i: (f, 0)),
        ],
        out_shape=[jax.ShapeDtypeStruct((SEQ, D_FF), BF16),
                   jax.ShapeDtypeStruct((D_FF, D_MODEL), BF16)],
        compiler_params=pltpu.CompilerParams(
            dimension_semantics=("arbitrary", "arbitrary"),
            vmem_limit_bytes=VMEM_LIMIT),
        name="ffn_up",
    )(hn, wg, wu, wd)


def _ffn_down_kernel(act_ref, wd_ref, x1_ref, g_ref, o_ref):
    k = pl.program_id(1)

    @pl.when(k == 0)
    def _():
        o_ref[...] = x1_ref[...]

    o_ref[...] += jnp.dot(act_ref[...], wd_ref[...], preferred_element_type=F32)

    @pl.when(k == pl.num_programs(1) - 1)
    def _():
        x2 = o_ref[...]
        o_ref[...] = x2 * _rms_scale(x2) * g_ref[...]


def _ffn_down(act, wd_bf16, x1, g, tm=1024, tk=512):
    return pl.pallas_call(
        _ffn_down_kernel,
        grid=(SEQ // tm, D_FF // tk),
        in_specs=[
            pl.BlockSpec((tm, tk), lambda i, k: (i, k)),
            pl.BlockSpec((tk, D_MODEL), lambda i, k: (k, 0)),
            pl.BlockSpec((tm, D_MODEL), lambda i, k: (i, 0)),
            pl.BlockSpec((1, D_MODEL), lambda i, k: (0, 0)),
        ],
        out_specs=pl.BlockSpec((tm, D_MODEL), lambda i, k: (i, 0)),
        out_shape=jax.ShapeDtypeStruct((SEQ, D_MODEL), F32),
        compiler_params=pltpu.CompilerParams(
            dimension_semantics=("arbitrary", "arbitrary"),
            vmem_limit_bytes=VMEM_LIMIT),
        name="ffn_down",
    )(act, wd_bf16, x1, g)


def kernel(x, norm_mix_g, w_in, a_ln_g, a_ln_b, a_w_s, a_b_s, s5_lambda_re, s5_lambda_im,
           s5_log_dt, s5_b_re, s5_b_im, s5_c_re, s5_c_im, s5_d, s5_w_glu, out_norm_a_g,
           out_norm_b_g, w_out, norm_ffn_g, w_gate, w_up, w_down, final_norm_g):
    assert x.shape == (1, SEQ, D_MODEL) and norm_mix_g.shape[0] == 1
    xs = x.reshape(SEQ, D_MODEL)
    l = 0

    bias_full = jnp.repeat(jnp.transpose(a_b_s[l]), A_HEAD_DIM, axis=1)
    wb, wc, wglu, a_re, a_im = _s5_params(
        s5_lambda_re[l], s5_lambda_im[l], s5_log_dt[l], s5_b_re[l], s5_b_im[l],
        s5_c_re[l], s5_c_im[l], s5_w_glu[l])
    mix_a, mix_b, w_out_bf16 = _mixer(
        xs, norm_mix_g[l][None], w_in[l].astype(BF16), a_ln_g[l][None], a_ln_b[l][None],
        a_w_s[l], bias_full, out_norm_a_g[l][None], w_out[l],
        wb, wc, wglu, a_re, a_im, s5_d[l][None], out_norm_b_g[l][None])

    x1, hn = _outproj(xs, mix_a, mix_b, w_out_bf16, norm_ffn_g[l][None])

    act, w_down_bf16 = _ffn_up(hn, w_gate[l], w_up[l], w_down[l])
    out = _ffn_down(act, w_down_bf16, x1, final_norm_g[None])
    return out.reshape(1, SEQ, D_MODEL)
```

```python
import math

import jax
import jax.numpy as jnp
from jax import lax
from jax.experimental import pallas as pl
from jax.experimental.pallas import tpu as pltpu

F32 = jnp.float32
BF16 = jnp.bfloat16

D_MODEL = 2048
SEQ = 8192
CHUNK = 64
A_WIDTH = 1024
A_HEADS = 8
A_HEAD_DIM = 128
GMLP_BLOCK = 128
B_WIDTH = 1024
S5_GROUP_CH = 16
S5_GROUPS = 64
S5_STATE = 64
IN_WIDTH = 3072
D_FF = 5632
EPS = 1e-6

LANES = 128
SUBLANES = 8
VMEM_LIMIT = 56 * 1024 * 1024

S5_LANE_TILES = B_WIDTH // LANES
S5_GROUPS_PER_TILE = LANES // S5_GROUP_CH
S5_STATE_TILES = S5_GROUPS * S5_STATE // LANES
S5_TILES_PER_LANE_TILE = S5_STATE_TILES // S5_LANE_TILES
S5_STATE_VREGS = S5_STATE_TILES // SUBLANES
S5_T = 256
S5_PITCH = S5_T + 4


def _gelu(x):
    return 0.5 * x * (1.0 + lax.erf(x * (1.0 / math.sqrt(2.0))))


def _rms_scale(x):
    return lax.rsqrt(jnp.mean(x * x, axis=-1, keepdims=True) + EPS)


def _s5_state_tiles(k):
    re = [k * S5_TILES_PER_LANE_TILE + c for c in range(S5_TILES_PER_LANE_TILE)]
    return re + [S5_STATE_TILES + q for q in re]


def _layernorm_bf16(z, g_ref, b_ref):
    mu = jnp.mean(z, axis=-1, keepdims=True)
    zc = z - mu
    var = jnp.mean(zc * zc, axis=-1, keepdims=True)
    return (zc * lax.rsqrt(var + EPS) * g_ref[...] + b_ref[...]).astype(BF16)


def _spatial_mix(v, ws_ref):
    n_blocks = v.shape[0] // GMLP_BLOCK
    ci = lax.broadcasted_iota(jnp.int32, (GMLP_BLOCK, GMLP_BLOCK), 0) // CHUNK
    cj = lax.broadcasted_iota(jnp.int32, (GMLP_BLOCK, GMLP_BLOCK), 1) // CHUNK
    mask = ci >= cj
    heads = []
    for h in range(A_HEADS):
        w = jnp.where(mask, ws_ref[h], 0.0).astype(BF16)
        cols = slice(h * A_HEAD_DIM, (h + 1) * A_HEAD_DIM)
        rhs = jnp.concatenate(
            [v[n * GMLP_BLOCK:(n + 1) * GMLP_BLOCK, cols] for n in range(n_blocks)], axis=1)
        heads.append(jnp.dot(w, rhs, preferred_element_type=F32))
    return jnp.concatenate(
        [jnp.concatenate([hd[:, n * A_HEAD_DIM:(n + 1) * A_HEAD_DIM] for hd in heads], axis=1)
         for n in range(n_blocks)], axis=0)


def _mixer_kernel(x_ref, g_ref, w_ref, lng_ref, lnb_ref, ws_ref, bias_ref, oga_ref, wo_ref,
                  wd_ref, wb_ref, wc_ref, wglu_ref, are_ref, aim_ref, d_ref, ogb_ref,
                  mixa_ref, mixb_ref, wob_ref, wdb_ref, st_ref, bu_ref):
    t_rows = x_ref.shape[0]

    @pl.when(pl.program_id(0) == 0)
    def _():
        st_ref[...] = jnp.zeros_like(st_ref)

    wob_ref[...] = wo_ref[...].astype(BF16)
    wdb_ref[...] = wd_ref[...].astype(BF16)

    x = x_ref[...]
    xn = (x * _rms_scale(x) * g_ref[...]).astype(BF16)

    u = jnp.dot(xn, w_ref[:, 2 * A_WIDTH:], preferred_element_type=F32)
    for k in range(S5_LANE_TILES):
        ub = u[:, k * LANES:(k + 1) * LANES].astype(BF16)
        res = jnp.dot(ub, wb_ref[k], preferred_element_type=F32)
        for c, q in enumerate(_s5_state_tiles(k)):
            bu_ref[q * S5_PITCH:q * S5_PITCH + t_rows, :] = res[:, c * LANES:(c + 1) * LANES]

    hu = jnp.dot(xn, w_ref[:, 0:A_WIDTH], preferred_element_type=F32)
    hv = jnp.dot(xn, w_ref[:, A_WIDTH:2 * A_WIDTH], preferred_element_type=F32)

    a_re = [are_ref[m] for m in range(S5_STATE_VREGS)]
    a_im = [aim_ref[m] for m in range(S5_STATE_VREGS)]
    s_re = [st_ref[m] for m in range(S5_STATE_VREGS)]
    s_im = [st_ref[S5_STATE_VREGS + m] for m in range(S5_STATE_VREGS)]
    im_base = S5_STATE_TILES * S5_PITCH
    for t in range(t_rows):
        for m in range(S5_STATE_VREGS):
            re_rows = pl.ds(t + SUBLANES * m * S5_PITCH, SUBLANES, stride=S5_PITCH)
            im_rows = pl.ds(t + im_base + SUBLANES * m * S5_PITCH, SUBLANES, stride=S5_PITCH)
            nr = a_re[m] * s_re[m] - a_im[m] * s_im[m] + bu_ref[re_rows, :]
            ni = a_re[m] * s_im[m] + a_im[m] * s_re[m] + bu_ref[im_rows, :]
            bu_ref[re_rows, :] = nr
            bu_ref[im_rows, :] = ni
            s_re[m], s_im[m] = nr, ni
    for m in range(S5_STATE_VREGS):
        st_ref[m] = s_re[m]
        st_ref[S5_STATE_VREGS + m] = s_im[m]

    yc = []
    for k in range(S5_LANE_TILES):
        lhs = jnp.concatenate(
            [bu_ref[q * S5_PITCH:q * S5_PITCH + t_rows, :] for q in _s5_state_tiles(k)],
            axis=1).astype(BF16)
        yc.append(jnp.dot(lhs, wc_ref[k], preferred_element_type=F32))

    zu = _gelu(hu)
    v = _layernorm_bf16(_gelu(hv), lng_ref, lnb_ref)

    ys = []
    for k in range(S5_LANE_TILES):
        cols = slice(k * LANES, (k + 1) * LANES)
        ys.append(_gelu(yc[k] + d_ref[:, cols] * u[:, cols]))
    gates = [jnp.dot(ys[k].astype(BF16), wglu_ref[k], preferred_element_type=F32)
             for k in range(S5_LANE_TILES)]
    yb = jnp.concatenate([ys[k] * jax.nn.sigmoid(gates[k]) for k in range(S5_LANE_TILES)], axis=1)
    mixb_ref[...] = (yb * _rms_scale(yb) * ogb_ref[...]).astype(BF16)

    bias = jnp.concatenate([bias_ref[...]] * (t_rows // GMLP_BLOCK), axis=0)
    ya = zu * (_spatial_mix(v, ws_ref) + bias)
    mixa_ref[...] = (ya * _rms_scale(ya) * oga_ref[...]).astype(BF16)


def _mixer(x, g, w_bf16, ln_g, ln_b, w_s, bias_full, out_a_g, w_out, w_down,
           wb, wc, wglu, a_re, a_im, d, out_b_g):
    tm = S5_T
    n_steps = SEQ // tm
    wo_rows = w_out.shape[0] // n_steps
    wd_rows = w_down.shape[0] // n_steps
    n_state_cols = 2 * S5_GROUPS_PER_TILE * S5_STATE
    const2 = lambda i: (0, 0)
    const3 = lambda i: (0, 0, 0)
    once = dict(pipeline_mode=pl.Buffered(1))
    return pl.pallas_call(
        _mixer_kernel,
        grid=(n_steps,),
        in_specs=[
            pl.BlockSpec((tm, D_MODEL), lambda i: (i, 0)),
            pl.BlockSpec((1, D_MODEL), const2),
            pl.BlockSpec((D_MODEL, IN_WIDTH), const2, **once),
            pl.BlockSpec((1, A_WIDTH), const2),
            pl.BlockSpec((1, A_WIDTH), const2),
            pl.BlockSpec((A_HEADS, GMLP_BLOCK, GMLP_BLOCK), const3),
            pl.BlockSpec((GMLP_BLOCK, A_WIDTH), const2),
            pl.BlockSpec((1, A_WIDTH), const2),
            pl.BlockSpec((wo_rows, D_MODEL), lambda i: (i, 0)),
            pl.BlockSpec((wd_rows, D_MODEL), lambda i: (i, 0)),
            pl.BlockSpec((S5_LANE_TILES, LANES, n_state_cols), const3, **once),
            pl.BlockSpec((S5_LANE_TILES, n_state_cols, LANES), const3, **once),
            pl.BlockSpec((S5_LANE_TILES, LANES, LANES), const3),
            pl.BlockSpec((S5_STATE_VREGS, SUBLANES, LANES), const3),
            pl.BlockSpec((S5_STATE_VREGS, SUBLANES, LANES), const3),
            pl.BlockSpec((1, B_WIDTH), const2),
            pl.BlockSpec((1, B_WIDTH), const2),
        ],
        out_specs=[
            pl.BlockSpec((tm, A_WIDTH), lambda i: (i, 0)),
            pl.BlockSpec((tm, B_WIDTH), lambda i: (i, 0)),
            pl.BlockSpec((wo_rows, D_MODEL), lambda i: (i, 0)),
            pl.BlockSpec((wd_rows, D_MODEL), lambda i: (i, 0)),
        ],
        out_shape=[jax.ShapeDtypeStruct((SEQ, A_WIDTH), BF16),
                   jax.ShapeDtypeStruct((SEQ, B_WIDTH), BF16),
                   jax.ShapeDtypeStruct(w_out.shape, BF16),
                   jax.ShapeDtypeStruct(w_down.shape, BF16)],
        scratch_shapes=[
            pltpu.VMEM((2 * S5_STATE_VREGS, SUBLANES, LANES), F32),
            pltpu.VMEM((2 * S5_STATE_TILES * S5_PITCH, LANES), F32),
        ],
        compiler_params=pltpu.CompilerParams(
            dimension_semantics=("arbitrary",), vmem_limit_bytes=VMEM_LIMIT),
        name="mixer",
    )(x, g, w_bf16, ln_g, ln_b, w_s, bias_full, out_a_g, w_out, w_down,
      wb, wc, wglu, a_re, a_im, d, out_b_g)


def _s5_params(lam_re, lam_im, log_dt, b_re, b_im, c_re, c_im, w_glu):
    dt = jnp.exp(log_dt)[:, None]
    mag = jnp.exp(lam_re * dt)
    ab_r = mag * jnp.cos(lam_im * dt)
    ab_i = mag * jnp.sin(lam_im * dt)
    den = lam_re * lam_re + lam_im * lam_im
    nr = ab_r - 1.0
    ni = ab_i
    co_r = (nr * lam_re + ni * lam_im) / den
    co_i = (ni * lam_re - nr * lam_im) / den
    bb_r = co_r[..., None] * b_re - co_i[..., None] * b_im
    bb_i = co_r[..., None] * b_im + co_i[..., None] * b_re

    gt = S5_GROUPS_PER_TILE
    eye = jnp.eye(gt, dtype=F32)

    def blockdiag(blocks):
        _, r, c = blocks.shape
        b = blocks.reshape(S5_LANE_TILES, gt, r, c)
        out = jnp.einsum('kgrc,gh->kgrhc', b, eye)
        return out.reshape(S5_LANE_TILES, gt * r, gt * c)

    wb = jnp.concatenate([blockdiag(jnp.swapaxes(bb_r, 1, 2)),
                          blockdiag(jnp.swapaxes(bb_i, 1, 2))], axis=2)
    wc = jnp.concatenate([blockdiag(jnp.swapaxes(c_re, 1, 2)),
                          blockdiag(-jnp.swapaxes(c_im, 1, 2))], axis=1)
    wglu = blockdiag(w_glu)
    a_re = ab_r.reshape(S5_STATE_VREGS, SUBLANES, LANES)
    a_im = ab_i.reshape(S5_STATE_VREGS, SUBLANES, LANES)
    return wb.astype(BF16), wc.astype(BF16), wglu.astype(BF16), a_re, a_im


def _outproj_kernel(x_ref, ma_ref, mb_ref, wa_ref, wb_ref, g_ref, x1_ref, hn_ref):
    acc = jnp.dot(ma_ref[...], wa_ref[...], preferred_element_type=F32)
    acc = acc + jnp.dot(mb_ref[...], wb_ref[...], preferred_element_type=F32)
    x1 = x_ref[...] + acc
    x1_ref[...] = x1
    hn_ref[...] = (x1 * _rms_scale(x1) * g_ref[...]).astype(BF16)


def _outproj(x, mix_a, mix_b, w_out_bf16, g, tm=512):
    return pl.pallas_call(
        _outproj_kernel,
        grid=(SEQ // tm,),
        in_specs=[
            pl.BlockSpec((tm, D_MODEL), lambda i: (i, 0)),
            pl.BlockSpec((tm, A_WIDTH), lambda i: (i, 0)),
            pl.BlockSpec((tm, B_WIDTH), lambda i: (i, 0)),
            pl.BlockSpec((A_WIDTH, D_MODEL), lambda i: (0, 0)),
            pl.BlockSpec((B_WIDTH, D_MODEL), lambda i: (1, 0)),
            pl.BlockSpec((1, D_MODEL), lambda i: (0, 0)),
        ],
        out_specs=[
            pl.BlockSpec((tm, D_MODEL), lambda i: (i, 0)),
            pl.BlockSpec((tm, D_MODEL), lambda i: (i, 0)),
        ],
        out_shape=[jax.ShapeDtypeStruct((SEQ, D_MODEL), F32),
                   jax.ShapeDtypeStruct((SEQ, D_MODEL), BF16)],
        compiler_params=pltpu.CompilerParams(
            dimension_semantics=("arbitrary",), vmem_limit_bytes=VMEM_LIMIT),
        name="outproj",
    )(x, mix_a, mix_b, w_out_bf16, w_out_bf16, g)


def _ffn_up_kernel(hn_ref, wg_ref, wu_ref, act_ref):
    hn = hn_ref[...]
    gate = jnp.dot(hn, wg_ref[...].astype(BF16), preferred_element_type=F32)
    up = jnp.dot(hn, wu_ref[...].astype(BF16), preferred_element_type=F32)
    act_ref[...] = (gate * jax.nn.sigmoid(gate) * up).astype(BF16)


def _ffn_up(hn, wg, wu, tm=2048, tf=512):
    return pl.pallas_call(
        _ffn_up_kernel,
        grid=(D_FF // tf, SEQ // tm),
        in_specs=[
            pl.BlockSpec((tm, D_MODEL), lambda f, i: (i, 0)),
            pl.BlockSpec((D_MODEL, tf), lambda f, i: (0, f)),
            pl.BlockSpec((D_MODEL, tf), lambda f, i: (0, f)),
        ],
        out_specs=pl.BlockSpec((tm, tf), lambda f, i: (i, f)),
        out_shape=jax.ShapeDtypeStruct((SEQ, D_FF), BF16),
        compiler_params=pltpu.CompilerParams(
            dimension_semantics=("arbitrary", "arbitrary"),
            vmem_limit_bytes=VMEM_LIMIT),
        name="ffn_up",
    )(hn, wg, wu)


def _ffn_down_kernel(act_ref, wd_ref, x1_ref, g_ref, o_ref):
    k = pl.program_id(1)

    @pl.when(k == 0)
    def _():
        o_ref[...] = x1_ref[...]

    o_ref[...] += jnp.dot(act_ref[...], wd_ref[...], preferred_element_type=F32)

    @pl.when(k == pl.num_programs(1) - 1)
    def _():
        x2 = o_ref[...]
        o_ref[...] = x2 * _rms_scale(x2) * g_ref[...]


def _ffn_down(act, wd_bf16, x1, g, tm=1024, tk=512):
    return pl.pallas_call(
        _ffn_down_kernel,
        grid=(SEQ // tm, D_FF // tk),
        in_specs=[
            pl.BlockSpec((tm, tk), lambda i, k: (i, k)),
            pl.BlockSpec((tk, D_MODEL), lambda i, k: (k, 0)),
            pl.BlockSpec((tm, D_MODEL), lambda i, k: (i, 0)),
            pl.BlockSpec((1, D_MODEL), lambda i, k: (0, 0)),
        ],
        out_specs=pl.BlockSpec((tm, D_MODEL), lambda i, k: (i, 0)),
        out_shape=jax.ShapeDtypeStruct((SEQ, D_MODEL), F32),
        compiler_params=pltpu.CompilerParams(
            dimension_semantics=("arbitrary", "arbitrary"),
            vmem_limit_bytes=VMEM_LIMIT),
        name="ffn_down",
    )(act, wd_bf16, x1, g)


def kernel(x, norm_mix_g, w_in, a_ln_g, a_ln_b, a_w_s, a_b_s, s5_lambda_re, s5_lambda_im,
           s5_log_dt, s5_b_re, s5_b_im, s5_c_re, s5_c_im, s5_d, s5_w_glu, out_norm_a_g,
           out_norm_b_g, w_out, norm_ffn_g, w_gate, w_up, w_down, final_norm_g):
    assert x.shape == (1, SEQ, D_MODEL) and norm_mix_g.shape[0] == 1
    xs = x.reshape(SEQ, D_MODEL)
    l = 0

    bias_full = jnp.repeat(jnp.transpose(a_b_s[l]), A_HEAD_DIM, axis=1)
    wb, wc, wglu, a_re, a_im = _s5_params(
        s5_lambda_re[l], s5_lambda_im[l], s5_log_dt[l], s5_b_re[l], s5_b_im[l],
        s5_c_re[l], s5_c_im[l], s5_w_glu[l])
    mix_a, mix_b, w_out_bf16, w_down_bf16 = _mixer(
        xs, norm_mix_g[l][None], w_in[l].astype(BF16), a_ln_g[l][None], a_ln_b[l][None],
        a_w_s[l], bias_full, out_norm_a_g[l][None], w_out[l], w_down[l],
        wb, wc, wglu, a_re, a_im, s5_d[l][None], out_norm_b_g[l][None])

    x1, hn = _outproj(xs, mix_a, mix_b, w_out_bf16, norm_ffn_g[l][None])

    act = _ffn_up(hn, w_gate[l], w_up[l])
    out = _ffn_down(act, w_down_bf16, x1, final_norm_g[None])
    return out.reshape(1, SEQ, D_MODEL)
```

```python
import math

import jax
import jax.numpy as jnp
from jax import lax
from jax.experimental import pallas as pl
from jax.experimental.pallas import tpu as pltpu

F32 = jnp.float32
BF16 = jnp.bfloat16

D_MODEL = 2048
SEQ = 8192
CHUNK = 64
A_WIDTH = 1024
A_HEADS = 8
A_HEAD_DIM = 128
GMLP_BLOCK = 128
B_WIDTH = 1024
S5_GROUP_CH = 16
S5_GROUPS = 64
S5_STATE = 64
IN_WIDTH = 3072
D_FF = 5632
EPS = 1e-6

LANES = 128
SUBLANES = 8
VMEM_LIMIT = 56 * 1024 * 1024
FFN_DOWN_VMEM_LIMIT = 60 * 1024 * 1024

S5_LANE_TILES = B_WIDTH // LANES
S5_GROUPS_PER_TILE = LANES // S5_GROUP_CH
S5_STATE_TILES = S5_GROUPS * S5_STATE // LANES
S5_TILES_PER_LANE_TILE = S5_STATE_TILES // S5_LANE_TILES
S5_STATE_VREGS = S5_STATE_TILES // SUBLANES
S5_T = 256
S5_PITCH = S5_T + 4


def _gelu(x):
    return 0.5 * x * (1.0 + lax.erf(x * (1.0 / math.sqrt(2.0))))


def _rms_scale(x):
    return lax.rsqrt(jnp.mean(x * x, axis=-1, keepdims=True) + EPS)


def _s5_state_tiles(k):
    re = [k * S5_TILES_PER_LANE_TILE + c for c in range(S5_TILES_PER_LANE_TILE)]
    return re + [S5_STATE_TILES + q for q in re]


def _layernorm_bf16(z, g_ref, b_ref):
    mu = jnp.mean(z, axis=-1, keepdims=True)
    zc = z - mu
    var = jnp.mean(zc * zc, axis=-1, keepdims=True)
    return (zc * lax.rsqrt(var + EPS) * g_ref[...] + b_ref[...]).astype(BF16)


def _spatial_mix(v, ws_ref):
    n_blocks = v.shape[0] // GMLP_BLOCK
    ci = lax.broadcasted_iota(jnp.int32, (GMLP_BLOCK, GMLP_BLOCK), 0) // CHUNK
    cj = lax.broadcasted_iota(jnp.int32, (GMLP_BLOCK, GMLP_BLOCK), 1) // CHUNK
    mask = ci >= cj
    heads = []
    for h in range(A_HEADS):
        w = jnp.where(mask, ws_ref[h], 0.0).astype(BF16)
        cols = slice(h * A_HEAD_DIM, (h + 1) * A_HEAD_DIM)
        rhs = jnp.concatenate(
            [v[n * GMLP_BLOCK:(n + 1) * GMLP_BLOCK, cols] for n in range(n_blocks)], axis=1)
        heads.append(jnp.dot(w, rhs, preferred_element_type=F32))
    return jnp.concatenate(
        [jnp.concatenate([hd[:, n * A_HEAD_DIM:(n + 1) * A_HEAD_DIM] for hd in heads], axis=1)
         for n in range(n_blocks)], axis=0)


def _mixer_kernel(x_ref, g_ref, w_ref, lng_ref, lnb_ref, ws_ref, bias_ref, oga_ref, wo_ref,
                  wd_ref, wb_ref, wc_ref, wglu_ref, are_ref, aim_ref, d_ref, ogb_ref,
                  mixa_ref, mixb_ref, wob_ref, wdb_ref, st_ref, bu_ref):
    t_rows = x_ref.shape[0]

    @pl.when(pl.program_id(0) == 0)
    def _():
        st_ref[...] = jnp.zeros_like(st_ref)

    wob_ref[...] = wo_ref[...].astype(BF16)
    wdb_ref[...] = wd_ref[...].astype(BF16)

    x = x_ref[...]
    xn = (x * _rms_scale(x) * g_ref[...]).astype(BF16)

    u = jnp.dot(xn, w_ref[:, 2 * A_WIDTH:], preferred_element_type=F32)
    for k in range(S5_LANE_TILES):
        ub = u[:, k * LANES:(k + 1) * LANES].astype(BF16)
        res = jnp.dot(ub, wb_ref[k], preferred_element_type=F32)
        for c, q in enumerate(_s5_state_tiles(k)):
            bu_ref[q * S5_PITCH:q * S5_PITCH + t_rows, :] = res[:, c * LANES:(c + 1) * LANES]

    hu = jnp.dot(xn, w_ref[:, 0:A_WIDTH], preferred_element_type=F32)
    hv = jnp.dot(xn, w_ref[:, A_WIDTH:2 * A_WIDTH], preferred_element_type=F32)

    a_re = [are_ref[m] for m in range(S5_STATE_VREGS)]
    a_im = [aim_ref[m] for m in range(S5_STATE_VREGS)]
    s_re = [st_ref[m] for m in range(S5_STATE_VREGS)]
    s_im = [st_ref[S5_STATE_VREGS + m] for m in range(S5_STATE_VREGS)]
    im_base = S5_STATE_TILES * S5_PITCH
    for t in range(t_rows):
        for m in range(S5_STATE_VREGS):
            re_rows = pl.ds(t + SUBLANES * m * S5_PITCH, SUBLANES, stride=S5_PITCH)
            im_rows = pl.ds(t + im_base + SUBLANES * m * S5_PITCH, SUBLANES, stride=S5_PITCH)
            nr = a_re[m] * s_re[m] - a_im[m] * s_im[m] + bu_ref[re_rows, :]
            ni = a_re[m] * s_im[m] + a_im[m] * s_re[m] + bu_ref[im_rows, :]
            bu_ref[re_rows, :] = nr
            bu_ref[im_rows, :] = ni
            s_re[m], s_im[m] = nr, ni
    for m in range(S5_STATE_VREGS):
        st_ref[m] = s_re[m]
        st_ref[S5_STATE_VREGS + m] = s_im[m]

    yc = []
    for k in range(S5_LANE_TILES):
        lhs = jnp.concatenate(
            [bu_ref[q * S5_PITCH:q * S5_PITCH + t_rows, :] for q in _s5_state_tiles(k)],
            axis=1).astype(BF16)
        yc.append(jnp.dot(lhs, wc_ref[k], preferred_element_type=F32))

    zu = _gelu(hu)
    v = _layernorm_bf16(_gelu(hv), lng_ref, lnb_ref)

    ys = []
    for k in range(S5_LANE_TILES):
        cols = slice(k * LANES, (k + 1) * LANES)
        ys.append(_gelu(yc[k] + d_ref[:, cols] * u[:, cols]))
    gates = [jnp.dot(ys[k].astype(BF16), wglu_ref[k], preferred_element_type=F32)
             for k in range(S5_LANE_TILES)]
    yb = jnp.concatenate([ys[k] * jax.nn.sigmoid(gates[k]) for k in range(S5_LANE_TILES)], axis=1)
    mixb_ref[...] = (yb * _rms_scale(yb) * ogb_ref[...]).astype(BF16)

    bias = jnp.concatenate([bias_ref[...]] * (t_rows // GMLP_BLOCK), axis=0)
    ya = zu * (_spatial_mix(v, ws_ref) + bias)
    mixa_ref[...] = (ya * _rms_scale(ya) * oga_ref[...]).astype(BF16)


def _mixer(x, g, w_bf16, ln_g, ln_b, w_s, bias_full, out_a_g, w_out, w_down,
           wb, wc, wglu, a_re, a_im, d, out_b_g):
    tm = S5_T
    n_steps = SEQ // tm
    wo_rows = w_out.shape[0] // n_steps
    wd_rows = w_down.shape[0] // n_steps
    n_state_cols = 2 * S5_GROUPS_PER_TILE * S5_STATE
    const2 = lambda i: (0, 0)
    const3 = lambda i: (0, 0, 0)
    once = dict(pipeline_mode=pl.Buffered(1))
    return pl.pallas_call(
        _mixer_kernel,
        grid=(n_steps,),
        in_specs=[
            pl.BlockSpec((tm, D_MODEL), lambda i: (i, 0)),
            pl.BlockSpec((1, D_MODEL), const2),
            pl.BlockSpec((D_MODEL, IN_WIDTH), const2, **once),
            pl.BlockSpec((1, A_WIDTH), const2),
            pl.BlockSpec((1, A_WIDTH), const2),
            pl.BlockSpec((A_HEADS, GMLP_BLOCK, GMLP_BLOCK), const3),
            pl.BlockSpec((GMLP_BLOCK, A_WIDTH), const2),
            pl.BlockSpec((1, A_WIDTH), const2),
            pl.BlockSpec((wo_rows, D_MODEL), lambda i: (i, 0)),
            pl.BlockSpec((wd_rows, D_MODEL), lambda i: (i, 0)),
            pl.BlockSpec((S5_LANE_TILES, LANES, n_state_cols), const3, **once),
            pl.BlockSpec((S5_LANE_TILES, n_state_cols, LANES), const3, **once),
            pl.BlockSpec((S5_LANE_TILES, LANES, LANES), const3),
            pl.BlockSpec((S5_STATE_VREGS, SUBLANES, LANES), const3),
            pl.BlockSpec((S5_STATE_VREGS, SUBLANES, LANES), const3),
            pl.BlockSpec((1, B_WIDTH), const2),
            pl.BlockSpec((1, B_WIDTH), const2),
        ],
        out_specs=[
            pl.BlockSpec((tm, A_WIDTH), lambda i: (i, 0)),
            pl.BlockSpec((tm, B_WIDTH), lambda i: (i, 0)),
            pl.BlockSpec((wo_rows, D_MODEL), lambda i: (i, 0)),
            pl.BlockSpec((wd_rows, D_MODEL), lambda i: (i, 0)),
        ],
        out_shape=[jax.ShapeDtypeStruct((SEQ, A_WIDTH), BF16),
                   jax.ShapeDtypeStruct((SEQ, B_WIDTH), BF16),
                   jax.ShapeDtypeStruct(w_out.shape, BF16),
                   jax.ShapeDtypeStruct(w_down.shape, BF16)],
        scratch_shapes=[
            pltpu.VMEM((2 * S5_STATE_VREGS, SUBLANES, LANES), F32),
            pltpu.VMEM((2 * S5_STATE_TILES * S5_PITCH, LANES), F32),
        ],
        compiler_params=pltpu.CompilerParams(
            dimension_semantics=("arbitrary",), vmem_limit_bytes=VMEM_LIMIT),
        name="mixer",
    )(x, g, w_bf16, ln_g, ln_b, w_s, bias_full, out_a_g, w_out, w_down,
      wb, wc, wglu, a_re, a_im, d, out_b_g)


def _s5_params(lam_re, lam_im, log_dt, b_re, b_im, c_re, c_im, w_glu):
    dt = jnp.exp(log_dt)[:, None]
    mag = jnp.exp(lam_re * dt)
    ab_r = mag * jnp.cos(lam_im * dt)
    ab_i = mag * jnp.sin(lam_im * dt)
    den = lam_re * lam_re + lam_im * lam_im
    nr = ab_r - 1.0
    ni = ab_i
    co_r = (nr * lam_re + ni * lam_im) / den
    co_i = (ni * lam_re - nr * lam_im) / den
    bb_r = co_r[..., None] * b_re - co_i[..., None] * b_im
    bb_i = co_r[..., None] * b_im + co_i[..., None] * b_re

    gt = S5_GROUPS_PER_TILE
    on_diag = jnp.eye(gt, dtype=bool)[None, :, None, :, None]

    def blockdiag(blocks):
        _, r, c = blocks.shape
        b = blocks.astype(BF16).reshape(S5_LANE_TILES, gt, r, 1, c)
        out = jnp.where(on_diag, b, jnp.zeros((), BF16))
        return out.reshape(S5_LANE_TILES, gt * r, gt * c)

    wb = jnp.concatenate([blockdiag(jnp.swapaxes(bb_r, 1, 2)),
                          blockdiag(jnp.swapaxes(bb_i, 1, 2))], axis=2)
    wc = jnp.concatenate([blockdiag(jnp.swapaxes(c_re, 1, 2)),
                          blockdiag(-jnp.swapaxes(c_im, 1, 2))], axis=1)
    wglu = blockdiag(w_glu)
    a_re = ab_r.reshape(S5_STATE_VREGS, SUBLANES, LANES)
    a_im = ab_i.reshape(S5_STATE_VREGS, SUBLANES, LANES)
    return wb, wc, wglu, a_re, a_im


def _outproj_kernel(x_ref, ma_ref, mb_ref, wa_ref, wb_ref, g_ref, x1_ref, hn_ref):
    acc = jnp.dot(ma_ref[...], wa_ref[...], preferred_element_type=F32)
    acc = acc + jnp.dot(mb_ref[...], wb_ref[...], preferred_element_type=F32)
    x1 = x_ref[...] + acc
    x1_ref[...] = x1
    hn_ref[...] = (x1 * _rms_scale(x1) * g_ref[...]).astype(BF16)


def _outproj(x, mix_a, mix_b, w_out_bf16, g, tm=512):
    return pl.pallas_call(
        _outproj_kernel,
        grid=(SEQ // tm,),
        in_specs=[
            pl.BlockSpec((tm, D_MODEL), lambda i: (i, 0)),
            pl.BlockSpec((tm, A_WIDTH), lambda i: (i, 0)),
            pl.BlockSpec((tm, B_WIDTH), lambda i: (i, 0)),
            pl.BlockSpec((A_WIDTH, D_MODEL), lambda i: (0, 0)),
            pl.BlockSpec((B_WIDTH, D_MODEL), lambda i: (1, 0)),
            pl.BlockSpec((1, D_MODEL), lambda i: (0, 0)),
        ],
        out_specs=[
            pl.BlockSpec((tm, D_MODEL), lambda i: (i, 0)),
            pl.BlockSpec((tm, D_MODEL), lambda i: (i, 0)),
        ],
        out_shape=[jax.ShapeDtypeStruct((SEQ, D_MODEL), F32),
                   jax.ShapeDtypeStruct((SEQ, D_MODEL), BF16)],
        compiler_params=pltpu.CompilerParams(
            dimension_semantics=("arbitrary",), vmem_limit_bytes=VMEM_LIMIT),
        name="outproj",
    )(x, mix_a, mix_b, w_out_bf16, w_out_bf16, g)


def _ffn_up_kernel(hn_ref, wg_ref, wu_ref, act_ref):
    hn = hn_ref[...]
    gate = jnp.dot(hn, wg_ref[...].astype(BF16), preferred_element_type=F32)
    up = jnp.dot(hn, wu_ref[...].astype(BF16), preferred_element_type=F32)
    act_ref[...] = (gate * jax.nn.sigmoid(gate) * up).astype(BF16)


def _ffn_up(hn, wg, wu, tm=2048, tf=512):
    return pl.pallas_call(
        _ffn_up_kernel,
        grid=(D_FF // tf, SEQ // tm),
        in_specs=[
            pl.BlockSpec((tm, D_MODEL), lambda f, i: (i, 0)),
            pl.BlockSpec((D_MODEL, tf), lambda f, i: (0, f)),
            pl.BlockSpec((D_MODEL, tf), lambda f, i: (0, f)),
        ],
        out_specs=pl.BlockSpec((tm, tf), lambda f, i: (i, f)),
        out_shape=jax.ShapeDtypeStruct((SEQ, D_FF), BF16),
        compiler_params=pltpu.CompilerParams(
            dimension_semantics=("arbitrary", "arbitrary"),
            vmem_limit_bytes=VMEM_LIMIT),
        name="ffn_up",
    )(hn, wg, wu)


def _ffn_down_kernel(act_ref, wd_ref, x1_ref, g_ref, o_ref):
    n_tiles = pl.num_programs(1)
    tn = wd_ref.shape[1]
    for n in range(o_ref.shape[1] // tn):
        @pl.when(pl.program_id(1) == n)
        def _(n=n):
            o_ref[:, n * tn:(n + 1) * tn] = x1_ref[...] + jnp.dot(
                act_ref[...], wd_ref[...], preferred_element_type=F32)

    @pl.when(pl.program_id(1) == n_tiles - 1)
    def _():
        x2 = o_ref[...]
        o_ref[...] = x2 * _rms_scale(x2) * g_ref[...]


def _ffn_down(act, wd_bf16, x1, g, tm=1024, tn=512):
    return pl.pallas_call(
        _ffn_down_kernel,
        grid=(SEQ // tm, D_MODEL // tn),
        in_specs=[
            pl.BlockSpec((tm, D_FF), lambda i, n: (i, 0)),
            pl.BlockSpec((D_FF, tn), lambda i, n: (0, n)),
            pl.BlockSpec((tm, tn), lambda i, n: (i, n)),
            pl.BlockSpec((1, D_MODEL), lambda i, n: (0, 0)),
        ],
        out_specs=pl.BlockSpec((tm, D_MODEL), lambda i, n: (i, 0)),
        out_shape=jax.ShapeDtypeStruct((SEQ, D_MODEL), F32),
        compiler_params=pltpu.CompilerParams(
            dimension_semantics=("arbitrary", "arbitrary"),
            vmem_limit_bytes=FFN_DOWN_VMEM_LIMIT),
        name="ffn_down",
    )(act, wd_bf16, x1, g)


def kernel(x, norm_mix_g, w_in, a_ln_g, a_ln_b, a_w_s, a_b_s, s5_lambda_re, s5_lambda_im,
           s5_log_dt, s5_b_re, s5_b_im, s5_c_re, s5_c_im, s5_d, s5_w_glu, out_norm_a_g,
           out_norm_b_g, w_out, norm_ffn_g, w_gate, w_up, w_down, final_norm_g):
    assert x.shape == (1, SEQ, D_MODEL) and norm_mix_g.shape[0] == 1
    xs = x.reshape(SEQ, D_MODEL)
    l = 0

    bias_full = jnp.repeat(jnp.transpose(a_b_s[l]), A_HEAD_DIM, axis=1)
    wb, wc, wglu, a_re, a_im = _s5_params(
        s5_lambda_re[l], s5_lambda_im[l], s5_log_dt[l], s5_b_re[l], s5_b_im[l],
        s5_c_re[l], s5_c_im[l], s5_w_glu[l])
    mix_a, mix_b, w_out_bf16, w_down_bf16 = _mixer(
        xs, norm_mix_g[l][None], w_in[l].astype(BF16), a_ln_g[l][None], a_ln_b[l][None],
        a_w_s[l], bias_full, out_norm_a_g[l][None], w_out[l], w_down[l],
        wb, wc, wglu, a_re, a_im, s5_d[l][None], out_norm_b_g[l][None])

    x1, hn = _outproj(xs, mix_a, mix_b, w_out_bf16, norm_ffn_g[l][None])

    act = _ffn_up(hn, w_gate[l], w_up[l])
    out = _ffn_down(act, w_down_bf16, x1, final_norm_g[None])
    return out.reshape(1, SEQ, D_MODEL)
```

```python
import math

import jax
import jax.numpy as jnp
from jax import lax
from jax.experimental import pallas as pl
from jax.experimental.pallas import tpu as pltpu

F32 = jnp.float32
BF16 = jnp.bfloat16

D_MODEL = 2048
SEQ = 8192
CHUNK = 64
A_WIDTH = 1024
A_HEADS = 8
A_HEAD_DIM = 128
GMLP_BLOCK = 128
B_WIDTH = 1024
S5_GROUP_CH = 16
S5_GROUPS = 64
S5_STATE = 64
IN_WIDTH = 3072
D_FF = 5632
EPS = 1e-6

LANES = 128
SUBLANES = 8
MXU_COLS = 256
VMEM_LIMIT = 56 * 1024 * 1024
FFN_DOWN_VMEM_LIMIT = 60 * 1024 * 1024

S5_LANE_TILES = B_WIDTH // LANES
S5_GROUPS_PER_TILE = LANES // S5_GROUP_CH
S5_STATE_TILES = S5_GROUPS * S5_STATE // LANES
S5_TILES_PER_LANE_TILE = S5_STATE_TILES // S5_LANE_TILES
S5_STATE_VREGS = S5_STATE_TILES // SUBLANES
S5_T = 256
S5_PITCH = S5_T + 4


def _gelu(x):
    return 0.5 * x * (1.0 + lax.erf(x * (1.0 / math.sqrt(2.0))))


def _rms_scale(x):
    return lax.rsqrt(jnp.mean(x * x, axis=-1, keepdims=True) + EPS)


def _s5_state_tiles(k):
    re = [k * S5_TILES_PER_LANE_TILE + c for c in range(S5_TILES_PER_LANE_TILE)]
    return re + [S5_STATE_TILES + q for q in re]


def _layernorm_bf16(z, g_ref, b_ref):
    mu = jnp.mean(z, axis=-1, keepdims=True)
    zc = z - mu
    var = jnp.mean(zc * zc, axis=-1, keepdims=True)
    return (zc * lax.rsqrt(var + EPS) * g_ref[...] + b_ref[...]).astype(BF16)


def _spatial_mix(v, ws_ref):
    n_blocks = v.shape[0] // GMLP_BLOCK
    ci = lax.broadcasted_iota(jnp.int32, (GMLP_BLOCK, GMLP_BLOCK), 0) // CHUNK
    cj = lax.broadcasted_iota(jnp.int32, (GMLP_BLOCK, GMLP_BLOCK), 1) // CHUNK
    mask = ci >= cj
    heads = []
    for h in range(A_HEADS):
        w = jnp.where(mask, ws_ref[h], 0.0).astype(BF16)
        cols = slice(h * A_HEAD_DIM, (h + 1) * A_HEAD_DIM)
        rhs = jnp.concatenate(
            [v[n * GMLP_BLOCK:(n + 1) * GMLP_BLOCK, cols] for n in range(n_blocks)], axis=1)
        heads.append(jnp.dot(w, rhs, preferred_element_type=F32))
    return jnp.concatenate(
        [jnp.concatenate([hd[:, n * A_HEAD_DIM:(n + 1) * A_HEAD_DIM] for hd in heads], axis=1)
         for n in range(n_blocks)], axis=0)


def _mixer_kernel(x_ref, g_ref, w_ref, lng_ref, lnb_ref, ws_ref, bias_ref, oga_ref, wo_ref,
                  wd_ref, wb_ref, wc_ref, wglu_ref, are_ref, aim_ref, d_ref, ogb_ref,
                  mixa_ref, mixb_ref, wob_ref, wdb_ref, st_ref, bu_ref):
    t_rows = x_ref.shape[0]

    @pl.when(pl.program_id(0) == 0)
    def _():
        st_ref[...] = jnp.zeros_like(st_ref)

    wob_ref[...] = wo_ref[...].astype(BF16)
    wdb_ref[...] = wd_ref[...].astype(BF16)

    x = x_ref[...]
    xn = (x * _rms_scale(x) * g_ref[...]).astype(BF16)

    u = jnp.dot(xn, w_ref[:, 2 * A_WIDTH:], preferred_element_type=F32)
    for k in range(S5_LANE_TILES):
        ub = u[:, k * LANES:(k + 1) * LANES].astype(BF16)
        res = jnp.dot(ub, wb_ref[k], preferred_element_type=F32)
        for c, q in enumerate(_s5_state_tiles(k)):
            bu_ref[q * S5_PITCH:q * S5_PITCH + t_rows, :] = res[:, c * LANES:(c + 1) * LANES]

    hu = jnp.dot(xn, w_ref[:, 0:A_WIDTH], preferred_element_type=F32)
    hv = jnp.dot(xn, w_ref[:, A_WIDTH:2 * A_WIDTH], preferred_element_type=F32)

    a_re = [are_ref[m] for m in range(S5_STATE_VREGS)]
    a_im = [aim_ref[m] for m in range(S5_STATE_VREGS)]
    s_re = [st_ref[m] for m in range(S5_STATE_VREGS)]
    s_im = [st_ref[S5_STATE_VREGS + m] for m in range(S5_STATE_VREGS)]
    im_base = S5_STATE_TILES * S5_PITCH
    for t in range(t_rows):
        for m in range(S5_STATE_VREGS):
            re_rows = pl.ds(t + SUBLANES * m * S5_PITCH, SUBLANES, stride=S5_PITCH)
            im_rows = pl.ds(t + im_base + SUBLANES * m * S5_PITCH, SUBLANES, stride=S5_PITCH)
            nr = a_re[m] * s_re[m] - a_im[m] * s_im[m] + bu_ref[re_rows, :]
            ni = a_re[m] * s_im[m] + a_im[m] * s_re[m] + bu_ref[im_rows, :]
            bu_ref[re_rows, :] = nr
            bu_ref[im_rows, :] = ni
            s_re[m], s_im[m] = nr, ni
    for m in range(S5_STATE_VREGS):
        st_ref[m] = s_re[m]
        st_ref[S5_STATE_VREGS + m] = s_im[m]

    zu = _gelu(hu)
    v = _layernorm_bf16(_gelu(hv), lng_ref, lnb_ref)

    yc = []
    for k in range(S5_LANE_TILES):
        lhs = jnp.concatenate(
            [bu_ref[q * S5_PITCH:q * S5_PITCH + t_rows, :] for q in _s5_state_tiles(k)],
            axis=1).astype(BF16)
        yc.append(jnp.dot(lhs, wc_ref[k], preferred_element_type=F32))

    ys = []
    for k in range(S5_LANE_TILES):
        cols = slice(k * LANES, (k + 1) * LANES)
        ys.append(_gelu(yc[k] + d_ref[:, cols] * u[:, cols]))
    gates = [jnp.dot(ys[k].astype(BF16), wglu_ref[k], preferred_element_type=F32)
             for k in range(S5_LANE_TILES)]
    yb = jnp.concatenate([ys[k] * jax.nn.sigmoid(gates[k]) for k in range(S5_LANE_TILES)], axis=1)
    mixb_ref[...] = (yb * _rms_scale(yb) * ogb_ref[...]).astype(BF16)

    bias = jnp.concatenate([bias_ref[...]] * (t_rows // GMLP_BLOCK), axis=0)
    ya = zu * (_spatial_mix(v, ws_ref) + bias)
    mixa_ref[...] = (ya * _rms_scale(ya) * oga_ref[...]).astype(BF16)


def _mixer(x, g, w_bf16, ln_g, ln_b, w_s, bias_full, out_a_g, w_out, w_down,
           wb, wc, wglu, a_re, a_im, d, out_b_g):
    tm = S5_T
    n_steps = SEQ // tm
    wo_rows = w_out.shape[0] // n_steps
    wd_rows = w_down.shape[0] // n_steps
    n_state_cols = 2 * S5_GROUPS_PER_TILE * S5_STATE
    const2 = lambda i: (0, 0)
    const3 = lambda i: (0, 0, 0)
    once = dict(pipeline_mode=pl.Buffered(1))
    return pl.pallas_call(
        _mixer_kernel,
        grid=(n_steps,),
        in_specs=[
            pl.BlockSpec((tm, D_MODEL), lambda i: (i, 0)),
            pl.BlockSpec((1, D_MODEL), const2),
            pl.BlockSpec((D_MODEL, IN_WIDTH), const2, **once),
            pl.BlockSpec((1, A_WIDTH), const2),
            pl.BlockSpec((1, A_WIDTH), const2),
            pl.BlockSpec((A_HEADS, GMLP_BLOCK, GMLP_BLOCK), const3),
            pl.BlockSpec((GMLP_BLOCK, A_WIDTH), const2),
            pl.BlockSpec((1, A_WIDTH), const2),
            pl.BlockSpec((wo_rows, D_MODEL), lambda i: (i, 0)),
            pl.BlockSpec((wd_rows, D_MODEL), lambda i: (i, 0)),
            pl.BlockSpec((S5_LANE_TILES, LANES, n_state_cols), const3, **once),
            pl.BlockSpec((S5_LANE_TILES, n_state_cols, LANES), const3, **once),
            pl.BlockSpec((S5_LANE_TILES, LANES, LANES), const3),
            pl.BlockSpec((S5_STATE_VREGS, SUBLANES, LANES), const3),
            pl.BlockSpec((S5_STATE_VREGS, SUBLANES, LANES), const3),
            pl.BlockSpec((1, B_WIDTH), const2),
            pl.BlockSpec((1, B_WIDTH), const2),
        ],
        out_specs=[
            pl.BlockSpec((tm, A_WIDTH), lambda i: (i, 0)),
            pl.BlockSpec((tm, B_WIDTH), lambda i: (i, 0)),
            pl.BlockSpec((wo_rows, D_MODEL), lambda i: (i, 0)),
            pl.BlockSpec((wd_rows, D_MODEL), lambda i: (i, 0)),
        ],
        out_shape=[jax.ShapeDtypeStruct((SEQ, A_WIDTH), BF16),
                   jax.ShapeDtypeStruct((SEQ, B_WIDTH), BF16),
                   jax.ShapeDtypeStruct(w_out.shape, BF16),
                   jax.ShapeDtypeStruct(w_down.shape, BF16)],
        scratch_shapes=[
            pltpu.VMEM((2 * S5_STATE_VREGS, SUBLANES, LANES), F32),
            pltpu.VMEM((2 * S5_STATE_TILES * S5_PITCH, LANES), F32),
        ],
        compiler_params=pltpu.CompilerParams(
            dimension_semantics=("arbitrary",), vmem_limit_bytes=VMEM_LIMIT),
        name="mixer",
    )(x, g, w_bf16, ln_g, ln_b, w_s, bias_full, out_a_g, w_out, w_down,
      wb, wc, wglu, a_re, a_im, d, out_b_g)


def _s5_params(lam_re, lam_im, log_dt, b_re, b_im, c_re, c_im, w_glu):
    dt = jnp.exp(log_dt)[:, None]
    mag = jnp.exp(lam_re * dt)
    ab_r = mag * jnp.cos(lam_im * dt)
    ab_i = mag * jnp.sin(lam_im * dt)
    den = lam_re * lam_re + lam_im * lam_im
    nr = ab_r - 1.0
    ni = ab_i
    co_r = (nr * lam_re + ni * lam_im) / den
    co_i = (ni * lam_re - nr * lam_im) / den
    bb_r = co_r[..., None] * b_re - co_i[..., None] * b_im
    bb_i = co_r[..., None] * b_im + co_i[..., None] * b_re

    gt = S5_GROUPS_PER_TILE
    on_diag = jnp.eye(gt, dtype=bool)[None, :, None, :, None]

    def blockdiag(blocks):
        _, r, c = blocks.shape
        b = blocks.astype(BF16).reshape(S5_LANE_TILES, gt, r, 1, c)
        out = jnp.where(on_diag, b, jnp.zeros((), BF16))
        return out.reshape(S5_LANE_TILES, gt * r, gt * c)

    wb = jnp.concatenate([blockdiag(jnp.swapaxes(bb_r, 1, 2)),
                          blockdiag(jnp.swapaxes(bb_i, 1, 2))], axis=2)
    wc = jnp.concatenate([blockdiag(jnp.swapaxes(c_re, 1, 2)),
                          blockdiag(-jnp.swapaxes(c_im, 1, 2))], axis=1)
    wglu = blockdiag(w_glu)
    a_re = ab_r.reshape(S5_STATE_VREGS, SUBLANES, LANES)
    a_im = ab_i.reshape(S5_STATE_VREGS, SUBLANES, LANES)
    return wb, wc, wglu, a_re, a_im


def _outproj_kernel(x_ref, ma_ref, mb_ref, wa_ref, wb_ref, g_ref, x1_ref, hn_ref):
    half = x_ref.shape[0] // 2
    for r in range(2):
        rows = slice(r * half, (r + 1) * half)
        acc = jnp.dot(ma_ref[rows, :], wa_ref[...], preferred_element_type=F32)
        acc = acc + jnp.dot(mb_ref[rows, :], wb_ref[...], preferred_element_type=F32)
        x1 = x_ref[rows, :] + acc
        x1_ref[rows, :] = x1
        hn_ref[rows, :] = (x1 * _rms_scale(x1) * g_ref[...]).astype(BF16)


def _outproj(x, mix_a, mix_b, w_out_bf16, g, tm=512):
    return pl.pallas_call(
        _outproj_kernel,
        grid=(SEQ // tm,),
        in_specs=[
            pl.BlockSpec((tm, D_MODEL), lambda i: (i, 0)),
            pl.BlockSpec((tm, A_WIDTH), lambda i: (i, 0)),
            pl.BlockSpec((tm, B_WIDTH), lambda i: (i, 0)),
            pl.BlockSpec((A_WIDTH, D_MODEL), lambda i: (0, 0)),
            pl.BlockSpec((B_WIDTH, D_MODEL), lambda i: (1, 0)),
            pl.BlockSpec((1, D_MODEL), lambda i: (0, 0)),
        ],
        out_specs=[
            pl.BlockSpec((tm, D_MODEL), lambda i: (i, 0)),
            pl.BlockSpec((tm, D_MODEL), lambda i: (i, 0)),
        ],
        out_shape=[jax.ShapeDtypeStruct((SEQ, D_MODEL), F32),
                   jax.ShapeDtypeStruct((SEQ, D_MODEL), BF16)],
        compiler_params=pltpu.CompilerParams(
            dimension_semantics=("arbitrary",), vmem_limit_bytes=VMEM_LIMIT),
        name="outproj",
    )(x, mix_a, mix_b, w_out_bf16, w_out_bf16, g)


def _ffn_up_kernel(hn_ref, wg_ref, wu_ref, act_ref):
    hn = hn_ref[...]
    for c in range(act_ref.shape[1] // MXU_COLS):
        cols = slice(c * MXU_COLS, (c + 1) * MXU_COLS)
        gate = jnp.dot(hn, wg_ref[:, cols].astype(BF16), preferred_element_type=F32)
        up = jnp.dot(hn, wu_ref[:, cols].astype(BF16), preferred_element_type=F32)
        act_ref[:, cols] = (gate * jax.nn.sigmoid(gate) * up).astype(BF16)


def _ffn_up(hn, wg, wu, tm=2048, tf=512):
    return pl.pallas_call(
        _ffn_up_kernel,
        grid=(D_FF // tf, SEQ // tm),
        in_specs=[
            pl.BlockSpec((tm, D_MODEL), lambda f, i: (i, 0)),
            pl.BlockSpec((D_MODEL, tf), lambda f, i: (0, f)),
            pl.BlockSpec((D_MODEL, tf), lambda f, i: (0, f)),
        ],
        out_specs=pl.BlockSpec((tm, tf), lambda f, i: (i, f)),
        out_shape=jax.ShapeDtypeStruct((SEQ, D_FF), BF16),
        compiler_params=pltpu.CompilerParams(
            dimension_semantics=("arbitrary", "arbitrary"),
            vmem_limit_bytes=VMEM_LIMIT),
        name="ffn_up",
    )(hn, wg, wu)


def _ffn_down_kernel(act_ref, wd_ref, x1_ref, g_ref, o_ref):
    n_tiles = pl.num_programs(1)
    tn = wd_ref.shape[1]
    for n in range(o_ref.shape[1] // tn):
        @pl.when(pl.program_id(1) == n)
        def _(n=n):
            o_ref[:, n * tn:(n + 1) * tn] = x1_ref[...] + jnp.dot(
                act_ref[...], wd_ref[...], preferred_element_type=F32)

    @pl.when(pl.program_id(1) == n_tiles - 1)
    def _():
        x2 = o_ref[...]
        o_ref[...] = x2 * _rms_scale(x2) * g_ref[...]


def _ffn_down(act, wd_bf16, x1, g, tm=1024, tn=512):
    return pl.pallas_call(
        _ffn_down_kernel,
        grid=(SEQ // tm, D_MODEL // tn),
        in_specs=[
            pl.BlockSpec((tm, D_FF), lambda i, n: (i, 0)),
            pl.BlockSpec((D_FF, tn), lambda i, n: (0, n)),
            pl.BlockSpec((tm, tn), lambda i, n: (i, n)),
            pl.BlockSpec((1, D_MODEL), lambda i, n: (0, 0)),
        ],
        out_specs=pl.BlockSpec((tm, D_MODEL), lambda i, n: (i, 0)),
        out_shape=jax.ShapeDtypeStruct((SEQ, D_MODEL), F32),
        compiler_params=pltpu.CompilerParams(
            dimension_semantics=("arbitrary", "arbitrary"),
            vmem_limit_bytes=FFN_DOWN_VMEM_LIMIT),
        name="ffn_down",
    )(act, wd_bf16, x1, g)


def kernel(x, norm_mix_g, w_in, a_ln_g, a_ln_b, a_w_s, a_b_s, s5_lambda_re, s5_lambda_im,
           s5_log_dt, s5_b_re, s5_b_im, s5_c_re, s5_c_im, s5_d, s5_w_glu, out_norm_a_g,
           out_norm_b_g, w_out, norm_ffn_g, w_gate, w_up, w_down, final_norm_g):
    assert x.shape == (1, SEQ, D_MODEL) and norm_mix_g.shape[0] == 1
    xs = x.reshape(SEQ, D_MODEL)
    l = 0

    bias_full = jnp.repeat(jnp.transpose(a_b_s[l]), A_HEAD_DIM, axis=1)
    wb, wc, wglu, a_re, a_im = _s5_params(
        s5_lambda_re[l], s5_lambda_im[l], s5_log_dt[l], s5_b_re[l], s5_b_im[l],
        s5_c_re[l], s5_c_im[l], s5_w_glu[l])
    mix_a, mix_b, w_out_bf16, w_down_bf16 = _mixer(
        xs, norm_mix_g[l][None], w_in[l].astype(BF16), a_ln_g[l][None], a_ln_b[l][None],
        a_w_s[l], bias_full, out_norm_a_g[l][None], w_out[l], w_down[l],
        wb, wc, wglu, a_re, a_im, s5_d[l][None], out_norm_b_g[l][None])

    x1, hn = _outproj(xs, mix_a, mix_b, w_out_bf16, norm_ffn_g[l][None])

    act = _ffn_up(hn, w_gate[l], w_up[l])
    out = _ffn_down(act, w_down_bf16, x1, final_norm_g[None])
    return out.reshape(1, SEQ, D_MODEL)
```

```python
import math

import jax
import jax.numpy as jnp
from jax import lax
from jax.experimental import pallas as pl
from jax.experimental.pallas import tpu as pltpu

F32 = jnp.float32
BF16 = jnp.bfloat16

D_MODEL = 2048
SEQ = 8192
CHUNK = 64
A_WIDTH = 1024
A_HEADS = 8
A_HEAD_DIM = 128
GMLP_BLOCK = 128
B_WIDTH = 1024
S5_GROUP_CH = 16
S5_GROUPS = 64
S5_STATE = 64
IN_WIDTH = 3072
D_FF = 5632
EPS = 1e-6

LANES = 128
SUBLANES = 8
MXU_COLS = 256
VMEM_LIMIT = 56 * 1024 * 1024
FFN_DOWN_VMEM_LIMIT = 60 * 1024 * 1024

S5_LANE_TILES = B_WIDTH // LANES
S5_GROUPS_PER_TILE = LANES // S5_GROUP_CH
S5_STATE_TILES = S5_GROUPS * S5_STATE // LANES
S5_TILES_PER_LANE_TILE = S5_STATE_TILES // S5_LANE_TILES
S5_STATE_VREGS = S5_STATE_TILES // SUBLANES
S5_T = 256
S5_PITCH = S5_T + 4


def _gelu(x):
    return 0.5 * x * (1.0 + lax.erf(x * (1.0 / math.sqrt(2.0))))


def _rms_scale(x):
    return lax.rsqrt(jnp.mean(x * x, axis=-1, keepdims=True) + EPS)


def _s5_state_tiles(k):
    re = [k * S5_TILES_PER_LANE_TILE + c for c in range(S5_TILES_PER_LANE_TILE)]
    return re + [S5_STATE_TILES + q for q in re]


def _layernorm_bf16(z, g_ref, b_ref):
    mu = jnp.mean(z, axis=-1, keepdims=True)
    zc = z - mu
    var = jnp.mean(zc * zc, axis=-1, keepdims=True)
    return (zc * lax.rsqrt(var + EPS) * g_ref[...] + b_ref[...]).astype(BF16)


def _spatial_mix(v, ws_ref):
    n_blocks = v.shape[0] // GMLP_BLOCK
    ci = lax.broadcasted_iota(jnp.int32, (GMLP_BLOCK, GMLP_BLOCK), 0) // CHUNK
    cj = lax.broadcasted_iota(jnp.int32, (GMLP_BLOCK, GMLP_BLOCK), 1) // CHUNK
    mask = ci >= cj
    heads = []
    for h in range(A_HEADS):
        w = jnp.where(mask, ws_ref[h], 0.0).astype(BF16)
        cols = slice(h * A_HEAD_DIM, (h + 1) * A_HEAD_DIM)
        rhs = jnp.concatenate(
            [v[n * GMLP_BLOCK:(n + 1) * GMLP_BLOCK, cols] for n in range(n_blocks)], axis=1)
        heads.append(jnp.dot(w, rhs, preferred_element_type=F32))
    return jnp.concatenate(
        [jnp.concatenate([hd[:, n * A_HEAD_DIM:(n + 1) * A_HEAD_DIM] for hd in heads], axis=1)
         for n in range(n_blocks)], axis=0)


def _mixer_kernel(x_ref, g_ref, w_ref, lng_ref, lnb_ref, ws_ref, bias_ref, oga_ref, wo_ref,
                  wd_ref, wb_ref, wc_ref, wglu_ref, are_ref, aim_ref, d_ref, ogb_ref,
                  mixa_ref, mixb_ref, wob_ref, wdb_ref, st_ref, bu_ref):
    t_rows = x_ref.shape[0]

    @pl.when(pl.program_id(0) == 0)
    def _():
        st_ref[...] = jnp.zeros_like(st_ref)

    wob_ref[...] = wo_ref[...].astype(BF16)
    wdb_ref[...] = wd_ref[...].astype(BF16)

    x = x_ref[...]
    xn = (x * _rms_scale(x) * g_ref[...]).astype(BF16)

    u = jnp.dot(xn, w_ref[:, 2 * A_WIDTH:], preferred_element_type=F32)
    for k in range(S5_LANE_TILES):
        ub = u[:, k * LANES:(k + 1) * LANES].astype(BF16)
        res = jnp.dot(ub, wb_ref[k], preferred_element_type=F32)
        for c, q in enumerate(_s5_state_tiles(k)):
            bu_ref[q * S5_PITCH:q * S5_PITCH + t_rows, :] = res[:, c * LANES:(c + 1) * LANES]

    hu = jnp.dot(xn, w_ref[:, 0:A_WIDTH], preferred_element_type=F32)
    hv = jnp.dot(xn, w_ref[:, A_WIDTH:2 * A_WIDTH], preferred_element_type=F32)

    a_re = [are_ref[m] for m in range(S5_STATE_VREGS)]
    a_im = [aim_ref[m] for m in range(S5_STATE_VREGS)]
    s_re = [st_ref[m] for m in range(S5_STATE_VREGS)]
    s_im = [st_ref[S5_STATE_VREGS + m] for m in range(S5_STATE_VREGS)]
    im_base = S5_STATE_TILES * S5_PITCH
    for t in range(t_rows):
        for m in range(S5_STATE_VREGS):
            re_rows = pl.ds(t + SUBLANES * m * S5_PITCH, SUBLANES, stride=S5_PITCH)
            im_rows = pl.ds(t + im_base + SUBLANES * m * S5_PITCH, SUBLANES, stride=S5_PITCH)
            nr = a_re[m] * s_re[m] - a_im[m] * s_im[m] + bu_ref[re_rows, :]
            ni = a_re[m] * s_im[m] + a_im[m] * s_re[m] + bu_ref[im_rows, :]
            bu_ref[re_rows, :] = nr
            bu_ref[im_rows, :] = ni
            s_re[m], s_im[m] = nr, ni
    for m in range(S5_STATE_VREGS):
        st_ref[m] = s_re[m]
        st_ref[S5_STATE_VREGS + m] = s_im[m]

    zu = _gelu(hu)
    v = _layernorm_bf16(_gelu(hv), lng_ref, lnb_ref)

    yc = []
    for k in range(S5_LANE_TILES):
        lhs = jnp.concatenate(
            [bu_ref[q * S5_PITCH:q * S5_PITCH + t_rows, :] for q in _s5_state_tiles(k)],
            axis=1).astype(BF16)
        yc.append(jnp.dot(lhs, wc_ref[k], preferred_element_type=F32))

    ys = []
    for k in range(S5_LANE_TILES):
        cols = slice(k * LANES, (k + 1) * LANES)
        ys.append(_gelu(yc[k] + d_ref[:, cols] * u[:, cols]))
    gates = [jnp.dot(ys[k].astype(BF16), wglu_ref[k], preferred_element_type=F32)
             for k in range(S5_LANE_TILES)]
    yb = jnp.concatenate([ys[k] * jax.nn.sigmoid(gates[k]) for k in range(S5_LANE_TILES)], axis=1)
    mixb_ref[...] = (yb * _rms_scale(yb) * ogb_ref[...]).astype(BF16)

    bias = jnp.concatenate([bias_ref[...]] * (t_rows // GMLP_BLOCK), axis=0)
    ya = zu * (_spatial_mix(v, ws_ref) + bias)
    mixa_ref[...] = (ya * _rms_scale(ya) * oga_ref[...]).astype(BF16)


def _mixer(x, g, w_bf16, ln_g, ln_b, w_s, bias_full, out_a_g, w_out, w_down,
           wb, wc, wglu, a_re, a_im, d, out_b_g):
    tm = S5_T
    n_steps = SEQ // tm
    wo_rows = w_out.shape[0] // n_steps
    wd_rows = w_down.shape[0] // n_steps
    n_state_cols = 2 * S5_GROUPS_PER_TILE * S5_STATE
    const2 = lambda i: (0, 0)
    const3 = lambda i: (0, 0, 0)
    once = dict(pipeline_mode=pl.Buffered(1))
    return pl.pallas_call(
        _mixer_kernel,
        grid=(n_steps,),
        in_specs=[
            pl.BlockSpec((tm, D_MODEL), lambda i: (i, 0)),
            pl.BlockSpec((1, D_MODEL), const2),
            pl.BlockSpec((D_MODEL, IN_WIDTH), const2, **once),
            pl.BlockSpec((1, A_WIDTH), const2),
            pl.BlockSpec((1, A_WIDTH), const2),
            pl.BlockSpec((A_HEADS, GMLP_BLOCK, GMLP_BLOCK), const3),
            pl.BlockSpec((GMLP_BLOCK, A_WIDTH), const2),
            pl.BlockSpec((1, A_WIDTH), const2),
            pl.BlockSpec((wo_rows, D_MODEL), lambda i: (i, 0)),
            pl.BlockSpec((wd_rows, D_MODEL), lambda i: (i, 0)),
            pl.BlockSpec((S5_LANE_TILES, LANES, n_state_cols), const3, **once),
            pl.BlockSpec((S5_LANE_TILES, n_state_cols, LANES), const3, **once),
            pl.BlockSpec((S5_LANE_TILES, LANES, LANES), const3),
            pl.BlockSpec((S5_STATE_VREGS, SUBLANES, LANES), const3),
            pl.BlockSpec((S5_STATE_VREGS, SUBLANES, LANES), const3),
            pl.BlockSpec((1, B_WIDTH), const2),
            pl.BlockSpec((1, B_WIDTH), const2),
        ],
        out_specs=[
            pl.BlockSpec((tm, A_WIDTH), lambda i: (i, 0)),
            pl.BlockSpec((tm, B_WIDTH), lambda i: (i, 0)),
            pl.BlockSpec((wo_rows, D_MODEL), lambda i: (i, 0)),
            pl.BlockSpec((wd_rows, D_MODEL), lambda i: (i, 0)),
        ],
        out_shape=[jax.ShapeDtypeStruct((SEQ, A_WIDTH), BF16),
                   jax.ShapeDtypeStruct((SEQ, B_WIDTH), BF16),
                   jax.ShapeDtypeStruct(w_out.shape, BF16),
                   jax.ShapeDtypeStruct(w_down.shape, BF16)],
        scratch_shapes=[
            pltpu.VMEM((2 * S5_STATE_VREGS, SUBLANES, LANES), F32),
            pltpu.VMEM((2 * S5_STATE_TILES * S5_PITCH, LANES), F32),
        ],
        compiler_params=pltpu.CompilerParams(
            dimension_semantics=("arbitrary",), vmem_limit_bytes=VMEM_LIMIT),
        name="mixer",
    )(x, g, w_bf16, ln_g, ln_b, w_s, bias_full, out_a_g, w_out, w_down,
      wb, wc, wglu, a_re, a_im, d, out_b_g)


def _s5_discretise_kernel(lre_ref, lim_ref, ldt_ref, bre_ref, bim_ref,
                          abr_ref, abi_ref, bbr_ref, bbi_ref):
    lr, li = lre_ref[...], lim_ref[...]
    dt = jnp.exp(ldt_ref[...])
    mag = jnp.exp(lr * dt)
    ab_r = mag * jnp.cos(li * dt)
    ab_i = mag * jnp.sin(li * dt)
    den = lr * lr + li * li
    nr = ab_r - 1.0
    co_r = (nr * lr + ab_i * li) / den
    co_i = (ab_i * lr - nr * li) / den
    abr_ref[...] = ab_r
    abi_ref[...] = ab_i
    br, bi = bre_ref[...], bim_ref[...]
    bbr_ref[...] = co_r[:, None, :] * br - co_i[:, None, :] * bi
    bbi_ref[...] = co_r[:, None, :] * bi + co_i[:, None, :] * br


def _s5_params(lam_re, lam_im, log_dt, b_re, b_im, c_re, c_im, w_glu):
    g_p = jax.ShapeDtypeStruct((S5_GROUPS, S5_STATE), F32)
    g_h_p = jax.ShapeDtypeStruct((S5_GROUPS, S5_GROUP_CH, S5_STATE), F32)
    ab_r, ab_i, bbt_r, bbt_i = pl.pallas_call(
        _s5_discretise_kernel,
        out_shape=[g_p, g_p, g_h_p, g_h_p],
        name="s5_discretise",
    )(lam_re, lam_im, log_dt[:, None], jnp.swapaxes(b_re, 1, 2), jnp.swapaxes(b_im, 1, 2))

    gt = S5_GROUPS_PER_TILE
    on_diag = jnp.eye(gt, dtype=bool)[None, :, None, :, None]

    def blockdiag(blocks):
        _, r, c = blocks.shape
        b = blocks.astype(BF16).reshape(S5_LANE_TILES, gt, r, 1, c)
        out = jnp.where(on_diag, b, jnp.zeros((), BF16))
        return out.reshape(S5_LANE_TILES, gt * r, gt * c)

    wb = jnp.concatenate([blockdiag(bbt_r), blockdiag(bbt_i)], axis=2)
    wc = jnp.concatenate([blockdiag(jnp.swapaxes(c_re, 1, 2)),
                          blockdiag(-jnp.swapaxes(c_im, 1, 2))], axis=1)
    wglu = blockdiag(w_glu)
    a_re = ab_r.reshape(S5_STATE_VREGS, SUBLANES, LANES)
    a_im = ab_i.reshape(S5_STATE_VREGS, SUBLANES, LANES)
    return wb, wc, wglu, a_re, a_im


def _outproj_kernel(x_ref, ma_ref, mb_ref, wa_ref, wb_ref, g_ref, x1_ref, hn_ref):
    half = x_ref.shape[0] // 2
    for r in range(2):
        rows = slice(r * half, (r + 1) * half)
        acc = jnp.dot(ma_ref[rows, :], wa_ref[...], preferred_element_type=F32)
        acc = acc + jnp.dot(mb_ref[rows, :], wb_ref[...], preferred_element_type=F32)
        x1 = x_ref[rows, :] + acc
        x1_ref[rows, :] = x1
        hn_ref[rows, :] = (x1 * _rms_scale(x1) * g_ref[...]).astype(BF16)


def _outproj(x, mix_a, mix_b, w_out_bf16, g, tm=512):
    return pl.pallas_call(
        _outproj_kernel,
        grid=(SEQ // tm,),
        in_specs=[
            pl.BlockSpec((tm, D_MODEL), lambda i: (i, 0)),
            pl.BlockSpec((tm, A_WIDTH), lambda i: (i, 0)),
            pl.BlockSpec((tm, B_WIDTH), lambda i: (i, 0)),
            pl.BlockSpec((A_WIDTH, D_MODEL), lambda i: (0, 0)),
            pl.BlockSpec((B_WIDTH, D_MODEL), lambda i: (1, 0)),
            pl.BlockSpec((1, D_MODEL), lambda i: (0, 0)),
        ],
        out_specs=[
            pl.BlockSpec((tm, D_MODEL), lambda i: (i, 0)),
            pl.BlockSpec((tm, D_MODEL), lambda i: (i, 0)),
        ],
        out_shape=[jax.ShapeDtypeStruct((SEQ, D_MODEL), F32),
                   jax.ShapeDtypeStruct((SEQ, D_MODEL), BF16)],
        compiler_params=pltpu.CompilerParams(
            dimension_semantics=("arbitrary",), vmem_limit_bytes=VMEM_LIMIT),
        name="outproj",
    )(x, mix_a, mix_b, w_out_bf16, w_out_bf16, g)


def _ffn_up_kernel(hn_ref, wg_ref, wu_ref, act_ref):
    hn = hn_ref[...]
    for c in range(act_ref.shape[1] // MXU_COLS):
        cols = slice(c * MXU_COLS, (c + 1) * MXU_COLS)
        gate = jnp.dot(hn, wg_ref[:, cols].astype(BF16), preferred_element_type=F32)
        up = jnp.dot(hn, wu_ref[:, cols].astype(BF16), preferred_element_type=F32)
        act_ref[:, cols] = (gate * jax.nn.sigmoid(gate) * up).astype(BF16)


def _ffn_up(hn, wg, wu, tm=2048, tf=512):
    return pl.pallas_call(
        _ffn_up_kernel,
        grid=(D_FF // tf, SEQ // tm),
        in_specs=[
            pl.BlockSpec((tm, D_MODEL), lambda f, i: (i, 0)),
            pl.BlockSpec((D_MODEL, tf), lambda f, i: (0, f)),
            pl.BlockSpec((D_MODEL, tf), lambda f, i: (0, f)),
        ],
        out_specs=pl.BlockSpec((tm, tf), lambda f, i: (i, f)),
        out_shape=jax.ShapeDtypeStruct((SEQ, D_FF), BF16),
        compiler_params=pltpu.CompilerParams(
            dimension_semantics=("arbitrary", "arbitrary"),
            vmem_limit_bytes=VMEM_LIMIT),
        name="ffn_up",
    )(hn, wg, wu)


def _ffn_down_kernel(act_ref, wd_ref, x1_ref, g_ref, o_ref):
    n_tiles = pl.num_programs(1)
    tn = wd_ref.shape[1]
    for n in range(o_ref.shape[1] // tn):
        @pl.when(pl.program_id(1) == n)
        def _(n=n):
            o_ref[:, n * tn:(n + 1) * tn] = x1_ref[...] + jnp.dot(
                act_ref[...], wd_ref[...], preferred_element_type=F32)

    @pl.when(pl.program_id(1) == n_tiles - 1)
    def _():
        x2 = o_ref[...]
        o_ref[...] = x2 * _rms_scale(x2) * g_ref[...]


def _ffn_down(act, wd_bf16, x1, g, tm=1024, tn=512):
    return pl.pallas_call(
        _ffn_down_kernel,
        grid=(SEQ // tm, D_MODEL // tn),
        in_specs=[
            pl.BlockSpec((tm, D_FF), lambda i, n: (i, 0)),
            pl.BlockSpec((D_FF, tn), lambda i, n: (0, n)),
            pl.BlockSpec((tm, tn), lambda i, n: (i, n)),
            pl.BlockSpec((1, D_MODEL), lambda i, n: (0, 0)),
        ],
        out_specs=pl.BlockSpec((tm, D_MODEL), lambda i, n: (i, 0)),
        out_shape=jax.ShapeDtypeStruct((SEQ, D_MODEL), F32),
        compiler_params=pltpu.CompilerParams(
            dimension_semantics=("arbitrary", "arbitrary"),
            vmem_limit_bytes=FFN_DOWN_VMEM_LIMIT),
        name="ffn_down",
    )(act, wd_bf16, x1, g)


def kernel(x, norm_mix_g, w_in, a_ln_g, a_ln_b, a_w_s, a_b_s, s5_lambda_re, s5_lambda_im,
           s5_log_dt, s5_b_re, s5_b_im, s5_c_re, s5_c_im, s5_d, s5_w_glu, out_norm_a_g,
           out_norm_b_g, w_out, norm_ffn_g, w_gate, w_up, w_down, final_norm_g):
    assert x.shape == (1, SEQ, D_MODEL) and norm_mix_g.shape[0] == 1
    xs = x.reshape(SEQ, D_MODEL)
    l = 0

    bias_full = jnp.repeat(jnp.transpose(a_b_s[l]), A_HEAD_DIM, axis=1)
    wb, wc, wglu, a_re, a_im = _s5_params(
        s5_lambda_re[l], s5_lambda_im[l], s5_log_dt[l], s5_b_re[l], s5_b_im[l],
        s5_c_re[l], s5_c_im[l], s5_w_glu[l])
    mix_a, mix_b, w_out_bf16, w_down_bf16 = _mixer(
        xs, norm_mix_g[l][None], w_in[l].astype(BF16), a_ln_g[l][None], a_ln_b[l][None],
        a_w_s[l], bias_full, out_norm_a_g[l][None], w_out[l], w_down[l],
        wb, wc, wglu, a_re, a_im, s5_d[l][None], out_norm_b_g[l][None])

    x1, hn = _outproj(xs, mix_a, mix_b, w_out_bf16, norm_ffn_g[l][None])

    act = _ffn_up(hn, w_gate[l], w_up[l])
    out = _ffn_down(act, w_down_bf16, x1, final_norm_g[None])
    return out.reshape(1, SEQ, D_MODEL)
```

```python
import math

import jax
import jax.numpy as jnp
from jax import lax
from jax.experimental import pallas as pl
from jax.experimental.pallas import tpu as pltpu

F32 = jnp.float32
BF16 = jnp.bfloat16

D_MODEL = 2048
SEQ = 8192
CHUNK = 64
A_WIDTH = 1024
A_HEADS = 8
A_HEAD_DIM = 128
GMLP_BLOCK = 128
B_WIDTH = 1024
S5_GROUP_CH = 16
S5_GROUPS = 64
S5_STATE = 64
IN_WIDTH = 3072
D_FF = 5632
EPS = 1e-6

LANES = 128
SUBLANES = 8
MXU_COLS = 256
VMEM_LIMIT = 56 * 1024 * 1024
FFN_DOWN_VMEM_LIMIT = 60 * 1024 * 1024

S5_LANE_TILES = B_WIDTH // LANES
S5_GROUPS_PER_TILE = LANES // S5_GROUP_CH
S5_STATE_TILES = S5_GROUPS * S5_STATE // LANES
S5_TILES_PER_LANE_TILE = S5_STATE_TILES // S5_LANE_TILES
S5_STATE_VREGS = S5_STATE_TILES // SUBLANES
S5_T = 512
S5_HALVES = 2
S5_PITCH = S5_T + 4


def _gelu(x):
    return 0.5 * x * (1.0 + lax.erf(x * (1.0 / math.sqrt(2.0))))


def _rms_scale(x):
    return lax.rsqrt(jnp.mean(x * x, axis=-1, keepdims=True) + EPS)


def _layernorm_bf16(z, g_ref, b_ref):
    mu = jnp.mean(z, axis=-1, keepdims=True)
    zc = z - mu
    var = jnp.mean(zc * zc, axis=-1, keepdims=True)
    return (zc * lax.rsqrt(var + EPS) * g_ref[...] + b_ref[...]).astype(BF16)


def _spatial_mix(v, ws_ref):
    n_blocks = v.shape[0] // GMLP_BLOCK
    ci = lax.broadcasted_iota(jnp.int32, (GMLP_BLOCK, GMLP_BLOCK), 0) // CHUNK
    cj = lax.broadcasted_iota(jnp.int32, (GMLP_BLOCK, GMLP_BLOCK), 1) // CHUNK
    mask = ci >= cj
    heads = []
    for h in range(A_HEADS):
        w = jnp.where(mask, ws_ref[h], 0.0).astype(BF16)
        cols = slice(h * A_HEAD_DIM, (h + 1) * A_HEAD_DIM)
        rhs = jnp.concatenate(
            [v[n * GMLP_BLOCK:(n + 1) * GMLP_BLOCK, cols] for n in range(n_blocks)], axis=1)
        heads.append(jnp.dot(w, rhs, preferred_element_type=F32))
    return jnp.concatenate(
        [jnp.concatenate([hd[:, n * A_HEAD_DIM:(n + 1) * A_HEAD_DIM] for hd in heads], axis=1)
         for n in range(n_blocks)], axis=0)


def _mixer_kernel(x_ref, g_ref, w_ref, lng_ref, lnb_ref, ws_ref, bias_ref, oga_ref, wo_ref,
                  wb_ref, wc_ref, wglu_ref, are_ref, aim_ref, d_ref, ogb_ref,
                  mixa_ref, mixb_ref, wob_ref, st_ref, bu_ref):
    t_rows = x_ref.shape[0]
    tile = lambda k: slice(k * LANES, (k + 1) * LANES)
    half_lane_tiles = S5_LANE_TILES // S5_HALVES
    half_state_tiles = S5_STATE_TILES // S5_HALVES
    half_vregs = S5_STATE_VREGS // S5_HALVES
    im_base = half_state_tiles * S5_PITCH

    def scratch_rows(local_tile):
        return slice(local_tile * S5_PITCH, local_tile * S5_PITCH + t_rows)

    def half_state_tiles_of(kl):
        re = [kl * S5_TILES_PER_LANE_TILE + c for c in range(S5_TILES_PER_LANE_TILE)]
        return re + [half_state_tiles + q for q in re]

    @pl.when(pl.program_id(0) == 0)
    def _():
        st_ref[...] = jnp.zeros_like(st_ref)

    wob_ref[...] = wo_ref[...].astype(BF16)

    x = x_ref[...]
    xn = (x * _rms_scale(x) * g_ref[...]).astype(BF16)
    u = jnp.dot(xn, w_ref[:, 2 * A_WIDTH:], preferred_element_type=F32)

    h_pre, yc, ys, gates = [], [], [], []
    for hf in range(S5_HALVES):
        for kl in range(half_lane_tiles):
            k = hf * half_lane_tiles + kl
            res = jnp.dot(u[:, tile(k)].astype(BF16), wb_ref[k],
                          preferred_element_type=F32)
            for c, q in enumerate(half_state_tiles_of(kl)):
                bu_ref[scratch_rows(q), :] = res[:, tile(c)]

        gates += [jnp.dot(ys[k].astype(BF16), wglu_ref[k], preferred_element_type=F32)
                  for k in range(len(gates), len(ys))]

        h_pre.append(jnp.dot(xn, w_ref[:, hf * A_WIDTH:(hf + 1) * A_WIDTH],
                             preferred_element_type=F32))

        vregs = [hf * half_vregs + m for m in range(half_vregs)]
        a_re = [are_ref[gm] for gm in vregs]
        a_im = [aim_ref[gm] for gm in vregs]
        s_re = [st_ref[gm] for gm in vregs]
        s_im = [st_ref[S5_STATE_VREGS + gm] for gm in vregs]
        for t in range(t_rows):
            for m in range(half_vregs):
                re_rows = pl.ds(t + SUBLANES * m * S5_PITCH, SUBLANES, stride=S5_PITCH)
                im_rows = pl.ds(t + im_base + SUBLANES * m * S5_PITCH, SUBLANES, stride=S5_PITCH)
                nr = a_re[m] * s_re[m] - a_im[m] * s_im[m] + bu_ref[re_rows, :]
                ni = a_re[m] * s_im[m] + a_im[m] * s_re[m] + bu_ref[im_rows, :]
                bu_ref[re_rows, :] = nr
                bu_ref[im_rows, :] = ni
                s_re[m], s_im[m] = nr, ni
        for m, gm in enumerate(vregs):
            st_ref[gm] = s_re[m]
            st_ref[S5_STATE_VREGS + gm] = s_im[m]

        for kl in range(half_lane_tiles):
            k = hf * half_lane_tiles + kl
            lhs = jnp.concatenate([bu_ref[scratch_rows(q), :] for q in half_state_tiles_of(kl)],
                                  axis=1).astype(BF16)
            yc.append(jnp.dot(lhs, wc_ref[k], preferred_element_type=F32))

        ys += [_gelu(yc[k] + d_ref[:, tile(k)] * u[:, tile(k)]) for k in range(len(ys), len(yc))]
        if hf == 0:
            zu = _gelu(h_pre[0])

    gates += [jnp.dot(ys[k].astype(BF16), wglu_ref[k], preferred_element_type=F32)
              for k in range(len(gates), len(ys))]
    yb = jnp.concatenate([ys[k] * jax.nn.sigmoid(gates[k]) for k in range(S5_LANE_TILES)], axis=1)
    mixb_ref[...] = (yb * _rms_scale(yb) * ogb_ref[...]).astype(BF16)

    v = _layernorm_bf16(_gelu(h_pre[1]), lng_ref, lnb_ref)
    bias = jnp.concatenate([bias_ref[...]] * (t_rows // GMLP_BLOCK), axis=0)
    ya = zu * (_spatial_mix(v, ws_ref) + bias)
    mixa_ref[...] = (ya * _rms_scale(ya) * oga_ref[...]).astype(BF16)


def _mixer(x, g, w_bf16, ln_g, ln_b, w_s, bias_full, out_a_g, w_out,
           wb, wc, wglu, a_re, a_im, d, out_b_g):
    tm = S5_T
    n_steps = SEQ // tm
    wo_rows = w_out.shape[0] // n_steps
    n_state_cols = 2 * S5_GROUPS_PER_TILE * S5_STATE
    const2 = lambda i: (0, 0)
    const3 = lambda i: (0, 0, 0)
    once = dict(pipeline_mode=pl.Buffered(1))
    return pl.pallas_call(
        _mixer_kernel,
        grid=(n_steps,),
        in_specs=[
            pl.BlockSpec((tm, D_MODEL), lambda i: (i, 0)),
            pl.BlockSpec((1, D_MODEL), const2),
            pl.BlockSpec((D_MODEL, IN_WIDTH), const2, **once),
            pl.BlockSpec((1, A_WIDTH), const2),
            pl.BlockSpec((1, A_WIDTH), const2),
            pl.BlockSpec((A_HEADS, GMLP_BLOCK, GMLP_BLOCK), const3),
            pl.BlockSpec((GMLP_BLOCK, A_WIDTH), const2),
            pl.BlockSpec((1, A_WIDTH), const2),
            pl.BlockSpec((wo_rows, D_MODEL), lambda i: (i, 0)),
            pl.BlockSpec((S5_LANE_TILES, LANES, n_state_cols), const3, **once),
            pl.BlockSpec((S5_LANE_TILES, n_state_cols, LANES), const3, **once),
            pl.BlockSpec((S5_LANE_TILES, LANES, LANES), const3),
            pl.BlockSpec((S5_STATE_VREGS, SUBLANES, LANES), const3),
            pl.BlockSpec((S5_STATE_VREGS, SUBLANES, LANES), const3),
            pl.BlockSpec((1, B_WIDTH), const2),
            pl.BlockSpec((1, B_WIDTH), const2),
        ],
        out_specs=[
            pl.BlockSpec((tm, A_WIDTH), lambda i: (i, 0)),
            pl.BlockSpec((tm, B_WIDTH), lambda i: (i, 0)),
            pl.BlockSpec((wo_rows, D_MODEL), lambda i: (i, 0)),
        ],
        out_shape=[jax.ShapeDtypeStruct((SEQ, A_WIDTH), BF16),
                   jax.ShapeDtypeStruct((SEQ, B_WIDTH), BF16),
                   jax.ShapeDtypeStruct(w_out.shape, BF16)],
        scratch_shapes=[
            pltpu.VMEM((2 * S5_STATE_VREGS, SUBLANES, LANES), F32),
            pltpu.VMEM((2 * S5_STATE_TILES // S5_HALVES * S5_PITCH, LANES), F32),
        ],
        compiler_params=pltpu.CompilerParams(
            dimension_semantics=("arbitrary",), vmem_limit_bytes=VMEM_LIMIT),
        name="mixer",
    )(x, g, w_bf16, ln_g, ln_b, w_s, bias_full, out_a_g, w_out,
      wb, wc, wglu, a_re, a_im, d, out_b_g)


def _s5_discretise_kernel(lre_ref, lim_ref, ldt_ref, bre_ref, bim_ref,
                          abr_ref, abi_ref, bbr_ref, bbi_ref):
    lr, li = lre_ref[...], lim_ref[...]
    dt = jnp.exp(ldt_ref[...])
    mag = jnp.exp(lr * dt)
    ab_r = mag * jnp.cos(li * dt)
    ab_i = mag * jnp.sin(li * dt)
    den = lr * lr + li * li
    nr = ab_r - 1.0
    co_r = (nr * lr + ab_i * li) / den
    co_i = (ab_i * lr - nr * li) / den
    abr_ref[...] = ab_r
    abi_ref[...] = ab_i
    br, bi = bre_ref[...], bim_ref[...]
    bbr_ref[...] = co_r[:, None, :] * br - co_i[:, None, :] * bi
    bbi_ref[...] = co_r[:, None, :] * bi + co_i[:, None, :] * br


def _s5_params(lam_re, lam_im, log_dt, b_re, b_im, c_re, c_im, w_glu):
    g_p = jax.ShapeDtypeStruct((S5_GROUPS, S5_STATE), F32)
    g_h_p = jax.ShapeDtypeStruct((S5_GROUPS, S5_GROUP_CH, S5_STATE), F32)
    ab_r, ab_i, bbt_r, bbt_i = pl.pallas_call(
        _s5_discretise_kernel,
        out_shape=[g_p, g_p, g_h_p, g_h_p],
        name="s5_discretise",
    )(lam_re, lam_im, log_dt[:, None], jnp.swapaxes(b_re, 1, 2), jnp.swapaxes(b_im, 1, 2))

    gt = S5_GROUPS_PER_TILE
    on_diag = jnp.eye(gt, dtype=bool)[None, :, None, :, None]

    def blockdiag(blocks):
        _, r, c = blocks.shape
        b = blocks.astype(BF16).reshape(S5_LANE_TILES, gt, r, 1, c)
        out = jnp.where(on_diag, b, jnp.zeros((), BF16))
        return out.reshape(S5_LANE_TILES, gt * r, gt * c)

    wb = jnp.concatenate([blockdiag(bbt_r), blockdiag(bbt_i)], axis=2)
    wc = jnp.concatenate([blockdiag(jnp.swapaxes(c_re, 1, 2)),
                          blockdiag(-jnp.swapaxes(c_im, 1, 2))], axis=1)
    wglu = blockdiag(w_glu)
    a_re = ab_r.reshape(S5_STATE_VREGS, SUBLANES, LANES)
    a_im = ab_i.reshape(S5_STATE_VREGS, SUBLANES, LANES)
    return wb, wc, wglu, a_re, a_im


def _outproj_kernel(x_ref, ma_ref, mb_ref, wa_ref, wb_ref, g_ref, x1_ref, hn_ref):
    half = x_ref.shape[0] // 2
    for r in range(2):
        rows = slice(r * half, (r + 1) * half)
        acc = jnp.dot(ma_ref[rows, :], wa_ref[...], preferred_element_type=F32)
        acc = acc + jnp.dot(mb_ref[rows, :], wb_ref[...], preferred_element_type=F32)
        x1 = x_ref[rows, :] + acc
        x1_ref[rows, :] = x1
        hn_ref[rows, :] = (x1 * _rms_scale(x1) * g_ref[...]).astype(BF16)


def _outproj(x, mix_a, mix_b, w_out_bf16, g, tm=512):
    return pl.pallas_call(
        _outproj_kernel,
        grid=(SEQ // tm,),
        in_specs=[
            pl.BlockSpec((tm, D_MODEL), lambda i: (i, 0)),
            pl.BlockSpec((tm, A_WIDTH), lambda i: (i, 0)),
            pl.BlockSpec((tm, B_WIDTH), lambda i: (i, 0)),
            pl.BlockSpec((A_WIDTH, D_MODEL), lambda i: (0, 0)),
            pl.BlockSpec((B_WIDTH, D_MODEL), lambda i: (1, 0)),
            pl.BlockSpec((1, D_MODEL), lambda i: (0, 0)),
        ],
        out_specs=[
            pl.BlockSpec((tm, D_MODEL), lambda i: (i, 0)),
            pl.BlockSpec((tm, D_MODEL), lambda i: (i, 0)),
        ],
        out_shape=[jax.ShapeDtypeStruct((SEQ, D_MODEL), F32),
                   jax.ShapeDtypeStruct((SEQ, D_MODEL), BF16)],
        compiler_params=pltpu.CompilerParams(
            dimension_semantics=("arbitrary",), vmem_limit_bytes=VMEM_LIMIT),
        name="outproj",
    )(x, mix_a, mix_b, w_out_bf16, w_out_bf16, g)


def _ffn_up_kernel(hn_ref, wg_ref, wu_ref, wd_ref, act_ref, wdb_ref):
    wdb_ref[...] = wd_ref[...].astype(BF16)
    hn = hn_ref[...]
    for c in range(act_ref.shape[1] // MXU_COLS):
        cols = slice(c * MXU_COLS, (c + 1) * MXU_COLS)
        gate = jnp.dot(hn, wg_ref[:, cols].astype(BF16), preferred_element_type=F32)
        up = jnp.dot(hn, wu_ref[:, cols].astype(BF16), preferred_element_type=F32)
        act_ref[:, cols] = (gate * jax.nn.sigmoid(gate) * up).astype(BF16)


def _ffn_up(hn, wg, wu, wd, tm=2048, tf=512):
    n_i = SEQ // tm
    wd_rows = wd.shape[0] // (D_FF // tf * n_i)
    wd_slab = pl.BlockSpec((wd_rows, D_MODEL), lambda f, i: (f * n_i + i, 0))
    return pl.pallas_call(
        _ffn_up_kernel,
        grid=(D_FF // tf, n_i),
        in_specs=[
            pl.BlockSpec((tm, D_MODEL), lambda f, i: (i, 0)),
            pl.BlockSpec((D_MODEL, tf), lambda f, i: (0, f)),
            pl.BlockSpec((D_MODEL, tf), lambda f, i: (0, f)),
            wd_slab,
        ],
        out_specs=[pl.BlockSpec((tm, tf), lambda f, i: (i, f)), wd_slab],
        out_shape=[jax.ShapeDtypeStruct((SEQ, D_FF), BF16),
                   jax.ShapeDtypeStruct(wd.shape, BF16)],
        compiler_params=pltpu.CompilerParams(
            dimension_semantics=("arbitrary", "arbitrary"),
            vmem_limit_bytes=VMEM_LIMIT),
        name="ffn_up",
    )(hn, wg, wu, wd)


def _ffn_down_kernel(act_ref, wd_ref, x1_ref, g_ref, o_ref):
    n_tiles = pl.num_programs(1)
    tn = wd_ref.shape[1]
    for n in range(o_ref.shape[1] // tn):
        @pl.when(pl.program_id(1) == n)
        def _(n=n):
            o_ref[:, n * tn:(n + 1) * tn] = x1_ref[...] + jnp.dot(
                act_ref[...], wd_ref[...], preferred_element_type=F32)

    @pl.when(pl.program_id(1) == n_tiles - 1)
    def _():
        x2 = o_ref[...]
        o_ref[...] = x2 * _rms_scale(x2) * g_ref[...]


def _ffn_down(act, wd_bf16, x1, g, tm=1024, tn=512):
    return pl.pallas_call(
        _ffn_down_kernel,
        grid=(SEQ // tm, D_MODEL // tn),
        in_specs=[
            pl.BlockSpec((tm, D_FF), lambda i, n: (i, 0)),
            pl.BlockSpec((D_FF, tn), lambda i, n: (0, n)),
            pl.BlockSpec((tm, tn), lambda i, n: (i, n)),
            pl.BlockSpec((1, D_MODEL), lambda i, n: (0, 0)),
        ],
        out_specs=pl.BlockSpec((tm, D_MODEL), lambda i, n: (i, 0)),
        out_shape=jax.ShapeDtypeStruct((SEQ, D_MODEL), F32),
        compiler_params=pltpu.CompilerParams(
            dimension_semantics=("arbitrary", "arbitrary"),
            vmem_limit_bytes=FFN_DOWN_VMEM_LIMIT),
        name="ffn_down",
    )(act, wd_bf16, x1, g)


def kernel(x, norm_mix_g, w_in, a_ln_g, a_ln_b, a_w_s, a_b_s, s5_lambda_re, s5_lambda_im,
           s5_log_dt, s5_b_re, s5_b_im, s5_c_re, s5_c_im, s5_d, s5_w_glu, out_norm_a_g,
           out_norm_b_g, w_out, norm_ffn_g, w_gate, w_up, w_down, final_norm_g):
    assert x.shape == (1, SEQ, D_MODEL) and norm_mix_g.shape[0] == 1
    xs = x.reshape(SEQ, D_MODEL)
    l = 0

    bias_full = jnp.repeat(jnp.transpose(a_b_s[l]), A_HEAD_DIM, axis=1)
    wb, wc, wglu, a_re, a_im = _s5_params(
        s5_lambda_re[l], s5_lambda_im[l], s5_log_dt[l], s5_b_re[l], s5_b_im[l],
        s5_c_re[l], s5_c_im[l], s5_w_glu[l])
    mix_a, mix_b, w_out_bf16 = _mixer(
        xs, norm_mix_g[l][None], w_in[l].astype(BF16), a_ln_g[l][None], a_ln_b[l][None],
        a_w_s[l], bias_full, out_norm_a_g[l][None], w_out[l],
        wb, wc, wglu, a_re, a_im, s5_d[l][None], out_norm_b_g[l][None])

    x1, hn = _outproj(xs, mix_a, mix_b, w_out_bf16, norm_ffn_g[l][None])

    act, w_down_bf16 = _ffn_up(hn, w_gate[l], w_up[l], w_down[l])
    out = _ffn_down(act, w_down_bf16, x1, final_norm_g[None])
    return out.reshape(1, SEQ, D_MODEL)
```

```python
import math

import jax
import jax.numpy as jnp
from jax import lax
from jax.experimental import pallas as pl
from jax.experimental.pallas import tpu as pltpu

F32 = jnp.float32
BF16 = jnp.bfloat16

D_MODEL = 2048
SEQ = 8192
CHUNK = 64
A_WIDTH = 1024
A_HEADS = 8
A_HEAD_DIM = 128
GMLP_BLOCK = 128
B_WIDTH = 1024
S5_GROUP_CH = 16
S5_GROUPS = 64
S5_STATE = 64
IN_WIDTH = 3072
D_FF = 5632
EPS = 1e-6

LANES = 128
SUBLANES = 8
MXU_COLS = 256
VMEM_LIMIT = 56 * 1024 * 1024
FFN_DOWN_VMEM_LIMIT = 60 * 1024 * 1024

S5_LANE_TILES = B_WIDTH // LANES
S5_GROUPS_PER_TILE = LANES // S5_GROUP_CH
S5_STATE_TILES = S5_GROUPS * S5_STATE // LANES
S5_TILES_PER_LANE_TILE = S5_STATE_TILES // S5_LANE_TILES
S5_STATE_VREGS = S5_STATE_TILES // SUBLANES
S5_T = 512
S5_HALVES = 2
S5_PITCH = S5_T + 4


def _gelu(x):
    return 0.5 * x * (1.0 + lax.erf(x * (1.0 / math.sqrt(2.0))))


def _rms_scale(x):
    return lax.rsqrt(jnp.mean(x * x, axis=-1, keepdims=True) + EPS)


def _layernorm_bf16(z, g_ref, b_ref):
    mu = jnp.mean(z, axis=-1, keepdims=True)
    zc = z - mu
    var = jnp.mean(zc * zc, axis=-1, keepdims=True)
    return (zc * lax.rsqrt(var + EPS) * g_ref[...] + b_ref[...]).astype(BF16)


def _spatial_mix(v, ws_ref):
    n_blocks = v.shape[0] // GMLP_BLOCK
    ci = lax.broadcasted_iota(jnp.int32, (GMLP_BLOCK, GMLP_BLOCK), 0) // CHUNK
    cj = lax.broadcasted_iota(jnp.int32, (GMLP_BLOCK, GMLP_BLOCK), 1) // CHUNK
    mask = ci >= cj
    heads = []
    for h in range(A_HEADS):
        w = jnp.where(mask, ws_ref[h], 0.0).astype(BF16)
        cols = slice(h * A_HEAD_DIM, (h + 1) * A_HEAD_DIM)
        rhs = jnp.concatenate(
            [v[n * GMLP_BLOCK:(n + 1) * GMLP_BLOCK, cols] for n in range(n_blocks)], axis=1)
        heads.append(jnp.dot(w, rhs, preferred_element_type=F32))
    return jnp.concatenate(
        [jnp.concatenate([hd[:, n * A_HEAD_DIM:(n + 1) * A_HEAD_DIM] for hd in heads], axis=1)
         for n in range(n_blocks)], axis=0)


def _mixer_kernel(x_ref, g_ref, w_ref, lng_ref, lnb_ref, ws_ref, bias_ref, oga_ref, wo_ref,
                  wb_ref, wc_ref, wglu_ref, are_ref, aim_ref, d_ref, ogb_ref,
                  mixa_ref, mixb_ref, wob_ref, st_ref, bu_ref):
    t_rows = x_ref.shape[0]
    tile = lambda k: slice(k * LANES, (k + 1) * LANES)
    half_lane_tiles = S5_LANE_TILES // S5_HALVES
    half_state_tiles = S5_STATE_TILES // S5_HALVES
    half_vregs = S5_STATE_VREGS // S5_HALVES

    def tile_base(q):
        return q * S5_PITCH + q // SUBLANES

    def scratch_rows(q):
        return slice(tile_base(q), tile_base(q) + t_rows)

    def half_state_tiles_of(kl):
        re = [kl * S5_TILES_PER_LANE_TILE + c for c in range(S5_TILES_PER_LANE_TILE)]
        return re + [half_state_tiles + q for q in re]

    @pl.when(pl.program_id(0) == 0)
    def _():
        st_ref[...] = jnp.zeros_like(st_ref)

    wob_ref[...] = wo_ref[...].astype(BF16)

    x = x_ref[...]
    xn = (x * _rms_scale(x) * g_ref[...]).astype(BF16)
    u = jnp.dot(xn, w_ref[:, 2 * A_WIDTH:], preferred_element_type=F32)

    h_pre, yc, ys, gates = [], [], [], []
    for hf in range(S5_HALVES):
        for kl in range(half_lane_tiles):
            k = hf * half_lane_tiles + kl
            res = jnp.dot(u[:, tile(k)].astype(BF16), wb_ref[k],
                          preferred_element_type=F32)
            for c, q in enumerate(half_state_tiles_of(kl)):
                bu_ref[scratch_rows(q), :] = res[:, tile(c)]

        gates += [jnp.dot(ys[k].astype(BF16), wglu_ref[k], preferred_element_type=F32)
                  for k in range(len(gates), len(ys))]

        h_pre.append(jnp.dot(xn, w_ref[:, hf * A_WIDTH:(hf + 1) * A_WIDTH],
                             preferred_element_type=F32))

        vregs = [hf * half_vregs + m for m in range(half_vregs)]
        a_re = [are_ref[gm] for gm in vregs]
        a_im = [aim_ref[gm] for gm in vregs]
        s_re = [st_ref[gm] for gm in vregs]
        s_im = [st_ref[S5_STATE_VREGS + gm] for gm in vregs]
        for t in range(t_rows):
            for m in range(half_vregs):
                re_rows = pl.ds(t + tile_base(SUBLANES * m), SUBLANES, stride=S5_PITCH)
                im_rows = pl.ds(t + tile_base(half_state_tiles + SUBLANES * m), SUBLANES,
                                stride=S5_PITCH)
                nr = a_re[m] * s_re[m] - a_im[m] * s_im[m] + bu_ref[re_rows, :]
                ni = a_re[m] * s_im[m] + a_im[m] * s_re[m] + bu_ref[im_rows, :]
                bu_ref[re_rows, :] = nr
                bu_ref[im_rows, :] = ni
                s_re[m], s_im[m] = nr, ni
        for m, gm in enumerate(vregs):
            st_ref[gm] = s_re[m]
            st_ref[S5_STATE_VREGS + gm] = s_im[m]

        for kl in range(half_lane_tiles):
            k = hf * half_lane_tiles + kl
            lhs = jnp.concatenate([bu_ref[scratch_rows(q), :] for q in half_state_tiles_of(kl)],
                                  axis=1).astype(BF16)
            yc.append(jnp.dot(lhs, wc_ref[k], preferred_element_type=F32))

        ys += [_gelu(yc[k] + d_ref[:, tile(k)] * u[:, tile(k)]) for k in range(len(ys), len(yc))]
        if hf == 0:
            zu = _gelu(h_pre[0])

    gates += [jnp.dot(ys[k].astype(BF16), wglu_ref[k], preferred_element_type=F32)
              for k in range(len(gates), len(ys))]
    yb = jnp.concatenate([ys[k] * jax.nn.sigmoid(gates[k]) for k in range(S5_LANE_TILES)], axis=1)
    mixb_ref[...] = (yb * _rms_scale(yb) * ogb_ref[...]).astype(BF16)

    v = _layernorm_bf16(_gelu(h_pre[1]), lng_ref, lnb_ref)
    bias = jnp.concatenate([bias_ref[...]] * (t_rows // GMLP_BLOCK), axis=0)
    ya = zu * (_spatial_mix(v, ws_ref) + bias)
    mixa_ref[...] = (ya * _rms_scale(ya) * oga_ref[...]).astype(BF16)


def _mixer(x, g, w_bf16, ln_g, ln_b, w_s, bias_full, out_a_g, w_out,
           wb, wc, wglu, a_re, a_im, d, out_b_g):
    tm = S5_T
    n_steps = SEQ // tm
    wo_rows = w_out.shape[0] // n_steps
    n_state_cols = 2 * S5_GROUPS_PER_TILE * S5_STATE
    const2 = lambda i: (0, 0)
    const3 = lambda i: (0, 0, 0)
    once = dict(pipeline_mode=pl.Buffered(1))
    return pl.pallas_call(
        _mixer_kernel,
        grid=(n_steps,),
        in_specs=[
            pl.BlockSpec((tm, D_MODEL), lambda i: (i, 0)),
            pl.BlockSpec((1, D_MODEL), const2),
            pl.BlockSpec((D_MODEL, IN_WIDTH), const2, **once),
            pl.BlockSpec((1, A_WIDTH), const2),
            pl.BlockSpec((1, A_WIDTH), const2),
            pl.BlockSpec((A_HEADS, GMLP_BLOCK, GMLP_BLOCK), const3),
            pl.BlockSpec((GMLP_BLOCK, A_WIDTH), const2),
            pl.BlockSpec((1, A_WIDTH), const2),
            pl.BlockSpec((wo_rows, D_MODEL), lambda i: (i, 0)),
            pl.BlockSpec((S5_LANE_TILES, LANES, n_state_cols), const3, **once),
            pl.BlockSpec((S5_LANE_TILES, n_state_cols, LANES), const3, **once),
            pl.BlockSpec((S5_LANE_TILES, LANES, LANES), const3),
            pl.BlockSpec((S5_STATE_VREGS, SUBLANES, LANES), const3),
            pl.BlockSpec((S5_STATE_VREGS, SUBLANES, LANES), const3),
            pl.BlockSpec((1, B_WIDTH), const2),
            pl.BlockSpec((1, B_WIDTH), const2),
        ],
        out_specs=[
            pl.BlockSpec((tm, A_WIDTH), lambda i: (i, 0)),
            pl.BlockSpec((tm, B_WIDTH), lambda i: (i, 0)),
            pl.BlockSpec((wo_rows, D_MODEL), lambda i: (i, 0)),
        ],
        out_shape=[jax.ShapeDtypeStruct((SEQ, A_WIDTH), BF16),
                   jax.ShapeDtypeStruct((SEQ, B_WIDTH), BF16),
                   jax.ShapeDtypeStruct(w_out.shape, BF16)],
        scratch_shapes=[
            pltpu.VMEM((2 * S5_STATE_VREGS, SUBLANES, LANES), F32),
            pltpu.VMEM((2 * S5_STATE_TILES // S5_HALVES * S5_PITCH + SUBLANES, LANES), F32),
        ],
        compiler_params=pltpu.CompilerParams(
            dimension_semantics=("arbitrary",), vmem_limit_bytes=VMEM_LIMIT),
        name="mixer",
    )(x, g, w_bf16, ln_g, ln_b, w_s, bias_full, out_a_g, w_out,
      wb, wc, wglu, a_re, a_im, d, out_b_g)


def _s5_discretise_kernel(lre_ref, lim_ref, ldt_ref, bre_ref, bim_ref,
                          abr_ref, abi_ref, bbr_ref, bbi_ref):
    lr, li = lre_ref[...], lim_ref[...]
    dt = jnp.exp(ldt_ref[...])
    mag = jnp.exp(lr * dt)
    ab_r = mag * jnp.cos(li * dt)
    ab_i = mag * jnp.sin(li * dt)
    den = lr * lr + li * li
    nr = ab_r - 1.0
    co_r = (nr * lr + ab_i * li) / den
    co_i = (ab_i * lr - nr * li) / den
    abr_ref[...] = ab_r
    abi_ref[...] = ab_i
    br, bi = bre_ref[...], bim_ref[...]
    bbr_ref[...] = co_r[:, None, :] * br - co_i[:, None, :] * bi
    bbi_ref[...] = co_r[:, None, :] * bi + co_i[:, None, :] * br


def _s5_params(lam_re, lam_im, log_dt, b_re, b_im, c_re, c_im, w_glu):
    g_p = jax.ShapeDtypeStruct((S5_GROUPS, S5_STATE), F32)
    g_h_p = jax.ShapeDtypeStruct((S5_GROUPS, S5_GROUP_CH, S5_STATE), F32)
    ab_r, ab_i, bbt_r, bbt_i = pl.pallas_call(
        _s5_discretise_kernel,
        out_shape=[g_p, g_p, g_h_p, g_h_p],
        name="s5_discretise",
    )(lam_re, lam_im, log_dt[:, None], jnp.swapaxes(b_re, 1, 2), jnp.swapaxes(b_im, 1, 2))

    gt = S5_GROUPS_PER_TILE
    on_diag = jnp.eye(gt, dtype=bool)[None, :, None, :, None]

    def blockdiag(blocks):
        _, r, c = blocks.shape
        b = blocks.astype(BF16).reshape(S5_LANE_TILES, gt, r, 1, c)
        out = jnp.where(on_diag, b, jnp.zeros((), BF16))
        return out.reshape(S5_LANE_TILES, gt * r, gt * c)

    wb = jnp.concatenate([blockdiag(bbt_r), blockdiag(bbt_i)], axis=2)
    wc = jnp.concatenate([blockdiag(jnp.swapaxes(c_re, 1, 2)),
                          blockdiag(-jnp.swapaxes(c_im, 1, 2))], axis=1)
    wglu = blockdiag(w_glu)
    a_re = ab_r.reshape(S5_STATE_VREGS, SUBLANES, LANES)
    a_im = ab_i.reshape(S5_STATE_VREGS, SUBLANES, LANES)
    return wb, wc, wglu, a_re, a_im


def _outproj_kernel(x_ref, ma_ref, mb_ref, wa_ref, wb_ref, g_ref, x1_ref, hn_ref):
    half = x_ref.shape[0] // 2
    for r in range(2):
        rows = slice(r * half, (r + 1) * half)
        acc = jnp.dot(ma_ref[rows, :], wa_ref[...], preferred_element_type=F32)
        acc = acc + jnp.dot(mb_ref[rows, :], wb_ref[...], preferred_element_type=F32)
        x1 = x_ref[rows, :] + acc
        x1_ref[rows, :] = x1
        hn_ref[rows, :] = (x1 * _rms_scale(x1) * g_ref[...]).astype(BF16)


def _outproj(x, mix_a, mix_b, w_out_bf16, g, tm=512):
    return pl.pallas_call(
        _outproj_kernel,
        grid=(SEQ // tm,),
        in_specs=[
            pl.BlockSpec((tm, D_MODEL), lambda i: (i, 0)),
            pl.BlockSpec((tm, A_WIDTH), lambda i: (i, 0)),
            pl.BlockSpec((tm, B_WIDTH), lambda i: (i, 0)),
            pl.BlockSpec((A_WIDTH, D_MODEL), lambda i: (0, 0)),
            pl.BlockSpec((B_WIDTH, D_MODEL), lambda i: (1, 0)),
            pl.BlockSpec((1, D_MODEL), lambda i: (0, 0)),
        ],
        out_specs=[
            pl.BlockSpec((tm, D_MODEL), lambda i: (i, 0)),
            pl.BlockSpec((tm, D_MODEL), lambda i: (i, 0)),
        ],
        out_shape=[jax.ShapeDtypeStruct((SEQ, D_MODEL), F32),
                   jax.ShapeDtypeStruct((SEQ, D_MODEL), BF16)],
        compiler_params=pltpu.CompilerParams(
            dimension_semantics=("arbitrary",), vmem_limit_bytes=VMEM_LIMIT),
        name="outproj",
    )(x, mix_a, mix_b, w_out_bf16, w_out_bf16, g)


def _ffn_up_kernel(hn_ref, wg_ref, wu_ref, wd_ref, act_ref, wdb_ref):
    wdb_ref[...] = wd_ref[...].astype(BF16)
    hn = hn_ref[...]
    for c in range(act_ref.shape[1] // MXU_COLS):
        cols = slice(c * MXU_COLS, (c + 1) * MXU_COLS)
        gate = jnp.dot(hn, wg_ref[:, cols].astype(BF16), preferred_element_type=F32)
        up = jnp.dot(hn, wu_ref[:, cols].astype(BF16), preferred_element_type=F32)
        act_ref[:, cols] = (gate * jax.nn.sigmoid(gate) * up).astype(BF16)


def _ffn_up(hn, wg, wu, wd, tm=2048, tf=512):
    n_i = SEQ // tm
    wd_rows = wd.shape[0] // (D_FF // tf * n_i)
    wd_slab = pl.BlockSpec((wd_rows, D_MODEL), lambda f, i: (f * n_i + i, 0))
    return pl.pallas_call(
        _ffn_up_kernel,
        grid=(D_FF // tf, n_i),
        in_specs=[
            pl.BlockSpec((tm, D_MODEL), lambda f, i: (i, 0)),
            pl.BlockSpec((D_MODEL, tf), lambda f, i: (0, f)),
            pl.BlockSpec((D_MODEL, tf), lambda f, i: (0, f)),
            wd_slab,
        ],
        out_specs=[pl.BlockSpec((tm, tf), lambda f, i: (i, f)), wd_slab],
        out_shape=[jax.ShapeDtypeStruct((SEQ, D_FF), BF16),
                   jax.ShapeDtypeStruct(wd.shape, BF16)],
        compiler_params=pltpu.CompilerParams(
            dimension_semantics=("arbitrary", "arbitrary"),
            vmem_limit_bytes=VMEM_LIMIT),
        name="ffn_up",
    )(hn, wg, wu, wd)


def _ffn_down_kernel(act_ref, wd_ref, x1_ref, g_ref, o_ref):
    n_tiles = pl.num_programs(1)
    tn = wd_ref.shape[1]
    for n in range(o_ref.shape[1] // tn):
        @pl.when(pl.program_id(1) == n)
        def _(n=n):
            o_ref[:, n * tn:(n + 1) * tn] = x1_ref[...] + jnp.dot(
                act_ref[...], wd_ref[...], preferred_element_type=F32)

    @pl.when(pl.program_id(1) == n_tiles - 1)
    def _():
        x2 = o_ref[...]
        o_ref[...] = x2 * _rms_scale(x2) * g_ref[...]


def _ffn_down(act, wd_bf16, x1, g, tm=1024, tn=512):
    return pl.pallas_call(
        _ffn_down_kernel,
        grid=(SEQ // tm, D_MODEL // tn),
        in_specs=[
            pl.BlockSpec((tm, D_FF), lambda i, n: (i, 0)),
            pl.BlockSpec((D_FF, tn), lambda i, n: (0, n)),
            pl.BlockSpec((tm, tn), lambda i, n: (i, n)),
            pl.BlockSpec((1, D_MODEL), lambda i, n: (0, 0)),
        ],
        out_specs=pl.BlockSpec((tm, D_MODEL), lambda i, n: (i, 0)),
        out_shape=jax.ShapeDtypeStruct((SEQ, D_MODEL), F32),
        compiler_params=pltpu.CompilerParams(
            dimension_semantics=("arbitrary", "arbitrary"),
            vmem_limit_bytes=FFN_DOWN_VMEM_LIMIT),
        name="ffn_down",
    )(act, wd_bf16, x1, g)


def kernel(x, norm_mix_g, w_in, a_ln_g, a_ln_b, a_w_s, a_b_s, s5_lambda_re, s5_lambda_im,
           s5_log_dt, s5_b_re, s5_b_im, s5_c_re, s5_c_im, s5_d, s5_w_glu, out_norm_a_g,
           out_norm_b_g, w_out, norm_ffn_g, w_gate, w_up, w_down, final_norm_g):
    assert x.shape == (1, SEQ, D_MODEL) and norm_mix_g.shape[0] == 1
    xs = x.reshape(SEQ, D_MODEL)
    l = 0

    bias_full = jnp.repeat(jnp.transpose(a_b_s[l]), A_HEAD_DIM, axis=1)
    wb, wc, wglu, a_re, a_im = _s5_params(
        s5_lambda_re[l], s5_lambda_im[l], s5_log_dt[l], s5_b_re[l], s5_b_im[l],
        s5_c_re[l], s5_c_im[l], s5_w_glu[l])
    mix_a, mix_b, w_out_bf16 = _mixer(
        xs, norm_mix_g[l][None], w_in[l].astype(BF16), a_ln_g[l][None], a_ln_b[l][None],
        a_w_s[l], bias_full, out_norm_a_g[l][None], w_out[l],
        wb, wc, wglu, a_re, a_im, s5_d[l][None], out_norm_b_g[l][None])

    x1, hn = _outproj(xs, mix_a, mix_b, w_out_bf16, norm_ffn_g[l][None])

    act, w_down_bf16 = _ffn_up(hn, w_gate[l], w_up[l], w_down[l])
    out = _ffn_down(act, w_down_bf16, x1, final_norm_g[None])
    return out.reshape(1, SEQ, D_MODEL)
```

```python
import math

import jax
import jax.numpy as jnp
from jax import lax
from jax.experimental import pallas as pl
from jax.experimental.pallas import tpu as pltpu

F32 = jnp.float32
BF16 = jnp.bfloat16

D_MODEL = 2048
SEQ = 8192
CHUNK = 64
A_WIDTH = 1024
A_HEADS = 8
A_HEAD_DIM = 128
GMLP_BLOCK = 128
B_WIDTH = 1024
S5_GROUP_CH = 16
S5_GROUPS = 64
S5_STATE = 64
IN_WIDTH = 3072
D_FF = 5632
EPS = 1e-6

LANES = 128
SUBLANES = 8
MXU_COLS = 256
VMEM_LIMIT = 56 * 1024 * 1024
FFN_DOWN_VMEM_LIMIT = 60 * 1024 * 1024

S5_LANE_TILES = B_WIDTH // LANES
S5_GROUPS_PER_TILE = LANES // S5_GROUP_CH
S5_STATE_TILES = S5_GROUPS * S5_STATE // LANES
S5_TILES_PER_LANE_TILE = S5_STATE_TILES // S5_LANE_TILES
S5_STATE_VREGS = S5_STATE_TILES // SUBLANES
S5_T = 512
S5_HALVES = 2
S5_PITCH = S5_T + 4


def _gelu(x):
    return 0.5 * x * (1.0 + lax.erf(x * (1.0 / math.sqrt(2.0))))


def _rms_scale(x):
    return lax.rsqrt(jnp.mean(x * x, axis=-1, keepdims=True) + EPS)


def _layernorm_bf16(z, g_ref, b_ref):
    mu = jnp.mean(z, axis=-1, keepdims=True)
    zc = z - mu
    var = jnp.mean(zc * zc, axis=-1, keepdims=True)
    return (zc * lax.rsqrt(var + EPS) * g_ref[...] + b_ref[...]).astype(BF16)


def _spatial_mix(v, ws_ref):
    n_blocks = v.shape[0] // GMLP_BLOCK
    ci = lax.broadcasted_iota(jnp.int32, (GMLP_BLOCK, GMLP_BLOCK), 0) // CHUNK
    cj = lax.broadcasted_iota(jnp.int32, (GMLP_BLOCK, GMLP_BLOCK), 1) // CHUNK
    mask = ci >= cj
    heads = []
    for h in range(A_HEADS):
        w = jnp.where(mask, ws_ref[h], 0.0).astype(BF16)
        cols = slice(h * A_HEAD_DIM, (h + 1) * A_HEAD_DIM)
        rhs = jnp.concatenate(
            [v[n * GMLP_BLOCK:(n + 1) * GMLP_BLOCK, cols] for n in range(n_blocks)], axis=1)
        heads.append(jnp.dot(w, rhs, preferred_element_type=F32))
    return jnp.concatenate(
        [jnp.concatenate([hd[:, n * A_HEAD_DIM:(n + 1) * A_HEAD_DIM] for hd in heads], axis=1)
         for n in range(n_blocks)], axis=0)


def _mixer_kernel(x_ref, g_ref, w_ref, lng_ref, lnb_ref, ws_ref, bias_ref, oga_ref, wo_ref,
                  wb_ref, wc_ref, wglu_ref, are_ref, aim_ref, d_ref, ogb_ref,
                  mixa_ref, mixb_ref, wob_ref, st_ref, bu_ref):
    t_rows = x_ref.shape[0]
    tile = lambda k: slice(k * LANES, (k + 1) * LANES)
    half_lane_tiles = S5_LANE_TILES // S5_HALVES
    half_state_tiles = S5_STATE_TILES // S5_HALVES
    half_vregs = S5_STATE_VREGS // S5_HALVES
    im_base = half_state_tiles * S5_PITCH

    def scratch_rows(local_tile):
        return slice(local_tile * S5_PITCH, local_tile * S5_PITCH + t_rows)

    def half_state_tiles_of(kl):
        re = [kl * S5_TILES_PER_LANE_TILE + c for c in range(S5_TILES_PER_LANE_TILE)]
        return re + [half_state_tiles + q for q in re]

    @pl.when(pl.program_id(0) == 0)
    def _():
        st_ref[...] = jnp.zeros_like(st_ref)

    wob_ref[...] = wo_ref[...].astype(BF16)

    x = x_ref[...]
    xn = (x * _rms_scale(x) * g_ref[...]).astype(BF16)
    u = jnp.dot(xn, w_ref[:, 2 * A_WIDTH:], preferred_element_type=F32)

    h_pre, yc, ys, gates = [], [], [], []
    for hf in range(S5_HALVES):
        for kl in range(half_lane_tiles):
            k = hf * half_lane_tiles + kl
            res = jnp.dot(u[:, tile(k)].astype(BF16), wb_ref[k],
                          preferred_element_type=F32)
            for c, q in enumerate(half_state_tiles_of(kl)):
                bu_ref[scratch_rows(q), :] = res[:, tile(c)]

        gates += [jnp.dot(ys[k].astype(BF16), wglu_ref[k], preferred_element_type=F32)
                  for k in range(len(gates), len(ys))]

        h_pre.append(jnp.dot(xn, w_ref[:, hf * A_WIDTH:(hf + 1) * A_WIDTH],
                             preferred_element_type=F32))

        vregs = [hf * half_vregs + m for m in range(half_vregs)]
        a_re = [are_ref[gm] for gm in vregs]
        a_im = [aim_ref[gm] for gm in vregs]
        s_re = [st_ref[gm] for gm in vregs]
        s_im = [st_ref[S5_STATE_VREGS + gm] for gm in vregs]
        for t in range(t_rows):
            for m in range(half_vregs):
                re_rows = pl.ds(t + SUBLANES * m * S5_PITCH, SUBLANES, stride=S5_PITCH)
                im_rows = pl.ds(t + im_base + SUBLANES * m * S5_PITCH, SUBLANES, stride=S5_PITCH)
                nr = a_re[m] * s_re[m] - a_im[m] * s_im[m] + bu_ref[re_rows, :]
                ni = a_re[m] * s_im[m] + a_im[m] * s_re[m] + bu_ref[im_rows, :]
                bu_ref[re_rows, :] = nr
                bu_ref[im_rows, :] = ni
                s_re[m], s_im[m] = nr, ni
        for m, gm in enumerate(vregs):
            st_ref[gm] = s_re[m]
            st_ref[S5_STATE_VREGS + gm] = s_im[m]

        for kl in range(half_lane_tiles):
            k = hf * half_lane_tiles + kl
            lhs = jnp.concatenate([bu_ref[scratch_rows(q), :] for q in half_state_tiles_of(kl)],
                                  axis=1).astype(BF16)
            yc.append(jnp.dot(lhs, wc_ref[k], preferred_element_type=F32))

        ys += [_gelu(yc[k] + d_ref[:, tile(k)] * u[:, tile(k)]) for k in range(len(ys), len(yc))]
        if hf == 0:
            zu = _gelu(h_pre[0])

    gates += [jnp.dot(ys[k].astype(BF16), wglu_ref[k], preferred_element_type=F32)
              for k in range(len(gates), len(ys))]
    yb = jnp.concatenate([ys[k] * jax.nn.sigmoid(gates[k]) for k in range(S5_LANE_TILES)], axis=1)
    mixb_ref[...] = (yb * _rms_scale(yb) * ogb_ref[...]).astype(BF16)

    v = _layernorm_bf16(_gelu(h_pre[1]), lng_ref, lnb_ref)
    bias = jnp.concatenate([bias_ref[...]] * (t_rows // GMLP_BLOCK), axis=0)
    ya = zu * (_spatial_mix(v, ws_ref) + bias)
    mixa_ref[...] = (ya * _rms_scale(ya) * oga_ref[...]).astype(BF16)


def _mixer(x, g, w_bf16, ln_g, ln_b, w_s, bias_full, out_a_g, w_out,
           wb, wc, wglu, a_re, a_im, d, out_b_g):
    tm = S5_T
    n_steps = SEQ // tm
    wo_rows = w_out.shape[0] // n_steps
    n_state_cols = 2 * S5_GROUPS_PER_TILE * S5_STATE
    const2 = lambda i: (0, 0)
    const3 = lambda i: (0, 0, 0)
    once = dict(pipeline_mode=pl.Buffered(1))
    return pl.pallas_call(
        _mixer_kernel,
        grid=(n_steps,),
        in_specs=[
            pl.BlockSpec((tm, D_MODEL), lambda i: (i, 0)),
            pl.BlockSpec((1, D_MODEL), const2),
            pl.BlockSpec((D_MODEL, IN_WIDTH), const2, **once),
            pl.BlockSpec((1, A_WIDTH), const2),
            pl.BlockSpec((1, A_WIDTH), const2),
            pl.BlockSpec((A_HEADS, GMLP_BLOCK, GMLP_BLOCK), const3),
            pl.BlockSpec((GMLP_BLOCK, A_WIDTH), const2),
            pl.BlockSpec((1, A_WIDTH), const2),
            pl.BlockSpec((wo_rows, D_MODEL), lambda i: (i, 0)),
            pl.BlockSpec((S5_LANE_TILES, LANES, n_state_cols), const3, **once),
            pl.BlockSpec((S5_LANE_TILES, n_state_cols, LANES), const3, **once),
            pl.BlockSpec((S5_LANE_TILES, LANES, LANES), const3),
            pl.BlockSpec((S5_STATE_VREGS, SUBLANES, LANES), const3),
            pl.BlockSpec((S5_STATE_VREGS, SUBLANES, LANES), const3),
            pl.BlockSpec((1, B_WIDTH), const2),
            pl.BlockSpec((1, B_WIDTH), const2),
        ],
        out_specs=[
            pl.BlockSpec((tm, A_WIDTH), lambda i: (i, 0)),
            pl.BlockSpec((tm, B_WIDTH), lambda i: (i, 0)),
            pl.BlockSpec((wo_rows, D_MODEL), lambda i: (i, 0)),
        ],
        out_shape=[jax.ShapeDtypeStruct((SEQ, A_WIDTH), BF16),
                   jax.ShapeDtypeStruct((SEQ, B_WIDTH), BF16),
                   jax.ShapeDtypeStruct(w_out.shape, BF16)],
        scratch_shapes=[
            pltpu.VMEM((2 * S5_STATE_VREGS, SUBLANES, LANES), F32),
            pltpu.VMEM((2 * S5_STATE_TILES // S5_HALVES * S5_PITCH, LANES), F32),
        ],
        compiler_params=pltpu.CompilerParams(
            dimension_semantics=("arbitrary",), vmem_limit_bytes=VMEM_LIMIT),
        name="mixer",
    )(x, g, w_bf16, ln_g, ln_b, w_s, bias_full, out_a_g, w_out,
      wb, wc, wglu, a_re, a_im, d, out_b_g)


def _s5_discretise_kernel(lre_ref, lim_ref, ldt_ref, bre_ref, bim_ref,
                          abr_ref, abi_ref, bbr_ref, bbi_ref):
    lr, li = lre_ref[...], lim_ref[...]
    dt = jnp.exp(ldt_ref[...])
    mag = jnp.exp(lr * dt)
    ab_r = mag * jnp.cos(li * dt)
    ab_i = mag * jnp.sin(li * dt)
    den = lr * lr + li * li
    nr = ab_r - 1.0
    co_r = (nr * lr + ab_i * li) / den
    co_i = (ab_i * lr - nr * li) / den
    abr_ref[...] = ab_r
    abi_ref[...] = ab_i
    br, bi = bre_ref[...], bim_ref[...]
    bbr_ref[...] = co_r[:, None, :] * br - co_i[:, None, :] * bi
    bbi_ref[...] = co_r[:, None, :] * bi + co_i[:, None, :] * br


def _s5_params(lam_re, lam_im, log_dt, b_re, b_im, c_re, c_im, w_glu):
    g_p = jax.ShapeDtypeStruct((S5_GROUPS, S5_STATE), F32)
    g_h_p = jax.ShapeDtypeStruct((S5_GROUPS, S5_GROUP_CH, S5_STATE), F32)
    ab_r, ab_i, bbt_r, bbt_i = pl.pallas_call(
        _s5_discretise_kernel,
        out_shape=[g_p, g_p, g_h_p, g_h_p],
        name="s5_discretise",
    )(lam_re, lam_im, log_dt[:, None], jnp.swapaxes(b_re, 1, 2), jnp.swapaxes(b_im, 1, 2))

    gt = S5_GROUPS_PER_TILE

    def blockdiag(parts, axis):
        _, r, c = parts[0].shape
        rows = [p.reshape(S5_LANE_TILES, gt * r, c) for p in parts]
        tiled = [jnp.concatenate([p] * gt, axis=2) for p in rows]
        full = jnp.concatenate(tiled, axis=axis)
        row_g = lax.broadcasted_iota(jnp.int32, full.shape, 1) % (gt * r) // r
        col_g = lax.broadcasted_iota(jnp.int32, full.shape, 2) % (gt * c) // c
        return jnp.where(row_g == col_g, full, 0.0).astype(BF16)

    wb = blockdiag([bbt_r, bbt_i], axis=2)
    wc = blockdiag([jnp.swapaxes(c_re, 1, 2), -jnp.swapaxes(c_im, 1, 2)], axis=1)
    wglu = blockdiag([w_glu], axis=2)
    a_re = ab_r.reshape(S5_STATE_VREGS, SUBLANES, LANES)
    a_im = ab_i.reshape(S5_STATE_VREGS, SUBLANES, LANES)
    return wb, wc, wglu, a_re, a_im


def _outproj_kernel(x_ref, ma_ref, mb_ref, wa_ref, wb_ref, g_ref, x1_ref, hn_ref):
    half = x_ref.shape[0] // 2
    for r in range(2):
        rows = slice(r * half, (r + 1) * half)
        acc = jnp.dot(ma_ref[rows, :], wa_ref[...], preferred_element_type=F32)
        acc = acc + jnp.dot(mb_ref[rows, :], wb_ref[...], preferred_element_type=F32)
        x1 = x_ref[rows, :] + acc
        x1_ref[rows, :] = x1
        hn_ref[rows, :] = (x1 * _rms_scale(x1) * g_ref[...]).astype(BF16)


def _outproj(x, mix_a, mix_b, w_out_bf16, g, tm=512):
    return pl.pallas_call(
        _outproj_kernel,
        grid=(SEQ // tm,),
        in_specs=[
            pl.BlockSpec((tm, D_MODEL), lambda i: (i, 0)),
            pl.BlockSpec((tm, A_WIDTH), lambda i: (i, 0)),
            pl.BlockSpec((tm, B_WIDTH), lambda i: (i, 0)),
            pl.BlockSpec((A_WIDTH, D_MODEL), lambda i: (0, 0)),
            pl.BlockSpec((B_WIDTH, D_MODEL), lambda i: (1, 0)),
            pl.BlockSpec((1, D_MODEL), lambda i: (0, 0)),
        ],
        out_specs=[
            pl.BlockSpec((tm, D_MODEL), lambda i: (i, 0)),
            pl.BlockSpec((tm, D_MODEL), lambda i: (i, 0)),
        ],
        out_shape=[jax.ShapeDtypeStruct((SEQ, D_MODEL), F32),
                   jax.ShapeDtypeStruct((SEQ, D_MODEL), BF16)],
        compiler_params=pltpu.CompilerParams(
            dimension_semantics=("arbitrary",), vmem_limit_bytes=VMEM_LIMIT),
        name="outproj",
    )(x, mix_a, mix_b, w_out_bf16, w_out_bf16, g)


def _ffn_up_kernel(hn_ref, wg_ref, wu_ref, wd_ref, act_ref, wdb_ref):
    wdb_ref[...] = wd_ref[...].astype(BF16)
    hn = hn_ref[...]
    for c in range(act_ref.shape[1] // MXU_COLS):
        cols = slice(c * MXU_COLS, (c + 1) * MXU_COLS)
        gate = jnp.dot(hn, wg_ref[:, cols].astype(BF16), preferred_element_type=F32)
        up = jnp.dot(hn, wu_ref[:, cols].astype(BF16), preferred_element_type=F32)
        act_ref[:, cols] = (gate * jax.nn.sigmoid(gate) * up).astype(BF16)


def _ffn_up(hn, wg, wu, wd, tm=2048, tf=512):
    n_i = SEQ // tm
    wd_rows = wd.shape[0] // (D_FF // tf * n_i)
    wd_slab = pl.BlockSpec((wd_rows, D_MODEL), lambda f, i: (f * n_i + i, 0))
    return pl.pallas_call(
        _ffn_up_kernel,
        grid=(D_FF // tf, n_i),
        in_specs=[
            pl.BlockSpec((tm, D_MODEL), lambda f, i: (i, 0)),
            pl.BlockSpec((D_MODEL, tf), lambda f, i: (0, f)),
            pl.BlockSpec((D_MODEL, tf), lambda f, i: (0, f)),
            wd_slab,
        ],
        out_specs=[pl.BlockSpec((tm, tf), lambda f, i: (i, f)), wd_slab],
        out_shape=[jax.ShapeDtypeStruct((SEQ, D_FF), BF16),
                   jax.ShapeDtypeStruct(wd.shape, BF16)],
        compiler_params=pltpu.CompilerParams(
            dimension_semantics=("arbitrary", "arbitrary"),
            vmem_limit_bytes=VMEM_LIMIT),
        name="ffn_up",
    )(hn, wg, wu, wd)


def _ffn_down_kernel(act_ref, wd_ref, x1_ref, g_ref, o_ref):
    n_tiles = pl.num_programs(1)
    tn = wd_ref.shape[1]
    for n in range(o_ref.shape[1] // tn):
        @pl.when(pl.program_id(1) == n)
        def _(n=n):
            o_ref[:, n * tn:(n + 1) * tn] = x1_ref[...] + jnp.dot(
                act_ref[...], wd_ref[...], preferred_element_type=F32)

    @pl.when(pl.program_id(1) == n_tiles - 1)
    def _():
        x2 = o_ref[...]
        o_ref[...] = x2 * _rms_scale(x2) * g_ref[...]


def _ffn_down(act, wd_bf16, x1, g, tm=1024, tn=512):
    return pl.pallas_call(
        _ffn_down_kernel,
        grid=(SEQ // tm, D_MODEL // tn),
        in_specs=[
            pl.BlockSpec((tm, D_FF), lambda i, n: (i, 0)),
            pl.BlockSpec((D_FF, tn), lambda i, n: (0, n)),
            pl.BlockSpec((tm, tn), lambda i, n: (i, n)),
            pl.BlockSpec((1, D_MODEL), lambda i, n: (0, 0)),
        ],
        out_specs=pl.BlockSpec((tm, D_MODEL), lambda i, n: (i, 0)),
        out_shape=jax.ShapeDtypeStruct((SEQ, D_MODEL), F32),
        compiler_params=pltpu.CompilerParams(
            dimension_semantics=("arbitrary", "arbitrary"),
            vmem_limit_bytes=FFN_DOWN_VMEM_LIMIT),
        name="ffn_down",
    )(act, wd_bf16, x1, g)


def kernel(x, norm_mix_g, w_in, a_ln_g, a_ln_b, a_w_s, a_b_s, s5_lambda_re, s5_lambda_im,
           s5_log_dt, s5_b_re, s5_b_im, s5_c_re, s5_c_im, s5_d, s5_w_glu, out_norm_a_g,
           out_norm_b_g, w_out, norm_ffn_g, w_gate, w_up, w_down, final_norm_g):
    assert x.shape == (1, SEQ, D_MODEL) and norm_mix_g.shape[0] == 1
    xs = x.reshape(SEQ, D_MODEL)
    l = 0

    bias_full = jnp.repeat(jnp.transpose(a_b_s[l]), A_HEAD_DIM, axis=1)
    wb, wc, wglu, a_re, a_im = _s5_params(
        s5_lambda_re[l], s5_lambda_im[l], s5_log_dt[l], s5_b_re[l], s5_b_im[l],
        s5_c_re[l], s5_c_im[l], s5_w_glu[l])
    mix_a, mix_b, w_out_bf16 = _mixer(
        xs, norm_mix_g[l][None], w_in[l].astype(BF16), a_ln_g[l][None], a_ln_b[l][None],
        a_w_s[l], bias_full, out_norm_a_g[l][None], w_out[l],
        wb, wc, wglu, a_re, a_im, s5_d[l][None], out_norm_b_g[l][None])

    x1, hn = _outproj(xs, mix_a, mix_b, w_out_bf16, norm_ffn_g[l][None])

    act, w_down_bf16 = _ffn_up(hn, w_gate[l], w_up[l], w_down[l])
    out = _ffn_down(act, w_down_bf16, x1, final_norm_g[None])
    return out.reshape(1, SEQ, D_MODEL)
```

```python
import math

import jax
import jax.numpy as jnp
from jax import lax
from jax.experimental import pallas as pl
from jax.experimental.pallas import tpu as pltpu

F32 = jnp.float32
BF16 = jnp.bfloat16

D_MODEL = 2048
SEQ = 8192
CHUNK = 64
A_WIDTH = 1024
A_HEADS = 8
A_HEAD_DIM = 128
GMLP_BLOCK = 128
B_WIDTH = 1024
S5_GROUP_CH = 16
S5_GROUPS = 64
S5_STATE = 64
IN_WIDTH = 3072
D_FF = 5632
EPS = 1e-6

LANES = 128
SUBLANES = 8
MXU_COLS = 256
VMEM_LIMIT = 56 * 1024 * 1024
FFN_DOWN_VMEM_LIMIT = 60 * 1024 * 1024

S5_LANE_TILES = B_WIDTH // LANES
S5_GROUPS_PER_TILE = LANES // S5_GROUP_CH
S5_STATE_TILES = S5_GROUPS * S5_STATE // LANES
S5_TILES_PER_LANE_TILE = S5_STATE_TILES // S5_LANE_TILES
S5_STATE_VREGS = S5_STATE_TILES // SUBLANES
S5_T = 512
S5_HALVES = 2
S5_PITCH = S5_T + 4


def _gelu(x):
    return 0.5 * x * (1.0 + lax.erf(x * (1.0 / math.sqrt(2.0))))


def _rms_scale(x):
    return lax.rsqrt(jnp.mean(x * x, axis=-1, keepdims=True) + EPS)


def _layernorm_bf16(z, g_ref, b_ref):
    mu = jnp.mean(z, axis=-1, keepdims=True)
    zc = z - mu
    var = jnp.mean(zc * zc, axis=-1, keepdims=True)
    return (zc * lax.rsqrt(var + EPS) * g_ref[...] + b_ref[...]).astype(BF16)


def _spatial_mix(v, ws_ref):
    n_blocks = v.shape[0] // GMLP_BLOCK
    ci = lax.broadcasted_iota(jnp.int32, (GMLP_BLOCK, GMLP_BLOCK), 0) // CHUNK
    cj = lax.broadcasted_iota(jnp.int32, (GMLP_BLOCK, GMLP_BLOCK), 1) // CHUNK
    mask = ci >= cj
    heads = []
    for h in range(A_HEADS):
        w = jnp.where(mask, ws_ref[h], 0.0).astype(BF16)
        cols = slice(h * A_HEAD_DIM, (h + 1) * A_HEAD_DIM)
        rhs = jnp.concatenate(
            [v[n * GMLP_BLOCK:(n + 1) * GMLP_BLOCK, cols] for n in range(n_blocks)], axis=1)
        heads.append(jnp.dot(w, rhs, preferred_element_type=F32))
    return jnp.concatenate(
        [jnp.concatenate([hd[:, n * A_HEAD_DIM:(n + 1) * A_HEAD_DIM] for hd in heads], axis=1)
         for n in range(n_blocks)], axis=0)


def _mixer_kernel(x_ref, g_ref, w_ref, lng_ref, lnb_ref, ws_ref, bias_ref, oga_ref, wo_ref,
                  wb_ref, wc_ref, wglu_ref, are_ref, aim_ref, d_ref, ogb_ref,
                  mixa_ref, mixb_ref, wob_ref, st_ref, bu_ref):
    t_rows = x_ref.shape[0]
    tile = lambda k: slice(k * LANES, (k + 1) * LANES)
    half_lane_tiles = S5_LANE_TILES // S5_HALVES
    half_state_tiles = S5_STATE_TILES // S5_HALVES
    half_vregs = S5_STATE_VREGS // S5_HALVES
    im_base = half_state_tiles * S5_PITCH

    def scratch_rows(local_tile):
        return slice(local_tile * S5_PITCH, local_tile * S5_PITCH + t_rows)

    def half_state_tiles_of(kl):
        re = [kl * S5_TILES_PER_LANE_TILE + c for c in range(S5_TILES_PER_LANE_TILE)]
        return re + [half_state_tiles + q for q in re]

    @pl.when(pl.program_id(0) == 0)
    def _():
        st_ref[...] = jnp.zeros_like(st_ref)

    wob_ref[...] = wo_ref[...].astype(BF16)

    x = x_ref[...]
    xn = (x * _rms_scale(x) * g_ref[...]).astype(BF16)
    u = jnp.dot(xn, w_ref[:, 2 * A_WIDTH:], preferred_element_type=F32)

    h_pre, yc, ys, gates = [], [], [], []
    for hf in range(S5_HALVES):
        for kl in range(half_lane_tiles):
            k = hf * half_lane_tiles + kl
            res = jnp.dot(u[:, tile(k)].astype(BF16), wb_ref[k],
                          preferred_element_type=F32)
            for c, q in enumerate(half_state_tiles_of(kl)):
                bu_ref[scratch_rows(q), :] = res[:, tile(c)]

        gates += [jnp.dot(ys[k].astype(BF16), wglu_ref[k], preferred_element_type=F32)
                  for k in range(len(gates), len(ys))]

        h_pre.append(jnp.dot(xn, w_ref[:, hf * A_WIDTH:(hf + 1) * A_WIDTH],
                             preferred_element_type=F32))

        vregs = [hf * half_vregs + m for m in range(half_vregs)]
        a_re = [are_ref[gm] for gm in vregs]
        a_im = [aim_ref[gm] for gm in vregs]
        s_re = [st_ref[gm] for gm in vregs]
        s_im = [st_ref[S5_STATE_VREGS + gm] for gm in vregs]
        for t in range(t_rows):
            for m in range(half_vregs):
                re_rows = pl.ds(t + SUBLANES * m * S5_PITCH, SUBLANES, stride=S5_PITCH)
                im_rows = pl.ds(t + im_base + SUBLANES * m * S5_PITCH, SUBLANES, stride=S5_PITCH)
                nr = a_re[m] * s_re[m] - a_im[m] * s_im[m] + bu_ref[re_rows, :]
                ni = a_re[m] * s_im[m] + a_im[m] * s_re[m] + bu_ref[im_rows, :]
                bu_ref[re_rows, :] = nr
                bu_ref[im_rows, :] = ni
                s_re[m], s_im[m] = nr, ni
        for m, gm in enumerate(vregs):
            st_ref[gm] = s_re[m]
            st_ref[S5_STATE_VREGS + gm] = s_im[m]

        for kl in range(half_lane_tiles):
            k = hf * half_lane_tiles + kl
            lhs = jnp.concatenate([bu_ref[scratch_rows(q), :] for q in half_state_tiles_of(kl)],
                                  axis=1).astype(BF16)
            yc.append(jnp.dot(lhs, wc_ref[k], preferred_element_type=F32))

        ys += [_gelu(yc[k] + d_ref[:, tile(k)] * u[:, tile(k)]) for k in range(len(ys), len(yc))]
        if hf == 0:
            zu = _gelu(h_pre[0])

    gates += [jnp.dot(ys[k].astype(BF16), wglu_ref[k], preferred_element_type=F32)
              for k in range(len(gates), len(ys))]
    yb = jnp.concatenate([ys[k] * jax.nn.sigmoid(gates[k]) for k in range(S5_LANE_TILES)], axis=1)
    mixb_ref[...] = (yb * _rms_scale(yb) * ogb_ref[...]).astype(BF16)

    v = _layernorm_bf16(_gelu(h_pre[1]), lng_ref, lnb_ref)
    bias = jnp.concatenate([bias_ref[...]] * (t_rows // GMLP_BLOCK), axis=0)
    ya = zu * (_spatial_mix(v, ws_ref) + bias)
    mixa_ref[...] = (ya * _rms_scale(ya) * oga_ref[...]).astype(BF16)


def _mixer(x, g, w_bf16, ln_g, ln_b, w_s, bias_full, out_a_g, w_out,
           wb, wc, wglu, a_re, a_im, d, out_b_g):
    tm = S5_T
    n_steps = SEQ // tm
    wo_rows = w_out.shape[0] // n_steps
    n_state_cols = 2 * S5_GROUPS_PER_TILE * S5_STATE
    const2 = lambda i: (0, 0)
    const3 = lambda i: (0, 0, 0)
    once = dict(pipeline_mode=pl.Buffered(1))
    return pl.pallas_call(
        _mixer_kernel,
        grid=(n_steps,),
        in_specs=[
            pl.BlockSpec((tm, D_MODEL), lambda i: (i, 0)),
            pl.BlockSpec((1, D_MODEL), const2),
            pl.BlockSpec((D_MODEL, IN_WIDTH), const2, **once),
            pl.BlockSpec((1, A_WIDTH), const2),
            pl.BlockSpec((1, A_WIDTH), const2),
            pl.BlockSpec((A_HEADS, GMLP_BLOCK, GMLP_BLOCK), const3),
            pl.BlockSpec((GMLP_BLOCK, A_WIDTH), const2),
            pl.BlockSpec((1, A_WIDTH), const2),
            pl.BlockSpec((wo_rows, D_MODEL), lambda i: (i, 0)),
            pl.BlockSpec((S5_LANE_TILES, LANES, n_state_cols), const3, **once),
            pl.BlockSpec((S5_LANE_TILES, n_state_cols, LANES), const3, **once),
            pl.BlockSpec((S5_LANE_TILES, LANES, LANES), const3),
            pl.BlockSpec((S5_STATE_VREGS, SUBLANES, LANES), const3),
            pl.BlockSpec((S5_STATE_VREGS, SUBLANES, LANES), const3),
            pl.BlockSpec((1, B_WIDTH), const2),
            pl.BlockSpec((1, B_WIDTH), const2),
        ],
        out_specs=[
            pl.BlockSpec((tm, A_WIDTH), lambda i: (i, 0)),
            pl.BlockSpec((tm, B_WIDTH), lambda i: (i, 0)),
            pl.BlockSpec((wo_rows, D_MODEL), lambda i: (i, 0)),
        ],
        out_shape=[jax.ShapeDtypeStruct((SEQ, A_WIDTH), BF16),
                   jax.ShapeDtypeStruct((SEQ, B_WIDTH), BF16),
                   jax.ShapeDtypeStruct(w_out.shape, BF16)],
        scratch_shapes=[
            pltpu.VMEM((2 * S5_STATE_VREGS, SUBLANES, LANES), F32),
            pltpu.VMEM((2 * S5_STATE_TILES // S5_HALVES * S5_PITCH, LANES), F32),
        ],
        compiler_params=pltpu.CompilerParams(
            dimension_semantics=("arbitrary",), vmem_limit_bytes=VMEM_LIMIT),
        name="mixer",
    )(x, g, w_bf16, ln_g, ln_b, w_s, bias_full, out_a_g, w_out,
      wb, wc, wglu, a_re, a_im, d, out_b_g)


def _repeat_cols_onehot(n_rows, n_cols):
    r = lax.broadcasted_iota(jnp.int32, (n_rows, n_cols), 0)
    c = lax.broadcasted_iota(jnp.int32, (n_rows, n_cols), 1)
    return jnp.where(r == c % n_rows, 1.0, 0.0).astype(BF16)


def _block_diag_mask(n_rows, n_cols, row_block, col_block):
    r = lax.broadcasted_iota(jnp.int32, (n_rows, n_cols), 0)
    c = lax.broadcasted_iota(jnp.int32, (n_rows, n_cols), 1)
    return r // row_block == c // col_block


def _s5_params_kernel(lre_ref, lim_ref, ldt_ref, bre_ref, bim_ref, cre_ref, cim_ref, wg_ref,
                      abr_ref, abi_ref, wb_ref, wc_ref, wglu_ref):
    lr, li = lre_ref[...], lim_ref[...]
    dt = jnp.exp(ldt_ref[...])
    mag = jnp.exp(lr * dt)
    ab_r = mag * jnp.cos(li * dt)
    ab_i = mag * jnp.sin(li * dt)
    den = lr * lr + li * li
    nr = ab_r - 1.0
    co_r = (nr * lr + ab_i * li) / den
    co_i = (ab_i * lr - nr * li) / den
    abr_ref[...] = ab_r
    abi_ref[...] = ab_i

    gt, h, p = S5_GROUPS_PER_TILE, S5_GROUP_CH, S5_STATE
    spread_p = _repeat_cols_onehot(p, gt * p)
    spread_h = _repeat_cols_onehot(h, gt * h)
    eye_gh = _repeat_cols_onehot(gt * h, gt * h)
    mask_b = _block_diag_mask(gt * h, gt * p, h, p)
    mask_c = _block_diag_mask(gt * p, gt * h, p, h)
    mask_g = _block_diag_mask(gt * h, gt * h, h, h)
    for k in range(S5_LANE_TILES):
        grp = slice(k * gt, (k + 1) * gt)
        cr = co_r[grp][:, None, :]
        ci = co_i[grp][:, None, :]
        br, bi = bre_ref[grp], bim_ref[grp]
        bb_r = (cr * br - ci * bi).reshape(gt * h, p).astype(BF16)
        bb_i = (cr * bi + ci * br).reshape(gt * h, p).astype(BF16)
        for half, bb in enumerate((bb_r, bb_i)):
            full = jnp.dot(bb, spread_p, preferred_element_type=F32)
            wb_ref[k, :, half * gt * p:(half + 1) * gt * p] = (
                jnp.where(mask_b, full, 0.0).astype(BF16))
        for half, (c_ref, sign) in enumerate(((cre_ref, 1.0), (cim_ref, -1.0))):
            c_k = (sign * c_ref[grp]).reshape(gt * h, p).astype(BF16)
            c_t = lax.dot_general(c_k, eye_gh, (((0,), (0,)), ((), ())),
                                  preferred_element_type=F32)
            full = jnp.concatenate([c_t] * gt, axis=0)
            wc_ref[k, half * gt * p:(half + 1) * gt * p, :] = (
                jnp.where(mask_c, full, 0.0).astype(BF16))
        g_k = wg_ref[grp].reshape(gt * h, h).astype(BF16)
        full = jnp.dot(g_k, spread_h, preferred_element_type=F32)
        wglu_ref[k] = jnp.where(mask_g, full, 0.0).astype(BF16)


def _s5_params(lam_re, lam_im, log_dt, b_re, b_im, c_re, c_im, w_glu):
    gt, h, p = S5_GROUPS_PER_TILE, S5_GROUP_CH, S5_STATE
    g_p = jax.ShapeDtypeStruct((S5_GROUPS, p), F32)
    ab_r, ab_i, wb, wc, wglu = pl.pallas_call(
        _s5_params_kernel,
        out_shape=[g_p, g_p,
                   jax.ShapeDtypeStruct((S5_LANE_TILES, gt * h, 2 * gt * p), BF16),
                   jax.ShapeDtypeStruct((S5_LANE_TILES, 2 * gt * p, gt * h), BF16),
                   jax.ShapeDtypeStruct((S5_LANE_TILES, gt * h, gt * h), BF16)],
        name="s5_params",
    )(lam_re, lam_im, log_dt[:, None], jnp.swapaxes(b_re, 1, 2), jnp.swapaxes(b_im, 1, 2),
      c_re, c_im, w_glu)
    a_re = ab_r.reshape(S5_STATE_VREGS, SUBLANES, LANES)
    a_im = ab_i.reshape(S5_STATE_VREGS, SUBLANES, LANES)
    return wb, wc, wglu, a_re, a_im


def _outproj_kernel(x_ref, ma_ref, mb_ref, wa_ref, wb_ref, g_ref, x1_ref, hn_ref):
    half = x_ref.shape[0] // 2
    for r in range(2):
        rows = slice(r * half, (r + 1) * half)
        acc = jnp.dot(ma_ref[rows, :], wa_ref[...], preferred_element_type=F32)
        acc = acc + jnp.dot(mb_ref[rows, :], wb_ref[...], preferred_element_type=F32)
        x1 = x_ref[rows, :] + acc
        x1_ref[rows, :] = x1
        hn_ref[rows, :] = (x1 * _rms_scale(x1) * g_ref[...]).astype(BF16)


def _outproj(x, mix_a, mix_b, w_out_bf16, g, tm=512):
    return pl.pallas_call(
        _outproj_kernel,
        grid=(SEQ // tm,),
        in_specs=[
            pl.BlockSpec((tm, D_MODEL), lambda i: (i, 0)),
            pl.BlockSpec((tm, A_WIDTH), lambda i: (i, 0)),
            pl.BlockSpec((tm, B_WIDTH), lambda i: (i, 0)),
            pl.BlockSpec((A_WIDTH, D_MODEL), lambda i: (0, 0)),
            pl.BlockSpec((B_WIDTH, D_MODEL), lambda i: (1, 0)),
            pl.BlockSpec((1, D_MODEL), lambda i: (0, 0)),
        ],
        out_specs=[
            pl.BlockSpec((tm, D_MODEL), lambda i: (i, 0)),
            pl.BlockSpec((tm, D_MODEL), lambda i: (i, 0)),
        ],
        out_shape=[jax.ShapeDtypeStruct((SEQ, D_MODEL), F32),
                   jax.ShapeDtypeStruct((SEQ, D_MODEL), BF16)],
        compiler_params=pltpu.CompilerParams(
            dimension_semantics=("arbitrary",), vmem_limit_bytes=VMEM_LIMIT),
        name="outproj",
    )(x, mix_a, mix_b, w_out_bf16, w_out_bf16, g)


def _ffn_up_kernel(hn_ref, wg_ref, wu_ref, wd_ref, act_ref, wdb_ref):
    wdb_ref[...] = wd_ref[...].astype(BF16)
    hn = hn_ref[...]
    for c in range(act_ref.shape[1] // MXU_COLS):
        cols = slice(c * MXU_COLS, (c + 1) * MXU_COLS)
        gate = jnp.dot(hn, wg_ref[:, cols].astype(BF16), preferred_element_type=F32)
        up = jnp.dot(hn, wu_ref[:, cols].astype(BF16), preferred_element_type=F32)
        act_ref[:, cols] = (gate * jax.nn.sigmoid(gate) * up).astype(BF16)


def _ffn_up(hn, wg, wu, wd, tm=2048, tf=512):
    n_i = SEQ // tm
    wd_rows = wd.shape[0] // (D_FF // tf * n_i)
    wd_slab = pl.BlockSpec((wd_rows, D_MODEL), lambda f, i: (f * n_i + i, 0))
    return pl.pallas_call(
        _ffn_up_kernel,
        grid=(D_FF // tf, n_i),
        in_specs=[
            pl.BlockSpec((tm, D_MODEL), lambda f, i: (i, 0)),
            pl.BlockSpec((D_MODEL, tf), lambda f, i: (0, f)),
            pl.BlockSpec((D_MODEL, tf), lambda f, i: (0, f)),
            wd_slab,
        ],
        out_specs=[pl.BlockSpec((tm, tf), lambda f, i: (i, f)), wd_slab],
        out_shape=[jax.ShapeDtypeStruct((SEQ, D_FF), BF16),
                   jax.ShapeDtypeStruct(wd.shape, BF16)],
        compiler_params=pltpu.CompilerParams(
            dimension_semantics=("arbitrary", "arbitrary"),
            vmem_limit_bytes=VMEM_LIMIT),
        name="ffn_up",
    )(hn, wg, wu, wd)


def _ffn_down_kernel(act_ref, wd_ref, x1_ref, g_ref, o_ref):
    n_tiles = pl.num_programs(1)
    tn = wd_ref.shape[1]
    for n in range(o_ref.shape[1] // tn):
        @pl.when(pl.program_id(1) == n)
        def _(n=n):
            o_ref[:, n * tn:(n + 1) * tn] = x1_ref[...] + jnp.dot(
                act_ref[...], wd_ref[...], preferred_element_type=F32)

    @pl.when(pl.program_id(1) == n_tiles - 1)
    def _():
        x2 = o_ref[...]
        o_ref[...] = x2 * _rms_scale(x2) * g_ref[...]


def _ffn_down(act, wd_bf16, x1, g, tm=1024, tn=512):
    return pl.pallas_call(
        _ffn_down_kernel,
        grid=(SEQ // tm, D_MODEL // tn),
        in_specs=[
            pl.BlockSpec((tm, D_FF), lambda i, n: (i, 0)),
            pl.BlockSpec((D_FF, tn), lambda i, n: (0, n)),
            pl.BlockSpec((tm, tn), lambda i, n: (i, n)),
            pl.BlockSpec((1, D_MODEL), lambda i, n: (0, 0)),
        ],
        out_specs=pl.BlockSpec((tm, D_MODEL), lambda i, n: (i, 0)),
        out_shape=jax.ShapeDtypeStruct((SEQ, D_MODEL), F32),
        compiler_params=pltpu.CompilerParams(
            dimension_semantics=("arbitrary", "arbitrary"),
            vmem_limit_bytes=FFN_DOWN_VMEM_LIMIT),
        name="ffn_down",
    )(act, wd_bf16, x1, g)


def kernel(x, norm_mix_g, w_in, a_ln_g, a_ln_b, a_w_s, a_b_s, s5_lambda_re, s5_lambda_im,
           s5_log_dt, s5_b_re, s5_b_im, s5_c_re, s5_c_im, s5_d, s5_w_glu, out_norm_a_g,
           out_norm_b_g, w_out, norm_ffn_g, w_gate, w_up, w_down, final_norm_g):
    assert x.shape == (1, SEQ, D_MODEL) and norm_mix_g.shape[0] == 1
    xs = x.reshape(SEQ, D_MODEL)
    l = 0

    bias_full = jnp.repeat(jnp.transpose(a_b_s[l]), A_HEAD_DIM, axis=1)
    wb, wc, wglu, a_re, a_im = _s5_params(
        s5_lambda_re[l], s5_lambda_im[l], s5_log_dt[l], s5_b_re[l], s5_b_im[l],
        s5_c_re[l], s5_c_im[l], s5_w_glu[l])
    mix_a, mix_b, w_out_bf16 = _mixer(
        xs, norm_mix_g[l][None], w_in[l].astype(BF16), a_ln_g[l][None], a_ln_b[l][None],
        a_w_s[l], bias_full, out_norm_a_g[l][None], w_out[l],
        wb, wc, wglu, a_re, a_im, s5_d[l][None], out_norm_b_g[l][None])

    x1, hn = _outproj(xs, mix_a, mix_b, w_out_bf16, norm_ffn_g[l][None])

    act, w_down_bf16 = _ffn_up(hn, w_gate[l], w_up[l], w_down[l])
    out = _ffn_down(act, w_down_bf16, x1, final_norm_g[None])
    return out.reshape(1, SEQ, D_MODEL)
```

```python
import math

import jax
import jax.numpy as jnp
from jax import lax
from jax.experimental import pallas as pl
from jax.experimental.pallas import tpu as pltpu

F32 = jnp.float32
BF16 = jnp.bfloat16

D_MODEL = 2048
SEQ = 8192
CHUNK = 64
A_WIDTH = 1024
A_HEADS = 8
A_HEAD_DIM = 128
GMLP_BLOCK = 128
B_WIDTH = 1024
S5_GROUP_CH = 16
S5_GROUPS = 64
S5_STATE = 64
IN_WIDTH = 3072
D_FF = 5632
EPS = 1e-6

LANES = 128
SUBLANES = 8
MXU_COLS = 256
VMEM_LIMIT = 56 * 1024 * 1024
FFN_DOWN_VMEM_LIMIT = 60 * 1024 * 1024

S5_LANE_TILES = B_WIDTH // LANES
S5_GROUPS_PER_TILE = LANES // S5_GROUP_CH
S5_STATE_TILES = S5_GROUPS * S5_STATE // LANES
S5_TILES_PER_LANE_TILE = S5_STATE_TILES // S5_LANE_TILES
S5_STATE_VREGS = S5_STATE_TILES // SUBLANES
S5_T = 512
S5_HALVES = 2
S5_PITCH = S5_T + 4


def _gelu(x):
    return 0.5 * x * (1.0 + lax.erf(x * (1.0 / math.sqrt(2.0))))


def _rms_scale(x):
    return lax.rsqrt(jnp.mean(x * x, axis=-1, keepdims=True) + EPS)


def _layernorm_bf16(z, g_ref, b_ref):
    mu = jnp.mean(z, axis=-1, keepdims=True)
    zc = z - mu
    var = jnp.mean(zc * zc, axis=-1, keepdims=True)
    return (zc * lax.rsqrt(var + EPS) * g_ref[...] + b_ref[...]).astype(BF16)


def _spatial_mix(v, ws_ref):
    n_blocks = v.shape[0] // GMLP_BLOCK
    ci = lax.broadcasted_iota(jnp.int32, (GMLP_BLOCK, GMLP_BLOCK), 0) // CHUNK
    cj = lax.broadcasted_iota(jnp.int32, (GMLP_BLOCK, GMLP_BLOCK), 1) // CHUNK
    mask = ci >= cj
    heads = []
    for h in range(A_HEADS):
        w = jnp.where(mask, ws_ref[h], 0.0).astype(BF16)
        cols = slice(h * A_HEAD_DIM, (h + 1) * A_HEAD_DIM)
        rhs = jnp.concatenate(
            [v[n * GMLP_BLOCK:(n + 1) * GMLP_BLOCK, cols] for n in range(n_blocks)], axis=1)
        heads.append(jnp.dot(w, rhs, preferred_element_type=F32))
    return jnp.concatenate(
        [jnp.concatenate([hd[:, n * A_HEAD_DIM:(n + 1) * A_HEAD_DIM] for hd in heads], axis=1)
         for n in range(n_blocks)], axis=0)


def _mixer_kernel(x_ref, g_ref, w_ref, lng_ref, lnb_ref, ws_ref, bias_ref, oga_ref, wo_ref,
                  wb_ref, wc_ref, wglu_ref, are_ref, aim_ref, d_ref, ogb_ref,
                  mixa_ref, mixb_ref, wob_ref, st_ref, bu_ref):
    t_rows = x_ref.shape[0]
    tile = lambda k: slice(k * LANES, (k + 1) * LANES)
    half_lane_tiles = S5_LANE_TILES // S5_HALVES
    half_state_tiles = S5_STATE_TILES // S5_HALVES
    half_vregs = S5_STATE_VREGS // S5_HALVES
    im_base = half_state_tiles * S5_PITCH

    def scratch_rows(local_tile):
        return slice(local_tile * S5_PITCH, local_tile * S5_PITCH + t_rows)

    def half_state_tiles_of(kl):
        re = [kl * S5_TILES_PER_LANE_TILE + c for c in range(S5_TILES_PER_LANE_TILE)]
        return re + [half_state_tiles + q for q in re]

    @pl.when(pl.program_id(0) == 0)
    def _():
        st_ref[...] = jnp.zeros_like(st_ref)

    wob_ref[...] = wo_ref[...].astype(BF16)

    x = x_ref[...]
    xn = (x * _rms_scale(x) * g_ref[...]).astype(BF16)
    u = jnp.dot(xn, w_ref[:, 2 * A_WIDTH:], preferred_element_type=F32)

    h_pre, yc, ys, gates = [], [], [], []
    for hf in range(S5_HALVES):
        for kl in range(half_lane_tiles):
            k = hf * half_lane_tiles + kl
            res = jnp.dot(u[:, tile(k)].astype(BF16), wb_ref[k],
                          preferred_element_type=F32)
            for c, q in enumerate(half_state_tiles_of(kl)):
                bu_ref[scratch_rows(q), :] = res[:, tile(c)]

        gates += [jnp.dot(ys[k].astype(BF16), wglu_ref[k], preferred_element_type=F32)
                  for k in range(len(gates), len(ys))]

        h_pre.append(jnp.dot(xn, w_ref[:, hf * A_WIDTH:(hf + 1) * A_WIDTH],
                             preferred_element_type=F32))

        vregs = [hf * half_vregs + m for m in range(half_vregs)]
        a_re = [are_ref[gm] for gm in vregs]
        a_im = [aim_ref[gm] for gm in vregs]
        s_re = [st_ref[gm] for gm in vregs]
        s_im = [st_ref[S5_STATE_VREGS + gm] for gm in vregs]
        for t in range(t_rows):
            for m in range(half_vregs):
                re_rows = pl.ds(t + SUBLANES * m * S5_PITCH, SUBLANES, stride=S5_PITCH)
                im_rows = pl.ds(t + im_base + SUBLANES * m * S5_PITCH, SUBLANES, stride=S5_PITCH)
                nr = a_re[m] * s_re[m] - a_im[m] * s_im[m] + bu_ref[re_rows, :]
                ni = a_re[m] * s_im[m] + a_im[m] * s_re[m] + bu_ref[im_rows, :]
                bu_ref[re_rows, :] = nr
                bu_ref[im_rows, :] = ni
                s_re[m], s_im[m] = nr, ni
        for m, gm in enumerate(vregs):
            st_ref[gm] = s_re[m]
            st_ref[S5_STATE_VREGS + gm] = s_im[m]

        for kl in range(half_lane_tiles):
            k = hf * half_lane_tiles + kl
            lhs = jnp.concatenate([bu_ref[scratch_rows(q), :] for q in half_state_tiles_of(kl)],
                                  axis=1).astype(BF16)
            yc.append(jnp.dot(lhs, wc_ref[k], preferred_element_type=F32))

        ys += [_gelu(yc[k] + d_ref[:, tile(k)] * u[:, tile(k)]) for k in range(len(ys), len(yc))]
        if hf == 0:
            zu = _gelu(h_pre[0])

    gates += [jnp.dot(ys[k].astype(BF16), wglu_ref[k], preferred_element_type=F32)
              for k in range(len(gates), len(ys))]
    yb = jnp.concatenate([ys[k] * jax.nn.sigmoid(gates[k]) for k in range(S5_LANE_TILES)], axis=1)
    mixb_ref[...] = (yb * _rms_scale(yb) * ogb_ref[...]).astype(BF16)

    v = _layernorm_bf16(_gelu(h_pre[1]), lng_ref, lnb_ref)
    bias = jnp.concatenate([bias_ref[...]] * (t_rows // GMLP_BLOCK), axis=0)
    ya = zu * (_spatial_mix(v, ws_ref) + bias)
    mixa_ref[...] = (ya * _rms_scale(ya) * oga_ref[...]).astype(BF16)


def _mixer(x, g, w_bf16, ln_g, ln_b, w_s, bias_full, out_a_g, w_out,
           wb, wc, wglu, a_re, a_im, d, out_b_g):
    tm = S5_T
    n_steps = SEQ // tm
    wo_rows = w_out.shape[0] // n_steps
    n_state_cols = 2 * S5_GROUPS_PER_TILE * S5_STATE
    const2 = lambda i: (0, 0)
    const3 = lambda i: (0, 0, 0)
    once = dict(pipeline_mode=pl.Buffered(1))
    return pl.pallas_call(
        _mixer_kernel,
        grid=(n_steps,),
        in_specs=[
            pl.BlockSpec((tm, D_MODEL), lambda i: (i, 0)),
            pl.BlockSpec((1, D_MODEL), const2),
            pl.BlockSpec((D_MODEL, IN_WIDTH), const2, **once),
            pl.BlockSpec((1, A_WIDTH), const2),
            pl.BlockSpec((1, A_WIDTH), const2),
            pl.BlockSpec((A_HEADS, GMLP_BLOCK, GMLP_BLOCK), const3),
            pl.BlockSpec((GMLP_BLOCK, A_WIDTH), const2),
            pl.BlockSpec((1, A_WIDTH), const2),
            pl.BlockSpec((wo_rows, D_MODEL), lambda i: (i, 0)),
            pl.BlockSpec((S5_LANE_TILES, LANES, n_state_cols), const3, **once),
            pl.BlockSpec((S5_LANE_TILES, n_state_cols, LANES), const3, **once),
            pl.BlockSpec((S5_LANE_TILES, LANES, LANES), const3),
            pl.BlockSpec((S5_STATE_VREGS, SUBLANES, LANES), const3),
            pl.BlockSpec((S5_STATE_VREGS, SUBLANES, LANES), const3),
            pl.BlockSpec((1, B_WIDTH), const2),
            pl.BlockSpec((1, B_WIDTH), const2),
        ],
        out_specs=[
            pl.BlockSpec((tm, A_WIDTH), lambda i: (i, 0)),
            pl.BlockSpec((tm, B_WIDTH), lambda i: (i, 0)),
            pl.BlockSpec((wo_rows, D_MODEL), lambda i: (i, 0)),
        ],
        out_shape=[jax.ShapeDtypeStruct((SEQ, A_WIDTH), BF16),
                   jax.ShapeDtypeStruct((SEQ, B_WIDTH), BF16),
                   jax.ShapeDtypeStruct(w_out.shape, BF16)],
        scratch_shapes=[
            pltpu.VMEM((2 * S5_STATE_VREGS, SUBLANES, LANES), F32),
            pltpu.VMEM((2 * S5_STATE_TILES // S5_HALVES * S5_PITCH, LANES), F32),
        ],
        compiler_params=pltpu.CompilerParams(
            dimension_semantics=("arbitrary",), vmem_limit_bytes=VMEM_LIMIT),
        name="mixer",
    )(x, g, w_bf16, ln_g, ln_b, w_s, bias_full, out_a_g, w_out,
      wb, wc, wglu, a_re, a_im, d, out_b_g)


def _repeat_cols_onehot(n_rows, n_cols):
    r = lax.broadcasted_iota(jnp.int32, (n_rows, n_cols), 0)
    c = lax.broadcasted_iota(jnp.int32, (n_rows, n_cols), 1)
    return jnp.where(r == c % n_rows, 1.0, 0.0).astype(BF16)


def _block_diag_mask(n_rows, n_cols, row_block, col_block):
    r = lax.broadcasted_iota(jnp.int32, (n_rows, n_cols), 0)
    c = lax.broadcasted_iota(jnp.int32, (n_rows, n_cols), 1)
    return r // row_block == c // col_block


def _s5_params_kernel(lre_ref, lim_ref, ldt_ref, bre_ref, bim_ref, cre_ref, cim_ref, wg_ref,
                      abr_ref, abi_ref, wb_ref, wc_ref, wglu_ref):
    lr, li = lre_ref[...], lim_ref[...]
    dt = jnp.exp(ldt_ref[...])
    mag = jnp.exp(lr * dt)
    ab_r = mag * jnp.cos(li * dt)
    ab_i = mag * jnp.sin(li * dt)
    den = lr * lr + li * li
    nr = ab_r - 1.0
    co_r = (nr * lr + ab_i * li) / den
    co_i = (ab_i * lr - nr * li) / den
    abr_ref[...] = ab_r
    abi_ref[...] = ab_i

    gt, h, p = S5_GROUPS_PER_TILE, S5_GROUP_CH, S5_STATE
    spread_p = _repeat_cols_onehot(p, gt * p)
    spread_h = _repeat_cols_onehot(h, gt * h)
    eye_gh = _repeat_cols_onehot(gt * h, gt * h)
    mask_b = _block_diag_mask(gt * h, gt * p, h, p)
    mask_c = _block_diag_mask(gt * p, gt * h, p, h)
    mask_g = _block_diag_mask(gt * h, gt * h, h, h)
    for k in range(S5_LANE_TILES):
        grp = slice(k * gt, (k + 1) * gt)
        cr = co_r[grp][:, None, :]
        ci = co_i[grp][:, None, :]
        br, bi = bre_ref[grp], bim_ref[grp]
        bb_r = (cr * br - ci * bi).reshape(gt * h, p).astype(BF16)
        bb_i = (cr * bi + ci * br).reshape(gt * h, p).astype(BF16)
        for half, bb in enumerate((bb_r, bb_i)):
            full = jnp.dot(bb, spread_p, preferred_element_type=F32)
            wb_ref[k, :, half * gt * p:(half + 1) * gt * p] = (
                jnp.where(mask_b, full, 0.0).astype(BF16))
        for half, (c_ref, sign) in enumerate(((cre_ref, 1.0), (cim_ref, -1.0))):
            c_k = (sign * c_ref[grp]).reshape(gt * h, p).astype(BF16)
            c_t = lax.dot_general(c_k, eye_gh, (((0,), (0,)), ((), ())),
                                  preferred_element_type=F32)
            full = jnp.concatenate([c_t] * gt, axis=0)
            wc_ref[k, half * gt * p:(half + 1) * gt * p, :] = (
                jnp.where(mask_c, full, 0.0).astype(BF16))
        g_k = wg_ref[grp].reshape(gt * h, h).astype(BF16)
        full = jnp.dot(g_k, spread_h, preferred_element_type=F32)
        wglu_ref[k] = jnp.where(mask_g, full, 0.0).astype(BF16)


def _s5_params(lam_re, lam_im, log_dt, b_re, b_im, c_re, c_im, w_glu):
    gt, h, p = S5_GROUPS_PER_TILE, S5_GROUP_CH, S5_STATE
    g_p = jax.ShapeDtypeStruct((S5_GROUPS, p), F32)
    ab_r, ab_i, wb, wc, wglu = pl.pallas_call(
        _s5_params_kernel,
        out_shape=[g_p, g_p,
                   jax.ShapeDtypeStruct((S5_LANE_TILES, gt * h, 2 * gt * p), BF16),
                   jax.ShapeDtypeStruct((S5_LANE_TILES, 2 * gt * p, gt * h), BF16),
                   jax.ShapeDtypeStruct((S5_LANE_TILES, gt * h, gt * h), BF16)],
        name="s5_params",
    )(lam_re, lam_im, log_dt[:, None], jnp.swapaxes(b_re, 1, 2), jnp.swapaxes(b_im, 1, 2),
      c_re, c_im, w_glu)
    a_re = ab_r.reshape(S5_STATE_VREGS, SUBLANES, LANES)
    a_im = ab_i.reshape(S5_STATE_VREGS, SUBLANES, LANES)
    return wb, wc, wglu, a_re, a_im


def _outproj_kernel(x_ref, ma_ref, mb_ref, wa_ref, wb_ref, g_ref, x1_ref, hn_ref):
    half = x_ref.shape[0] // 2
    for r in range(2):
        rows = slice(r * half, (r + 1) * half)
        acc = jnp.dot(ma_ref[rows, :], wa_ref[...], preferred_element_type=F32)
        acc = acc + jnp.dot(mb_ref[rows, :], wb_ref[...], preferred_element_type=F32)
        x1 = x_ref[rows, :] + acc
        x1_ref[rows, :] = x1
        hn_ref[rows, :] = (x1 * _rms_scale(x1) * g_ref[...]).astype(BF16)


def _outproj(x, mix_a, mix_b, w_out_bf16, g, tm=512):
    return pl.pallas_call(
        _outproj_kernel,
        grid=(SEQ // tm,),
        in_specs=[
            pl.BlockSpec((tm, D_MODEL), lambda i: (i, 0)),
            pl.BlockSpec((tm, A_WIDTH), lambda i: (i, 0)),
            pl.BlockSpec((tm, B_WIDTH), lambda i: (i, 0)),
            pl.BlockSpec((A_WIDTH, D_MODEL), lambda i: (0, 0)),
            pl.BlockSpec((B_WIDTH, D_MODEL), lambda i: (1, 0)),
            pl.BlockSpec((1, D_MODEL), lambda i: (0, 0)),
        ],
        out_specs=[
            pl.BlockSpec((tm, D_MODEL), lambda i: (i, 0)),
            pl.BlockSpec((tm, D_MODEL), lambda i: (i, 0)),
        ],
        out_shape=[jax.ShapeDtypeStruct((SEQ, D_MODEL), F32),
                   jax.ShapeDtypeStruct((SEQ, D_MODEL), BF16)],
        compiler_params=pltpu.CompilerParams(
            dimension_semantics=("arbitrary",), vmem_limit_bytes=VMEM_LIMIT),
        name="outproj",
    )(x, mix_a, mix_b, w_out_bf16, w_out_bf16, g)


def _ffn_up_kernel(hn_ref, wg_ref, wu_ref, wd_ref, act_ref, wdb_ref):
    wdb_ref[...] = wd_ref[...].astype(BF16)
    tm = hn_ref.shape[0]
    n_slabs = act_ref.shape[1] // MXU_COLS
    for c in range(n_slabs):
        cols = slice(c * MXU_COLS, (c + 1) * MXU_COLS)
        wg = wg_ref[:, cols].astype(BF16)
        wu = wu_ref[:, cols].astype(BF16)
        n_parts = 2 if c == n_slabs - 1 else 1
        for r in range(n_parts):
            rows = slice(r * tm // n_parts, (r + 1) * tm // n_parts)
            hn = hn_ref[rows, :]
            gate = jnp.dot(hn, wg, preferred_element_type=F32)
            up = jnp.dot(hn, wu, preferred_element_type=F32)
            act_ref[rows, cols] = (gate * jax.nn.sigmoid(gate) * up).astype(BF16)


def _ffn_up(hn, wg, wu, wd, tm=2048, tf=512):
    n_i = SEQ // tm
    wd_rows = wd.shape[0] // (D_FF // tf * n_i)
    wd_slab = pl.BlockSpec((wd_rows, D_MODEL), lambda f, i: (f * n_i + i, 0))
    return pl.pallas_call(
        _ffn_up_kernel,
        grid=(D_FF // tf, n_i),
        in_specs=[
            pl.BlockSpec((tm, D_MODEL), lambda f, i: (i, 0)),
            pl.BlockSpec((D_MODEL, tf), lambda f, i: (0, f)),
            pl.BlockSpec((D_MODEL, tf), lambda f, i: (0, f)),
            wd_slab,
        ],
        out_specs=[pl.BlockSpec((tm, tf), lambda f, i: (i, f)), wd_slab],
        out_shape=[jax.ShapeDtypeStruct((SEQ, D_FF), BF16),
                   jax.ShapeDtypeStruct(wd.shape, BF16)],
        compiler_params=pltpu.CompilerParams(
            dimension_semantics=("arbitrary", "arbitrary"),
            vmem_limit_bytes=VMEM_LIMIT),
        name="ffn_up",
    )(hn, wg, wu, wd)


def _ffn_down_kernel(act_hbm, wd_ref, x1_ref, g_ref, o_ref, act_buf, act_sem):
    i, n = pl.program_id(0), pl.program_id(1)
    n_row_tiles, n_col_tiles = pl.num_programs(0), pl.num_programs(1)
    tm = act_buf.shape[1]
    tn = wd_ref.shape[1]
    n_chunks = o_ref.shape[1] // tn
    chunk = tm // n_chunks
    slot = lax.rem(i, 2)

    def chunk_copy(row_tile, c, dst_slot):
        src = act_hbm.at[pl.ds(pl.multiple_of(row_tile * tm + c * chunk, chunk), chunk), :]
        dst = act_buf.at[dst_slot, pl.ds(pl.multiple_of(c * chunk, chunk), chunk), :]
        return pltpu.make_async_copy(src, dst, act_sem.at[dst_slot, c])

    @pl.when((i == 0) & (n == 0))
    def _():
        for c in range(n_chunks):
            chunk_copy(0, c, 0).start()

    @pl.when(i + 1 < n_row_tiles)
    def _():
        chunk_copy(i + 1, n, 1 - slot).start()

    @pl.when(n == 0)
    def _():
        for c in range(n_chunks):
            chunk_copy(i, c, slot).wait()

    for col in range(n_chunks):
        @pl.when(n == col)
        def _(col=col):
            o_ref[:, col * tn:(col + 1) * tn] = x1_ref[...] + jnp.dot(
                act_buf[slot], wd_ref[...], preferred_element_type=F32)

    @pl.when(n == n_col_tiles - 1)
    def _():
        x2 = o_ref[...]
        o_ref[...] = x2 * _rms_scale(x2) * g_ref[...]


def _ffn_down(act, wd_bf16, x1, g, tm=1024, tn=512):
    n_col_tiles = D_MODEL // tn
    return pl.pallas_call(
        _ffn_down_kernel,
        grid=(SEQ // tm, n_col_tiles),
        in_specs=[
            pl.BlockSpec(memory_space=pl.ANY),
            pl.BlockSpec((D_FF, tn), lambda i, n: (0, n)),
            pl.BlockSpec((tm, tn), lambda i, n: (i, n)),
            pl.BlockSpec((1, D_MODEL), lambda i, n: (0, 0)),
        ],
        out_specs=pl.BlockSpec((tm, D_MODEL), lambda i, n: (i, 0)),
        out_shape=jax.ShapeDtypeStruct((SEQ, D_MODEL), F32),
        scratch_shapes=[
            pltpu.VMEM((2, tm, D_FF), BF16),
            pltpu.SemaphoreType.DMA((2, n_col_tiles)),
        ],
        compiler_params=pltpu.CompilerParams(
            dimension_semantics=("arbitrary", "arbitrary"),
            vmem_limit_bytes=FFN_DOWN_VMEM_LIMIT),
        name="ffn_down",
    )(act, wd_bf16, x1, g)


def kernel(x, norm_mix_g, w_in, a_ln_g, a_ln_b, a_w_s, a_b_s, s5_lambda_re, s5_lambda_im,
           s5_log_dt, s5_b_re, s5_b_im, s5_c_re, s5_c_im, s5_d, s5_w_glu, out_norm_a_g,
           out_norm_b_g, w_out, norm_ffn_g, w_gate, w_up, w_down, final_norm_g):
    assert x.shape == (1, SEQ, D_MODEL) and norm_mix_g.shape[0] == 1
    xs = x.reshape(SEQ, D_MODEL)
    l = 0

    bias_full = jnp.repeat(jnp.transpose(a_b_s[l]), A_HEAD_DIM, axis=1)
    wb, wc, wglu, a_re, a_im = _s5_params(
        s5_lambda_re[l], s5_lambda_im[l], s5_log_dt[l], s5_b_re[l], s5_b_im[l],
        s5_c_re[l], s5_c_im[l], s5_w_glu[l])
    mix_a, mix_b, w_out_bf16 = _mixer(
        xs, norm_mix_g[l][None], w_in[l].astype(BF16), a_ln_g[l][None], a_ln_b[l][None],
        a_w_s[l], bias_full, out_norm_a_g[l][None], w_out[l],
        wb, wc, wglu, a_re, a_im, s5_d[l][None], out_norm_b_g[l][None])

    x1, hn = _outproj(xs, mix_a, mix_b, w_out_bf16, norm_ffn_g[l][None])

    act, w_down_bf16 = _ffn_up(hn, w_gate[l], w_up[l], w_down[l])
    out = _ffn_down(act, w_down_bf16, x1, final_norm_g[None])
    return out.reshape(1, SEQ, D_MODEL)
```

```python
import math

import jax
import jax.numpy as jnp
from jax import lax
from jax.experimental import pallas as pl
from jax.experimental.pallas import tpu as pltpu

F32 = jnp.float32
BF16 = jnp.bfloat16

D_MODEL = 2048
SEQ = 8192
CHUNK = 64
A_WIDTH = 1024
A_HEADS = 8
A_HEAD_DIM = 128
GMLP_BLOCK = 128
B_WIDTH = 1024
S5_GROUP_CH = 16
S5_GROUPS = 64
S5_STATE = 64
IN_WIDTH = 3072
D_FF = 5632
EPS = 1e-6

LANES = 128
SUBLANES = 8
MXU_COLS = 256
VMEM_LIMIT = 56 * 1024 * 1024
BIG_TILE_VMEM_LIMIT = 60 * 1024 * 1024

S5_LANE_TILES = B_WIDTH // LANES
S5_GROUPS_PER_TILE = LANES // S5_GROUP_CH
S5_STATE_TILES = S5_GROUPS * S5_STATE // LANES
S5_TILES_PER_LANE_TILE = S5_STATE_TILES // S5_LANE_TILES
S5_STATE_VREGS = S5_STATE_TILES // SUBLANES
S5_T = 512
S5_HALVES = 2
S5_PITCH = S5_T + 4


def _gelu(x):
    return 0.5 * x * (1.0 + lax.erf(x * (1.0 / math.sqrt(2.0))))


def _rms_scale(x):
    return lax.rsqrt(jnp.mean(x * x, axis=-1, keepdims=True) + EPS)


def _layernorm_bf16(z, g_ref, b_ref):
    mu = jnp.mean(z, axis=-1, keepdims=True)
    zc = z - mu
    var = jnp.mean(zc * zc, axis=-1, keepdims=True)
    return (zc * lax.rsqrt(var + EPS) * g_ref[...] + b_ref[...]).astype(BF16)


def _spatial_mix(v, ws_ref):
    n_blocks = v.shape[0] // GMLP_BLOCK
    ci = lax.broadcasted_iota(jnp.int32, (GMLP_BLOCK, GMLP_BLOCK), 0) // CHUNK
    cj = lax.broadcasted_iota(jnp.int32, (GMLP_BLOCK, GMLP_BLOCK), 1) // CHUNK
    mask = ci >= cj
    heads = []
    for h in range(A_HEADS):
        w = jnp.where(mask, ws_ref[h], 0.0).astype(BF16)
        cols = slice(h * A_HEAD_DIM, (h + 1) * A_HEAD_DIM)
        rhs = jnp.concatenate(
            [v[n * GMLP_BLOCK:(n + 1) * GMLP_BLOCK, cols] for n in range(n_blocks)], axis=1)
        heads.append(jnp.dot(w, rhs, preferred_element_type=F32))
    return jnp.concatenate(
        [jnp.concatenate([hd[:, n * A_HEAD_DIM:(n + 1) * A_HEAD_DIM] for hd in heads], axis=1)
         for n in range(n_blocks)], axis=0)


def _mixer_kernel(x_ref, g_ref, w_ref, lng_ref, lnb_ref, ws_ref, bias_ref, oga_ref, wo_ref,
                  wb_ref, wc_ref, wglu_ref, are_ref, aim_ref, d_ref, ogb_ref,
                  mixa_ref, mixb_ref, wob_ref, st_ref, bu_ref):
    t_rows = x_ref.shape[0]
    tile = lambda k: slice(k * LANES, (k + 1) * LANES)
    half_lane_tiles = S5_LANE_TILES // S5_HALVES
    half_state_tiles = S5_STATE_TILES // S5_HALVES
    half_vregs = S5_STATE_VREGS // S5_HALVES
    im_base = half_state_tiles * S5_PITCH

    def scratch_rows(local_tile):
        return slice(local_tile * S5_PITCH, local_tile * S5_PITCH + t_rows)

    def half_state_tiles_of(kl):
        re = [kl * S5_TILES_PER_LANE_TILE + c for c in range(S5_TILES_PER_LANE_TILE)]
        return re + [half_state_tiles + q for q in re]

    @pl.when(pl.program_id(0) == 0)
    def _():
        st_ref[...] = jnp.zeros_like(st_ref)

    wob_ref[...] = wo_ref[...].astype(BF16)

    x = x_ref[...]
    xn = (x * _rms_scale(x) * g_ref[...]).astype(BF16)
    u = jnp.dot(xn, w_ref[:, 2 * A_WIDTH:], preferred_element_type=F32)

    h_pre, yc, ys, gates = [], [], [], []
    for hf in range(S5_HALVES):
        for kl in range(half_lane_tiles):
            k = hf * half_lane_tiles + kl
            res = jnp.dot(u[:, tile(k)].astype(BF16), wb_ref[k],
                          preferred_element_type=F32)
            for c, q in enumerate(half_state_tiles_of(kl)):
                bu_ref[scratch_rows(q), :] = res[:, tile(c)]

        gates += [jnp.dot(ys[k].astype(BF16), wglu_ref[k], preferred_element_type=F32)
                  for k in range(len(gates), len(ys))]

        h_pre.append(jnp.dot(xn, w_ref[:, hf * A_WIDTH:(hf + 1) * A_WIDTH],
                             preferred_element_type=F32))

        vregs = [hf * half_vregs + m for m in range(half_vregs)]
        a_re = [are_ref[gm] for gm in vregs]
        a_im = [aim_ref[gm] for gm in vregs]
        s_re = [st_ref[gm] for gm in vregs]
        s_im = [st_ref[S5_STATE_VREGS + gm] for gm in vregs]
        for t in range(t_rows):
            for m in range(half_vregs):
                re_rows = pl.ds(t + SUBLANES * m * S5_PITCH, SUBLANES, stride=S5_PITCH)
                im_rows = pl.ds(t + im_base + SUBLANES * m * S5_PITCH, SUBLANES, stride=S5_PITCH)
                nr = a_re[m] * s_re[m] - a_im[m] * s_im[m] + bu_ref[re_rows, :]
                ni = a_re[m] * s_im[m] + a_im[m] * s_re[m] + bu_ref[im_rows, :]
                bu_ref[re_rows, :] = nr
                bu_ref[im_rows, :] = ni
                s_re[m], s_im[m] = nr, ni
        for m, gm in enumerate(vregs):
            st_ref[gm] = s_re[m]
            st_ref[S5_STATE_VREGS + gm] = s_im[m]

        for kl in range(half_lane_tiles):
            k = hf * half_lane_tiles + kl
            lhs = jnp.concatenate([bu_ref[scratch_rows(q), :] for q in half_state_tiles_of(kl)],
                                  axis=1).astype(BF16)
            yc.append(jnp.dot(lhs, wc_ref[k], preferred_element_type=F32))

        ys += [_gelu(yc[k] + d_ref[:, tile(k)] * u[:, tile(k)]) for k in range(len(ys), len(yc))]
        if hf == 0:
            zu = _gelu(h_pre[0])

    gates += [jnp.dot(ys[k].astype(BF16), wglu_ref[k], preferred_element_type=F32)
              for k in range(len(gates), len(ys))]
    yb = jnp.concatenate([ys[k] * jax.nn.sigmoid(gates[k]) for k in range(S5_LANE_TILES)], axis=1)
    mixb_ref[...] = (yb * _rms_scale(yb) * ogb_ref[...]).astype(BF16)

    v = _layernorm_bf16(_gelu(h_pre[1]), lng_ref, lnb_ref)
    bias = jnp.concatenate([bias_ref[...]] * (t_rows // GMLP_BLOCK), axis=0)
    ya = zu * (_spatial_mix(v, ws_ref) + bias)
    mixa_ref[...] = (ya * _rms_scale(ya) * oga_ref[...]).astype(BF16)


def _mixer(x, g, w_bf16, ln_g, ln_b, w_s, bias_full, out_a_g, w_out,
           wb, wc, wglu, a_re, a_im, d, out_b_g):
    tm = S5_T
    n_steps = SEQ // tm
    wo_rows = w_out.shape[0] // n_steps
    n_state_cols = 2 * S5_GROUPS_PER_TILE * S5_STATE
    const2 = lambda i: (0, 0)
    const3 = lambda i: (0, 0, 0)
    once = dict(pipeline_mode=pl.Buffered(1))
    return pl.pallas_call(
        _mixer_kernel,
        grid=(n_steps,),
        in_specs=[
            pl.BlockSpec((tm, D_MODEL), lambda i: (i, 0)),
            pl.BlockSpec((1, D_MODEL), const2),
            pl.BlockSpec((D_MODEL, IN_WIDTH), const2, **once),
            pl.BlockSpec((1, A_WIDTH), const2),
            pl.BlockSpec((1, A_WIDTH), const2),
            pl.BlockSpec((A_HEADS, GMLP_BLOCK, GMLP_BLOCK), const3),
            pl.BlockSpec((GMLP_BLOCK, A_WIDTH), const2),
            pl.BlockSpec((1, A_WIDTH), const2),
            pl.BlockSpec((wo_rows, D_MODEL), lambda i: (i, 0)),
            pl.BlockSpec((S5_LANE_TILES, LANES, n_state_cols), const3, **once),
            pl.BlockSpec((S5_LANE_TILES, n_state_cols, LANES), const3, **once),
            pl.BlockSpec((S5_LANE_TILES, LANES, LANES), const3),
            pl.BlockSpec((S5_STATE_VREGS, SUBLANES, LANES), const3),
            pl.BlockSpec((S5_STATE_VREGS, SUBLANES, LANES), const3),
            pl.BlockSpec((1, B_WIDTH), const2),
            pl.BlockSpec((1, B_WIDTH), const2),
        ],
        out_specs=[
            pl.BlockSpec((tm, A_WIDTH), lambda i: (i, 0)),
            pl.BlockSpec((tm, B_WIDTH), lambda i: (i, 0)),
            pl.BlockSpec((wo_rows, D_MODEL), lambda i: (i, 0)),
        ],
        out_shape=[jax.ShapeDtypeStruct((SEQ, A_WIDTH), BF16),
                   jax.ShapeDtypeStruct((SEQ, B_WIDTH), BF16),
                   jax.ShapeDtypeStruct(w_out.shape, BF16)],
        scratch_shapes=[
            pltpu.VMEM((2 * S5_STATE_VREGS, SUBLANES, LANES), F32),
            pltpu.VMEM((2 * S5_STATE_TILES // S5_HALVES * S5_PITCH, LANES), F32),
        ],
        compiler_params=pltpu.CompilerParams(
            dimension_semantics=("arbitrary",), vmem_limit_bytes=VMEM_LIMIT),
        name="mixer",
    )(x, g, w_bf16, ln_g, ln_b, w_s, bias_full, out_a_g, w_out,
      wb, wc, wglu, a_re, a_im, d, out_b_g)


def _repeat_cols_onehot(n_rows, n_cols):
    r = lax.broadcasted_iota(jnp.int32, (n_rows, n_cols), 0)
    c = lax.broadcasted_iota(jnp.int32, (n_rows, n_cols), 1)
    return jnp.where(r == c % n_rows, 1.0, 0.0).astype(BF16)


def _block_diag_mask(n_rows, n_cols, row_block, col_block):
    r = lax.broadcasted_iota(jnp.int32, (n_rows, n_cols), 0)
    c = lax.broadcasted_iota(jnp.int32, (n_rows, n_cols), 1)
    return r // row_block == c // col_block


def _s5_params_kernel(lre_ref, lim_ref, ldt_ref, bre_ref, bim_ref, cre_ref, cim_ref, wg_ref,
                      abr_ref, abi_ref, wb_ref, wc_ref, wglu_ref):
    lr, li = lre_ref[...], lim_ref[...]
    dt = jnp.exp(ldt_ref[...])
    mag = jnp.exp(lr * dt)
    ab_r = mag * jnp.cos(li * dt)
    ab_i = mag * jnp.sin(li * dt)
    den = lr * lr + li * li
    nr = ab_r - 1.0
    co_r = (nr * lr + ab_i * li) / den
    co_i = (ab_i * lr - nr * li) / den
    abr_ref[...] = ab_r
    abi_ref[...] = ab_i

    gt, h, p = S5_GROUPS_PER_TILE, S5_GROUP_CH, S5_STATE
    spread_p = _repeat_cols_onehot(p, gt * p)
    spread_h = _repeat_cols_onehot(h, gt * h)
    eye_gh = _repeat_cols_onehot(gt * h, gt * h)
    mask_b = _block_diag_mask(gt * h, gt * p, h, p)
    mask_c = _block_diag_mask(gt * p, gt * h, p, h)
    mask_g = _block_diag_mask(gt * h, gt * h, h, h)
    for k in range(S5_LANE_TILES):
        grp = slice(k * gt, (k + 1) * gt)
        cr = co_r[grp][:, None, :]
        ci = co_i[grp][:, None, :]
        br, bi = bre_ref[grp], bim_ref[grp]
        bb_r = (cr * br - ci * bi).reshape(gt * h, p).astype(BF16)
        bb_i = (cr * bi + ci * br).reshape(gt * h, p).astype(BF16)
        for half, bb in enumerate((bb_r, bb_i)):
            full = jnp.dot(bb, spread_p, preferred_element_type=F32)
            wb_ref[k, :, half * gt * p:(half + 1) * gt * p] = (
                jnp.where(mask_b, full, 0.0).astype(BF16))
        for half, (c_ref, sign) in enumerate(((cre_ref, 1.0), (cim_ref, -1.0))):
            c_k = (sign * c_ref[grp]).reshape(gt * h, p).astype(BF16)
            c_t = lax.dot_general(c_k, eye_gh, (((0,), (0,)), ((), ())),
                                  preferred_element_type=F32)
            full = jnp.concatenate([c_t] * gt, axis=0)
            wc_ref[k, half * gt * p:(half + 1) * gt * p, :] = (
                jnp.where(mask_c, full, 0.0).astype(BF16))
        g_k = wg_ref[grp].reshape(gt * h, h).astype(BF16)
        full = jnp.dot(g_k, spread_h, preferred_element_type=F32)
        wglu_ref[k] = jnp.where(mask_g, full, 0.0).astype(BF16)


def _s5_params(lam_re, lam_im, log_dt, b_re, b_im, c_re, c_im, w_glu):
    gt, h, p = S5_GROUPS_PER_TILE, S5_GROUP_CH, S5_STATE
    g_p = jax.ShapeDtypeStruct((S5_GROUPS, p), F32)
    ab_r, ab_i, wb, wc, wglu = pl.pallas_call(
        _s5_params_kernel,
        out_shape=[g_p, g_p,
                   jax.ShapeDtypeStruct((S5_LANE_TILES, gt * h, 2 * gt * p), BF16),
                   jax.ShapeDtypeStruct((S5_LANE_TILES, 2 * gt * p, gt * h), BF16),
                   jax.ShapeDtypeStruct((S5_LANE_TILES, gt * h, gt * h), BF16)],
        name="s5_params",
    )(lam_re, lam_im, log_dt[:, None], jnp.swapaxes(b_re, 1, 2), jnp.swapaxes(b_im, 1, 2),
      c_re, c_im, w_glu)
    a_re = ab_r.reshape(S5_STATE_VREGS, SUBLANES, LANES)
    a_im = ab_i.reshape(S5_STATE_VREGS, SUBLANES, LANES)
    return wb, wc, wglu, a_re, a_im


def _outproj_kernel(x_ref, ma_ref, mb_ref, wa_ref, wb_ref, g_ref, x1_ref, hn_ref):
    half = x_ref.shape[0] // 2
    for r in range(2):
        rows = slice(r * half, (r + 1) * half)
        acc = jnp.dot(ma_ref[rows, :], wa_ref[...], preferred_element_type=F32)
        acc = acc + jnp.dot(mb_ref[rows, :], wb_ref[...], preferred_element_type=F32)
        x1 = x_ref[rows, :] + acc
        x1_ref[rows, :] = x1
        hn_ref[rows, :] = (x1 * _rms_scale(x1) * g_ref[...]).astype(BF16)


def _outproj(x, mix_a, mix_b, w_out_bf16, g, tm=1024):
    return pl.pallas_call(
        _outproj_kernel,
        grid=(SEQ // tm,),
        in_specs=[
            pl.BlockSpec((tm, D_MODEL), lambda i: (i, 0)),
            pl.BlockSpec((tm, A_WIDTH), lambda i: (i, 0)),
            pl.BlockSpec((tm, B_WIDTH), lambda i: (i, 0)),
            pl.BlockSpec((A_WIDTH, D_MODEL), lambda i: (0, 0)),
            pl.BlockSpec((B_WIDTH, D_MODEL), lambda i: (1, 0)),
            pl.BlockSpec((1, D_MODEL), lambda i: (0, 0)),
        ],
        out_specs=[
            pl.BlockSpec((tm, D_MODEL), lambda i: (i, 0)),
            pl.BlockSpec((tm, D_MODEL), lambda i: (i, 0)),
        ],
        out_shape=[jax.ShapeDtypeStruct((SEQ, D_MODEL), F32),
                   jax.ShapeDtypeStruct((SEQ, D_MODEL), BF16)],
        compiler_params=pltpu.CompilerParams(
            dimension_semantics=("arbitrary",), vmem_limit_bytes=BIG_TILE_VMEM_LIMIT),
        name="outproj",
    )(x, mix_a, mix_b, w_out_bf16, w_out_bf16, g)


def _ffn_up_kernel(hn_ref, wg_ref, wu_ref, wd_ref, act_ref, wdb_ref):
    wdb_ref[...] = wd_ref[...].astype(BF16)
    tm = hn_ref.shape[0]
    n_slabs = act_ref.shape[1] // MXU_COLS
    for c in range(n_slabs):
        cols = slice(c * MXU_COLS, (c + 1) * MXU_COLS)
        wg = wg_ref[:, cols].astype(BF16)
        wu = wu_ref[:, cols].astype(BF16)
        n_parts = 2 if c == n_slabs - 1 else 1
        for r in range(n_parts):
            rows = slice(r * tm // n_parts, (r + 1) * tm // n_parts)
            hn = hn_ref[rows, :]
            gate = jnp.dot(hn, wg, preferred_element_type=F32)
            up = jnp.dot(hn, wu, preferred_element_type=F32)
            act_ref[rows, cols] = (gate * jax.nn.sigmoid(gate) * up).astype(BF16)


def _ffn_up(hn, wg, wu, wd, tm=2048, tf=512):
    n_i = SEQ // tm
    wd_rows = wd.shape[0] // (D_FF // tf * n_i)
    wd_slab = pl.BlockSpec((wd_rows, D_MODEL), lambda f, i: (f * n_i + i, 0))
    return pl.pallas_call(
        _ffn_up_kernel,
        grid=(D_FF // tf, n_i),
        in_specs=[
            pl.BlockSpec((tm, D_MODEL), lambda f, i: (i, 0)),
            pl.BlockSpec((D_MODEL, tf), lambda f, i: (0, f)),
            pl.BlockSpec((D_MODEL, tf), lambda f, i: (0, f)),
            wd_slab,
        ],
        out_specs=[pl.BlockSpec((tm, tf), lambda f, i: (i, f)), wd_slab],
        out_shape=[jax.ShapeDtypeStruct((SEQ, D_FF), BF16),
                   jax.ShapeDtypeStruct(wd.shape, BF16)],
        compiler_params=pltpu.CompilerParams(
            dimension_semantics=("arbitrary", "arbitrary"),
            vmem_limit_bytes=VMEM_LIMIT),
        name="ffn_up",
    )(hn, wg, wu, wd)


def _ffn_down_kernel(act_ref, wd_ref, x1_ref, g_ref, o_ref):
    n_tiles = pl.num_programs(1)
    tn = wd_ref.shape[1]
    for n in range(o_ref.shape[1] // tn):
        @pl.when(pl.program_id(1) == n)
        def _(n=n):
            o_ref[:, n * tn:(n + 1) * tn] = x1_ref[...] + jnp.dot(
                act_ref[...], wd_ref[...], preferred_element_type=F32)

    @pl.when(pl.program_id(1) == n_tiles - 1)
    def _():
        x2 = o_ref[...]
        o_ref[...] = x2 * _rms_scale(x2) * g_ref[...]


def _ffn_down(act, wd_bf16, x1, g, tm=1024, tn=512):
    return pl.pallas_call(
        _ffn_down_kernel,
        grid=(SEQ // tm, D_MODEL // tn),
        in_specs=[
            pl.BlockSpec((tm, D_FF), lambda i, n: (i, 0)),
            pl.BlockSpec((D_FF, tn), lambda i, n: (0, n)),
            pl.BlockSpec((tm, tn), lambda i, n: (i, n)),
            pl.BlockSpec((1, D_MODEL), lambda i, n: (0, 0)),
        ],
        out_specs=pl.BlockSpec((tm, D_MODEL), lambda i, n: (i, 0)),
        out_shape=jax.ShapeDtypeStruct((SEQ, D_MODEL), F32),
        compiler_params=pltpu.CompilerParams(
            dimension_semantics=("arbitrary", "arbitrary"),
            vmem_limit_bytes=BIG_TILE_VMEM_LIMIT),
        name="ffn_down",
    )(act, wd_bf16, x1, g)


def kernel(x, norm_mix_g, w_in, a_ln_g, a_ln_b, a_w_s, a_b_s, s5_lambda_re, s5_lambda_im,
           s5_log_dt, s5_b_re, s5_b_im, s5_c_re, s5_c_im, s5_d, s5_w_glu, out_norm_a_g,
           out_norm_b_g, w_out, norm_ffn_g, w_gate, w_up, w_down, final_norm_g):
    assert x.shape == (1, SEQ, D_MODEL) and norm_mix_g.shape[0] == 1
    xs = x.reshape(SEQ, D_MODEL)
    l = 0

    bias_full = jnp.repeat(jnp.transpose(a_b_s[l]), A_HEAD_DIM, axis=1)
    wb, wc, wglu, a_re, a_im = _s5_params(
        s5_lambda_re[l], s5_lambda_im[l], s5_log_dt[l], s5_b_re[l], s5_b_im[l],
        s5_c_re[l], s5_c_im[l], s5_w_glu[l])
    mix_a, mix_b, w_out_bf16 = _mixer(
        xs, norm_mix_g[l][None], w_in[l].astype(BF16), a_ln_g[l][None], a_ln_b[l][None],
        a_w_s[l], bias_full, out_norm_a_g[l][None], w_out[l],
        wb, wc, wglu, a_re, a_im, s5_d[l][None], out_norm_b_g[l][None])

    x1, hn = _outproj(xs, mix_a, mix_b, w_out_bf16, norm_ffn_g[l][None])

    act, w_down_bf16 = _ffn_up(hn, w_gate[l], w_up[l], w_down[l])
    out = _ffn_down(act, w_down_bf16, x1, final_norm_g[None])
    return out.reshape(1, SEQ, D_MODEL)
```

```python
import math

import jax
import jax.numpy as jnp
from jax import lax
from jax.experimental import pallas as pl
from jax.experimental.pallas import tpu as pltpu

F32 = jnp.float32
BF16 = jnp.bfloat16

D_MODEL = 2048
SEQ = 8192
CHUNK = 64
A_WIDTH = 1024
A_HEADS = 8
A_HEAD_DIM = 128
GMLP_BLOCK = 128
B_WIDTH = 1024
S5_GROUP_CH = 16
S5_GROUPS = 64
S5_STATE = 64
IN_WIDTH = 3072
D_FF = 5632
EPS = 1e-6

LANES = 128
SUBLANES = 8
MXU_COLS = 256
VMEM_LIMIT = 56 * 1024 * 1024
FFN_DOWN_VMEM_LIMIT = 60 * 1024 * 1024
MIXER_VMEM_LIMIT = 60 * 1024 * 1024
W_IN_STAGE_ROWS = 256

S5_LANE_TILES = B_WIDTH // LANES
S5_GROUPS_PER_TILE = LANES // S5_GROUP_CH
S5_STATE_TILES = S5_GROUPS * S5_STATE // LANES
S5_TILES_PER_LANE_TILE = S5_STATE_TILES // S5_LANE_TILES
S5_STATE_VREGS = S5_STATE_TILES // SUBLANES
S5_T = 512
S5_HALVES = 2
S5_PITCH = S5_T + 4


def _gelu(x):
    return 0.5 * x * (1.0 + lax.erf(x * (1.0 / math.sqrt(2.0))))


def _rms_scale(x):
    return lax.rsqrt(jnp.mean(x * x, axis=-1, keepdims=True) + EPS)


def _layernorm_bf16(z, g_ref, b_ref):
    mu = jnp.mean(z, axis=-1, keepdims=True)
    zc = z - mu
    var = jnp.mean(zc * zc, axis=-1, keepdims=True)
    return (zc * lax.rsqrt(var + EPS) * g_ref[...] + b_ref[...]).astype(BF16)


def _spatial_mix(v, ws_ref):
    n_blocks = v.shape[0] // GMLP_BLOCK
    ci = lax.broadcasted_iota(jnp.int32, (GMLP_BLOCK, GMLP_BLOCK), 0) // CHUNK
    cj = lax.broadcasted_iota(jnp.int32, (GMLP_BLOCK, GMLP_BLOCK), 1) // CHUNK
    mask = ci >= cj
    heads = []
    for h in range(A_HEADS):
        w = jnp.where(mask, ws_ref[h], 0.0).astype(BF16)
        cols = slice(h * A_HEAD_DIM, (h + 1) * A_HEAD_DIM)
        rhs = jnp.concatenate(
            [v[n * GMLP_BLOCK:(n + 1) * GMLP_BLOCK, cols] for n in range(n_blocks)], axis=1)
        heads.append(jnp.dot(w, rhs, preferred_element_type=F32))
    return jnp.concatenate(
        [jnp.concatenate([hd[:, n * A_HEAD_DIM:(n + 1) * A_HEAD_DIM] for hd in heads], axis=1)
         for n in range(n_blocks)], axis=0)


def _mixer_kernel(x_ref, g_ref, w_hbm, lng_ref, lnb_ref, ws_ref, bias_ref, oga_ref, wo_ref,
                  wb_ref, wc_ref, wglu_ref, are_ref, aim_ref, d_ref, ogb_ref,
                  mixa_ref, mixb_ref, wob_ref, st_ref, bu_ref, w_ref, stage_ref, stage_sem):
    t_rows = x_ref.shape[0]
    tile = lambda k: slice(k * LANES, (k + 1) * LANES)
    half_lane_tiles = S5_LANE_TILES // S5_HALVES
    half_state_tiles = S5_STATE_TILES // S5_HALVES
    half_vregs = S5_STATE_VREGS // S5_HALVES
    im_base = half_state_tiles * S5_PITCH

    def scratch_rows(local_tile):
        return slice(local_tile * S5_PITCH, local_tile * S5_PITCH + t_rows)

    def half_state_tiles_of(kl):
        re = [kl * S5_TILES_PER_LANE_TILE + c for c in range(S5_TILES_PER_LANE_TILE)]
        return re + [half_state_tiles + q for q in re]

    @pl.when(pl.program_id(0) == 0)
    def _():
        st_ref[...] = jnp.zeros_like(st_ref)
        chunk = stage_ref.shape[1]
        n_chunks = w_ref.shape[0] // chunk

        def chunk_copy(c):
            return pltpu.make_async_copy(w_hbm.at[c * chunk:(c + 1) * chunk, :],
                                         stage_ref.at[c % 2], stage_sem.at[c % 2])

        chunk_copy(0).start()
        for c in range(n_chunks):
            if c + 1 < n_chunks:
                chunk_copy(c + 1).start()
            chunk_copy(c).wait()
            w_ref[c * chunk:(c + 1) * chunk, :] = stage_ref[c % 2].astype(BF16)

    wob_ref[...] = wo_ref[...].astype(BF16)

    x = x_ref[...]
    xn = (x * _rms_scale(x) * g_ref[...]).astype(BF16)
    u = jnp.dot(xn, w_ref[:, 2 * A_WIDTH:], preferred_element_type=F32)

    h_pre, yc, ys, gates = [], [], [], []
    for hf in range(S5_HALVES):
        for kl in range(half_lane_tiles):
            k = hf * half_lane_tiles + kl
            res = jnp.dot(u[:, tile(k)].astype(BF16), wb_ref[k],
                          preferred_element_type=F32)
            for c, q in enumerate(half_state_tiles_of(kl)):
                bu_ref[scratch_rows(q), :] = res[:, tile(c)]

        gates += [jnp.dot(ys[k].astype(BF16), wglu_ref[k], preferred_element_type=F32)
                  for k in range(len(gates), len(ys))]

        h_pre.append(jnp.dot(xn, w_ref[:, hf * A_WIDTH:(hf + 1) * A_WIDTH],
                             preferred_element_type=F32))

        vregs = [hf * half_vregs + m for m in range(half_vregs)]
        a_re = [are_ref[gm] for gm in vregs]
        a_im = [aim_ref[gm] for gm in vregs]
        s_re = [st_ref[gm] for gm in vregs]
        s_im = [st_ref[S5_STATE_VREGS + gm] for gm in vregs]
        for t in range(t_rows):
            for m in range(half_vregs):
                re_rows = pl.ds(t + SUBLANES * m * S5_PITCH, SUBLANES, stride=S5_PITCH)
                im_rows = pl.ds(t + im_base + SUBLANES * m * S5_PITCH, SUBLANES, stride=S5_PITCH)
                nr = a_re[m] * s_re[m] - a_im[m] * s_im[m] + bu_ref[re_rows, :]
                ni = a_re[m] * s_im[m] + a_im[m] * s_re[m] + bu_ref[im_rows, :]
                bu_ref[re_rows, :] = nr
                bu_ref[im_rows, :] = ni
                s_re[m], s_im[m] = nr, ni
        for m, gm in enumerate(vregs):
            st_ref[gm] = s_re[m]
            st_ref[S5_STATE_VREGS + gm] = s_im[m]

        for kl in range(half_lane_tiles):
            k = hf * half_lane_tiles + kl
            lhs = jnp.concatenate([bu_ref[scratch_rows(q), :] for q in half_state_tiles_of(kl)],
                                  axis=1).astype(BF16)
            yc.append(jnp.dot(lhs, wc_ref[k], preferred_element_type=F32))

        ys += [_gelu(yc[k] + d_ref[:, tile(k)] * u[:, tile(k)]) for k in range(len(ys), len(yc))]
        if hf == 0:
            zu = _gelu(h_pre[0])

    gates += [jnp.dot(ys[k].astype(BF16), wglu_ref[k], preferred_element_type=F32)
              for k in range(len(gates), len(ys))]
    yb = jnp.concatenate([ys[k] * jax.nn.sigmoid(gates[k]) for k in range(S5_LANE_TILES)], axis=1)
    mixb_ref[...] = (yb * _rms_scale(yb) * ogb_ref[...]).astype(BF16)

    v = _layernorm_bf16(_gelu(h_pre[1]), lng_ref, lnb_ref)
    bias = jnp.concatenate([bias_ref[...]] * (t_rows // GMLP_BLOCK), axis=0)
    ya = zu * (_spatial_mix(v, ws_ref) + bias)
    mixa_ref[...] = (ya * _rms_scale(ya) * oga_ref[...]).astype(BF16)


def _mixer(x, g, w_in, ln_g, ln_b, w_s, bias_full, out_a_g, w_out,
           wb, wc, wglu, a_re, a_im, d, out_b_g):
    tm = S5_T
    n_steps = SEQ // tm
    wo_rows = w_out.shape[0] // n_steps
    n_state_cols = 2 * S5_GROUPS_PER_TILE * S5_STATE
    const2 = lambda i: (0, 0)
    const3 = lambda i: (0, 0, 0)
    once = dict(pipeline_mode=pl.Buffered(1))
    return pl.pallas_call(
        _mixer_kernel,
        grid=(n_steps,),
        in_specs=[
            pl.BlockSpec((tm, D_MODEL), lambda i: (i, 0)),
            pl.BlockSpec((1, D_MODEL), const2),
            pl.BlockSpec(memory_space=pl.ANY),
            pl.BlockSpec((1, A_WIDTH), const2),
            pl.BlockSpec((1, A_WIDTH), const2),
            pl.BlockSpec((A_HEADS, GMLP_BLOCK, GMLP_BLOCK), const3),
            pl.BlockSpec((GMLP_BLOCK, A_WIDTH), const2),
            pl.BlockSpec((1, A_WIDTH), const2),
            pl.BlockSpec((wo_rows, D_MODEL), lambda i: (i, 0)),
            pl.BlockSpec((S5_LANE_TILES, LANES, n_state_cols), const3, **once),
            pl.BlockSpec((S5_LANE_TILES, n_state_cols, LANES), const3, **once),
            pl.BlockSpec((S5_LANE_TILES, LANES, LANES), const3),
            pl.BlockSpec((S5_STATE_VREGS, SUBLANES, LANES), const3),
            pl.BlockSpec((S5_STATE_VREGS, SUBLANES, LANES), const3),
            pl.BlockSpec((1, B_WIDTH), const2),
            pl.BlockSpec((1, B_WIDTH), const2),
        ],
        out_specs=[
            pl.BlockSpec((tm, A_WIDTH), lambda i: (i, 0)),
            pl.BlockSpec((tm, B_WIDTH), lambda i: (i, 0)),
            pl.BlockSpec((wo_rows, D_MODEL), lambda i: (i, 0)),
        ],
        out_shape=[jax.ShapeDtypeStruct((SEQ, A_WIDTH), BF16),
                   jax.ShapeDtypeStruct((SEQ, B_WIDTH), BF16),
                   jax.ShapeDtypeStruct(w_out.shape, BF16)],
        scratch_shapes=[
            pltpu.VMEM((2 * S5_STATE_VREGS, SUBLANES, LANES), F32),
            pltpu.VMEM((2 * S5_STATE_TILES // S5_HALVES * S5_PITCH, LANES), F32),
            pltpu.VMEM((D_MODEL, IN_WIDTH), BF16),
            pltpu.VMEM((2, W_IN_STAGE_ROWS, IN_WIDTH), F32),
            pltpu.SemaphoreType.DMA((2,)),
        ],
        compiler_params=pltpu.CompilerParams(
            dimension_semantics=("arbitrary",), vmem_limit_bytes=MIXER_VMEM_LIMIT),
        name="mixer",
    )(x, g, w_in, ln_g, ln_b, w_s, bias_full, out_a_g, w_out,
      wb, wc, wglu, a_re, a_im, d, out_b_g)


def _repeat_cols_onehot(n_rows, n_cols):
    r = lax.broadcasted_iota(jnp.int32, (n_rows, n_cols), 0)
    c = lax.broadcasted_iota(jnp.int32, (n_rows, n_cols), 1)
    return jnp.where(r == c % n_rows, 1.0, 0.0).astype(BF16)


def _block_diag_mask(n_rows, n_cols, row_block, col_block):
    r = lax.broadcasted_iota(jnp.int32, (n_rows, n_cols), 0)
    c = lax.broadcasted_iota(jnp.int32, (n_rows, n_cols), 1)
    return r // row_block == c // col_block


def _s5_params_kernel(lre_ref, lim_ref, ldt_ref, bre_ref, bim_ref, cre_ref, cim_ref, wg_ref,
                      abr_ref, abi_ref, wb_ref, wc_ref, wglu_ref):
    lr, li = lre_ref[...], lim_ref[...]
    dt = jnp.exp(ldt_ref[...])
    mag = jnp.exp(lr * dt)
    ab_r = mag * jnp.cos(li * dt)
    ab_i = mag * jnp.sin(li * dt)
    den = lr * lr + li * li
    nr = ab_r - 1.0
    co_r = (nr * lr + ab_i * li) / den
    co_i = (ab_i * lr - nr * li) / den
    abr_ref[...] = ab_r
    abi_ref[...] = ab_i

    gt, h, p = S5_GROUPS_PER_TILE, S5_GROUP_CH, S5_STATE
    spread_p = _repeat_cols_onehot(p, gt * p)
    spread_h = _repeat_cols_onehot(h, gt * h)
    eye_gh = _repeat_cols_onehot(gt * h, gt * h)
    mask_b = _block_diag_mask(gt * h, gt * p, h, p)
    mask_c = _block_diag_mask(gt * p, gt * h, p, h)
    mask_g = _block_diag_mask(gt * h, gt * h, h, h)
    for k in range(S5_LANE_TILES):
        grp = slice(k * gt, (k + 1) * gt)
        cr = co_r[grp][:, None, :]
        ci = co_i[grp][:, None, :]
        br, bi = bre_ref[grp], bim_ref[grp]
        bb_r = (cr * br - ci * bi).reshape(gt * h, p).astype(BF16)
        bb_i = (cr * bi + ci * br).reshape(gt * h, p).astype(BF16)
        for half, bb in enumerate((bb_r, bb_i)):
            full = jnp.dot(bb, spread_p, preferred_element_type=F32)
            wb_ref[k, :, half * gt * p:(half + 1) * gt * p] = (
                jnp.where(mask_b, full, 0.0).astype(BF16))
        for half, (c_ref, sign) in enumerate(((cre_ref, 1.0), (cim_ref, -1.0))):
            c_k = (sign * c_ref[grp]).reshape(gt * h, p).astype(BF16)
            c_t = lax.dot_general(c_k, eye_gh, (((0,), (0,)), ((), ())),
                                  preferred_element_type=F32)
            full = jnp.concatenate([c_t] * gt, axis=0)
            wc_ref[k, half * gt * p:(half + 1) * gt * p, :] = (
                jnp.where(mask_c, full, 0.0).astype(BF16))
        g_k = wg_ref[grp].reshape(gt * h, h).astype(BF16)
        full = jnp.dot(g_k, spread_h, preferred_element_type=F32)
        wglu_ref[k] = jnp.where(mask_g, full, 0.0).astype(BF16)


def _s5_params(lam_re, lam_im, log_dt, b_re, b_im, c_re, c_im, w_glu):
    gt, h, p = S5_GROUPS_PER_TILE, S5_GROUP_CH, S5_STATE
    g_p = jax.ShapeDtypeStruct((S5_GROUPS, p), F32)
    ab_r, ab_i, wb, wc, wglu = pl.pallas_call(
        _s5_params_kernel,
        out_shape=[g_p, g_p,
                   jax.ShapeDtypeStruct((S5_LANE_TILES, gt * h, 2 * gt * p), BF16),
                   jax.ShapeDtypeStruct((S5_LANE_TILES, 2 * gt * p, gt * h), BF16),
                   jax.ShapeDtypeStruct((S5_LANE_TILES, gt * h, gt * h), BF16)],
        name="s5_params",
    )(lam_re, lam_im, log_dt[:, None], jnp.swapaxes(b_re, 1, 2), jnp.swapaxes(b_im, 1, 2),
      c_re, c_im, w_glu)
    a_re = ab_r.reshape(S5_STATE_VREGS, SUBLANES, LANES)
    a_im = ab_i.reshape(S5_STATE_VREGS, SUBLANES, LANES)
    return wb, wc, wglu, a_re, a_im


def _outproj_kernel(x_ref, ma_ref, mb_ref, wa_ref, wb_ref, g_ref, x1_ref, hn_ref):
    half = x_ref.shape[0] // 2
    for r in range(2):
        rows = slice(r * half, (r + 1) * half)
        acc = jnp.dot(ma_ref[rows, :], wa_ref[...], preferred_element_type=F32)
        acc = acc + jnp.dot(mb_ref[rows, :], wb_ref[...], preferred_element_type=F32)
        x1 = x_ref[rows, :] + acc
        x1_ref[rows, :] = x1
        hn_ref[rows, :] = (x1 * _rms_scale(x1) * g_ref[...]).astype(BF16)


def _outproj(x, mix_a, mix_b, w_out_bf16, g, tm=512):
    return pl.pallas_call(
        _outproj_kernel,
        grid=(SEQ // tm,),
        in_specs=[
            pl.BlockSpec((tm, D_MODEL), lambda i: (i, 0)),
            pl.BlockSpec((tm, A_WIDTH), lambda i: (i, 0)),
            pl.BlockSpec((tm, B_WIDTH), lambda i: (i, 0)),
            pl.BlockSpec((A_WIDTH, D_MODEL), lambda i: (0, 0)),
            pl.BlockSpec((B_WIDTH, D_MODEL), lambda i: (1, 0)),
            pl.BlockSpec((1, D_MODEL), lambda i: (0, 0)),
        ],
        out_specs=[
            pl.BlockSpec((tm, D_MODEL), lambda i: (i, 0)),
            pl.BlockSpec((tm, D_MODEL), lambda i: (i, 0)),
        ],
        out_shape=[jax.ShapeDtypeStruct((SEQ, D_MODEL), F32),
                   jax.ShapeDtypeStruct((SEQ, D_MODEL), BF16)],
        compiler_params=pltpu.CompilerParams(
            dimension_semantics=("arbitrary",), vmem_limit_bytes=VMEM_LIMIT),
        name="outproj",
    )(x, mix_a, mix_b, w_out_bf16, w_out_bf16, g)


def _ffn_up_kernel(hn_ref, wg_ref, wu_ref, wd_ref, act_ref, wdb_ref):
    wdb_ref[...] = wd_ref[...].astype(BF16)
    hn = hn_ref[...]
    for c in range(act_ref.shape[1] // MXU_COLS):
        cols = slice(c * MXU_COLS, (c + 1) * MXU_COLS)
        gate = jnp.dot(hn, wg_ref[:, cols].astype(BF16), preferred_element_type=F32)
        up = jnp.dot(hn, wu_ref[:, cols].astype(BF16), preferred_element_type=F32)
        act_ref[:, cols] = (gate * jax.nn.sigmoid(gate) * up).astype(BF16)


def _ffn_up(hn, wg, wu, wd, tm=2048, tf=512):
    n_i = SEQ // tm
    wd_rows = wd.shape[0] // (D_FF // tf * n_i)
    wd_slab = pl.BlockSpec((wd_rows, D_MODEL), lambda f, i: (f * n_i + i, 0))
    return pl.pallas_call(
        _ffn_up_kernel,
        grid=(D_FF // tf, n_i),
        in_specs=[
            pl.BlockSpec((tm, D_MODEL), lambda f, i: (i, 0)),
            pl.BlockSpec((D_MODEL, tf), lambda f, i: (0, f)),
            pl.BlockSpec((D_MODEL, tf), lambda f, i: (0, f)),
            wd_slab,
        ],
        out_specs=[pl.BlockSpec((tm, tf), lambda f, i: (i, f)), wd_slab],
        out_shape=[jax.ShapeDtypeStruct((SEQ, D_FF), BF16),
                   jax.ShapeDtypeStruct(wd.shape, BF16)],
        compiler_params=pltpu.CompilerParams(
            dimension_semantics=("arbitrary", "arbitrary"),
            vmem_limit_bytes=VMEM_LIMIT),
        name="ffn_up",
    )(hn, wg, wu, wd)


def _ffn_down_kernel(act_ref, wd_ref, x1_ref, g_ref, o_ref):
    n_tiles = pl.num_programs(1)
    tn = wd_ref.shape[1]
    for n in range(o_ref.shape[1] // tn):
        @pl.when(pl.program_id(1) == n)
        def _(n=n):
            o_ref[:, n * tn:(n + 1) * tn] = x1_ref[...] + jnp.dot(
                act_ref[...], wd_ref[...], preferred_element_type=F32)

    @pl.when(pl.program_id(1) == n_tiles - 1)
    def _():
        x2 = o_ref[...]
        o_ref[...] = x2 * _rms_scale(x2) * g_ref[...]


def _ffn_down(act, wd_bf16, x1, g, tm=1024, tn=512):
    return pl.pallas_call(
        _ffn_down_kernel,
        grid=(SEQ // tm, D_MODEL // tn),
        in_specs=[
            pl.BlockSpec((tm, D_FF), lambda i, n: (i, 0)),
            pl.BlockSpec((D_FF, tn), lambda i, n: (0, n)),
            pl.BlockSpec((tm, tn), lambda i, n: (i, n)),
            pl.BlockSpec((1, D_MODEL), lambda i, n: (0, 0)),
        ],
        out_specs=pl.BlockSpec((tm, D_MODEL), lambda i, n: (i, 0)),
        out_shape=jax.ShapeDtypeStruct((SEQ, D_MODEL), F32),
        compiler_params=pltpu.CompilerParams(
            dimension_semantics=("arbitrary", "arbitrary"),
            vmem_limit_bytes=FFN_DOWN_VMEM_LIMIT),
        name="ffn_down",
    )(act, wd_bf16, x1, g)


def kernel(x, norm_mix_g, w_in, a_ln_g, a_ln_b, a_w_s, a_b_s, s5_lambda_re, s5_lambda_im,
           s5_log_dt, s5_b_re, s5_b_im, s5_c_re, s5_c_im, s5_d, s5_w_glu, out_norm_a_g,
           out_norm_b_g, w_out, norm_ffn_g, w_gate, w_up, w_down, final_norm_g):
    assert x.shape == (1, SEQ, D_MODEL) and norm_mix_g.shape[0] == 1
    xs = x.reshape(SEQ, D_MODEL)
    l = 0

    bias_full = jnp.repeat(jnp.transpose(a_b_s[l]), A_HEAD_DIM, axis=1)
    wb, wc, wglu, a_re, a_im = _s5_params(
        s5_lambda_re[l], s5_lambda_im[l], s5_log_dt[l], s5_b_re[l], s5_b_im[l],
        s5_c_re[l], s5_c_im[l], s5_w_glu[l])
    mix_a, mix_b, w_out_bf16 = _mixer(
        xs, norm_mix_g[l][None], w_in[l], a_ln_g[l][None], a_ln_b[l][None],
        a_w_s[l], bias_full, out_norm_a_g[l][None], w_out[l],
        wb, wc, wglu, a_re, a_im, s5_d[l][None], out_norm_b_g[l][None])

    x1, hn = _outproj(xs, mix_a, mix_b, w_out_bf16, norm_ffn_g[l][None])

    act, w_down_bf16 = _ffn_up(hn, w_gate[l], w_up[l], w_down[l])
    out = _ffn_down(act, w_down_bf16, x1, final_norm_g[None])
    return out.reshape(1, SEQ, D_MODEL)
```

```python
import math

import jax
import jax.numpy as jnp
from jax import lax
from jax.experimental import pallas as pl
from jax.experimental.pallas import tpu as pltpu

F32 = jnp.float32
BF16 = jnp.bfloat16

D_MODEL = 2048
SEQ = 8192
CHUNK = 64
A_WIDTH = 1024
A_HEADS = 8
A_HEAD_DIM = 128
GMLP_BLOCK = 128
B_WIDTH = 1024
S5_GROUP_CH = 16
S5_GROUPS = 64
S5_STATE = 64
IN_WIDTH = 3072
D_FF = 5632
EPS = 1e-6

LANES = 128
SUBLANES = 8
MXU_COLS = 256
VMEM_LIMIT = 56 * 1024 * 1024
FFN_DOWN_VMEM_LIMIT = 60 * 1024 * 1024
MIXER_VMEM_LIMIT = 60 * 1024 * 1024
W_IN_STAGE_ROWS = 256

S5_LANE_TILES = B_WIDTH // LANES
S5_GROUPS_PER_TILE = LANES // S5_GROUP_CH
S5_STATE_TILES = S5_GROUPS * S5_STATE // LANES
S5_TILES_PER_LANE_TILE = S5_STATE_TILES // S5_LANE_TILES
S5_STATE_VREGS = S5_STATE_TILES // SUBLANES
S5_T = 512
S5_HALVES = 2
S5_PITCH = S5_T + 4


def _gelu(x):
    return 0.5 * x * (1.0 + lax.erf(x * (1.0 / math.sqrt(2.0))))


def _rms_scale(x):
    return lax.rsqrt(jnp.mean(x * x, axis=-1, keepdims=True) + EPS)


def _layernorm_bf16(z, g_ref, b_ref):
    mu = jnp.mean(z, axis=-1, keepdims=True)
    zc = z - mu
    var = jnp.mean(zc * zc, axis=-1, keepdims=True)
    return (zc * lax.rsqrt(var + EPS) * g_ref[...] + b_ref[...]).astype(BF16)


def _spatial_mix(v, ws_ref):
    n_blocks = v.shape[0] // GMLP_BLOCK
    ci = lax.broadcasted_iota(jnp.int32, (GMLP_BLOCK, GMLP_BLOCK), 0) // CHUNK
    cj = lax.broadcasted_iota(jnp.int32, (GMLP_BLOCK, GMLP_BLOCK), 1) // CHUNK
    mask = ci >= cj
    heads = []
    for h in range(A_HEADS):
        w = jnp.where(mask, ws_ref[h], 0.0).astype(BF16)
        cols = slice(h * A_HEAD_DIM, (h + 1) * A_HEAD_DIM)
        rhs = jnp.concatenate(
            [v[n * GMLP_BLOCK:(n + 1) * GMLP_BLOCK, cols] for n in range(n_blocks)], axis=1)
        heads.append(jnp.dot(w, rhs, preferred_element_type=F32))
    return jnp.concatenate(
        [jnp.concatenate([hd[:, n * A_HEAD_DIM:(n + 1) * A_HEAD_DIM] for hd in heads], axis=1)
         for n in range(n_blocks)], axis=0)


def _mixer_kernel(x_ref, g_ref, w_hbm, lng_ref, lnb_ref, ws_ref, bias_ref, oga_ref, wo_ref,
                  wb_ref, wc_ref, wglu_ref, are_ref, aim_ref, d_ref, ogb_ref,
                  mixa_ref, mixb_ref, wob_ref, st_ref, bu_ref, w_ref, stage_ref, stage_sem):
    t_rows = x_ref.shape[0]
    tile = lambda k: slice(k * LANES, (k + 1) * LANES)
    half_lane_tiles = S5_LANE_TILES // S5_HALVES
    half_state_tiles = S5_STATE_TILES // S5_HALVES
    half_vregs = S5_STATE_VREGS // S5_HALVES
    im_base = half_state_tiles * S5_PITCH

    def scratch_rows(local_tile):
        return slice(local_tile * S5_PITCH, local_tile * S5_PITCH + t_rows)

    def half_state_tiles_of(kl):
        re = [kl * S5_TILES_PER_LANE_TILE + c for c in range(S5_TILES_PER_LANE_TILE)]
        return re + [half_state_tiles + q for q in re]

    @pl.when(pl.program_id(0) == 0)
    def _():
        st_ref[...] = jnp.zeros_like(st_ref)
        chunk = stage_ref.shape[1]
        n_chunks = w_ref.shape[0] // chunk

        def chunk_copy(c):
            return pltpu.make_async_copy(w_hbm.at[c * chunk:(c + 1) * chunk, :],
                                         stage_ref.at[c % 2], stage_sem.at[c % 2])

        chunk_copy(0).start()
        for c in range(n_chunks):
            if c + 1 < n_chunks:
                chunk_copy(c + 1).start()
            chunk_copy(c).wait()
            w_ref[c * chunk:(c + 1) * chunk, :] = stage_ref[c % 2].astype(BF16)

    wob_ref[...] = wo_ref[...].astype(BF16)

    x = x_ref[...]
    xn = (x * _rms_scale(x) * g_ref[...]).astype(BF16)
    u = jnp.dot(xn, w_ref[:, 2 * A_WIDTH:], preferred_element_type=F32)

    h_pre, yc, ys, gates = [], [], [], []
    for hf in range(S5_HALVES):
        for kl in range(half_lane_tiles):
            k = hf * half_lane_tiles + kl
            res = jnp.dot(u[:, tile(k)].astype(BF16), wb_ref[k],
                          preferred_element_type=F32)
            for c, q in enumerate(half_state_tiles_of(kl)):
                bu_ref[scratch_rows(q), :] = res[:, tile(c)]

        gates += [jnp.dot(ys[k].astype(BF16), wglu_ref[k], preferred_element_type=F32)
                  for k in range(len(gates), len(ys))]

        h_pre.append(jnp.dot(xn, w_ref[:, hf * A_WIDTH:(hf + 1) * A_WIDTH],
                             preferred_element_type=F32))

        vregs = [hf * half_vregs + m for m in range(half_vregs)]
        a_re = [are_ref[gm] for gm in vregs]
        a_im = [aim_ref[gm] for gm in vregs]
        s_re = [st_ref[gm] for gm in vregs]
        s_im = [st_ref[S5_STATE_VREGS + gm] for gm in vregs]
        for t in range(t_rows):
            for m in range(half_vregs):
                re_rows = pl.ds(t + SUBLANES * m * S5_PITCH, SUBLANES, stride=S5_PITCH)
                im_rows = pl.ds(t + im_base + SUBLANES * m * S5_PITCH, SUBLANES, stride=S5_PITCH)
                nr = a_re[m] * s_re[m] - a_im[m] * s_im[m] + bu_ref[re_rows, :]
                ni = a_re[m] * s_im[m] + a_im[m] * s_re[m] + bu_ref[im_rows, :]
                bu_ref[re_rows, :] = nr
                bu_ref[im_rows, :] = ni
                s_re[m], s_im[m] = nr, ni
        for m, gm in enumerate(vregs):
            st_ref[gm] = s_re[m]
            st_ref[S5_STATE_VREGS + gm] = s_im[m]

        for kl in range(half_lane_tiles):
            k = hf * half_lane_tiles + kl
            lhs = jnp.concatenate([bu_ref[scratch_rows(q), :] for q in half_state_tiles_of(kl)],
                                  axis=1).astype(BF16)
            yc.append(jnp.dot(lhs, wc_ref[k], preferred_element_type=F32))

        ys += [_gelu(yc[k] + d_ref[:, tile(k)] * u[:, tile(k)]) for k in range(len(ys), len(yc))]
        if hf == 0:
            zu = _gelu(h_pre[0])

    gates += [jnp.dot(ys[k].astype(BF16), wglu_ref[k], preferred_element_type=F32)
              for k in range(len(gates), len(ys))]
    yb = jnp.concatenate([ys[k] * jax.nn.sigmoid(gates[k]) for k in range(S5_LANE_TILES)], axis=1)
    mixb_ref[...] = (yb * _rms_scale(yb) * ogb_ref[...]).astype(BF16)

    v = _layernorm_bf16(_gelu(h_pre[1]), lng_ref, lnb_ref)
    bias = jnp.concatenate([bias_ref[...]] * (t_rows // GMLP_BLOCK), axis=0)
    ya = zu * (_spatial_mix(v, ws_ref) + bias)
    mixa_ref[...] = (ya * _rms_scale(ya) * oga_ref[...]).astype(BF16)


def _mixer(x, g, w_in, ln_g, ln_b, w_s, bias_full, out_a_g, w_out,
           wb, wc, wglu, a_re, a_im, d, out_b_g):
    tm = S5_T
    n_steps = SEQ // tm
    wo_rows = w_out.shape[0] // n_steps
    n_state_cols = 2 * S5_GROUPS_PER_TILE * S5_STATE
    const2 = lambda i: (0, 0)
    const3 = lambda i: (0, 0, 0)
    once = dict(pipeline_mode=pl.Buffered(1))
    return pl.pallas_call(
        _mixer_kernel,
        grid=(n_steps,),
        in_specs=[
            pl.BlockSpec((tm, D_MODEL), lambda i: (i, 0)),
            pl.BlockSpec((1, D_MODEL), const2),
            pl.BlockSpec(memory_space=pl.ANY),
            pl.BlockSpec((1, A_WIDTH), const2),
            pl.BlockSpec((1, A_WIDTH), const2),
            pl.BlockSpec((A_HEADS, GMLP_BLOCK, GMLP_BLOCK), const3),
            pl.BlockSpec((GMLP_BLOCK, A_WIDTH), const2),
            pl.BlockSpec((1, A_WIDTH), const2),
            pl.BlockSpec((wo_rows, D_MODEL), lambda i: (i, 0)),
            pl.BlockSpec((S5_LANE_TILES, LANES, n_state_cols), const3, **once),
            pl.BlockSpec((S5_LANE_TILES, n_state_cols, LANES), const3, **once),
            pl.BlockSpec((S5_LANE_TILES, LANES, LANES), const3),
            pl.BlockSpec((S5_STATE_VREGS, SUBLANES, LANES), const3),
            pl.BlockSpec((S5_STATE_VREGS, SUBLANES, LANES), const3),
            pl.BlockSpec((1, B_WIDTH), const2),
            pl.BlockSpec((1, B_WIDTH), const2),
        ],
        out_specs=[
            pl.BlockSpec((tm, A_WIDTH), lambda i: (i, 0)),
            pl.BlockSpec((tm, B_WIDTH), lambda i: (i, 0)),
            pl.BlockSpec((wo_rows, D_MODEL), lambda i: (i, 0)),
        ],
        out_shape=[jax.ShapeDtypeStruct((SEQ, A_WIDTH), BF16),
                   jax.ShapeDtypeStruct((SEQ, B_WIDTH), BF16),
                   jax.ShapeDtypeStruct(w_out.shape, BF16)],
        scratch_shapes=[
            pltpu.VMEM((2 * S5_STATE_VREGS, SUBLANES, LANES), F32),
            pltpu.VMEM((2 * S5_STATE_TILES // S5_HALVES * S5_PITCH, LANES), F32),
            pltpu.VMEM((D_MODEL, IN_WIDTH), BF16),
            pltpu.VMEM((2, W_IN_STAGE_ROWS, IN_WIDTH), F32),
            pltpu.SemaphoreType.DMA((2,)),
        ],
        compiler_params=pltpu.CompilerParams(
            dimension_semantics=("arbitrary",), vmem_limit_bytes=MIXER_VMEM_LIMIT),
        name="mixer",
    )(x, g, w_in, ln_g, ln_b, w_s, bias_full, out_a_g, w_out,
      wb, wc, wglu, a_re, a_im, d, out_b_g)


def _repeat_cols_onehot(n_rows, n_cols):
    r = lax.broadcasted_iota(jnp.int32, (n_rows, n_cols), 0)
    c = lax.broadcasted_iota(jnp.int32, (n_rows, n_cols), 1)
    return jnp.where(r == c % n_rows, 1.0, 0.0).astype(BF16)


def _block_diag_mask(n_rows, n_cols, row_block, col_block):
    r = lax.broadcasted_iota(jnp.int32, (n_rows, n_cols), 0)
    c = lax.broadcasted_iota(jnp.int32, (n_rows, n_cols), 1)
    return r // row_block == c // col_block


def _s5_params_kernel(lre_ref, lim_ref, ldt_ref, bre_ref, bim_ref, cre_ref, cim_ref, wg_ref,
                      abr_ref, abi_ref, wb_ref, wc_ref, wglu_ref):
    lr, li = lre_ref[...], lim_ref[...]
    dt = jnp.exp(ldt_ref[...])
    mag = jnp.exp(lr * dt)
    ab_r = mag * jnp.cos(li * dt)
    ab_i = mag * jnp.sin(li * dt)
    den = lr * lr + li * li
    nr = ab_r - 1.0
    co_r = (nr * lr + ab_i * li) / den
    co_i = (ab_i * lr - nr * li) / den
    abr_ref[...] = ab_r
    abi_ref[...] = ab_i

    gt, h, p = S5_GROUPS_PER_TILE, S5_GROUP_CH, S5_STATE
    spread_p = _repeat_cols_onehot(p, gt * p)
    spread_h = _repeat_cols_onehot(h, gt * h)
    eye_gh = _repeat_cols_onehot(gt * h, gt * h)
    mask_b = _block_diag_mask(gt * h, gt * p, h, p)
    mask_c = _block_diag_mask(gt * p, gt * h, p, h)
    mask_g = _block_diag_mask(gt * h, gt * h, h, h)
    for k in range(S5_LANE_TILES):
        grp = slice(k * gt, (k + 1) * gt)
        cr = co_r[grp][:, None, :]
        ci = co_i[grp][:, None, :]
        br, bi = bre_ref[grp], bim_ref[grp]
        bb_r = (cr * br - ci * bi).reshape(gt * h, p).astype(BF16)
        bb_i = (cr * bi + ci * br).reshape(gt * h, p).astype(BF16)
        for half, bb in enumerate((bb_r, bb_i)):
            full = jnp.dot(bb, spread_p, preferred_element_type=F32)
            wb_ref[k, :, half * gt * p:(half + 1) * gt * p] = (
                jnp.where(mask_b, full, 0.0).astype(BF16))
        for half, (c_ref, sign) in enumerate(((cre_ref, 1.0), (cim_ref, -1.0))):
            c_k = (sign * c_ref[grp]).reshape(gt * h, p).astype(BF16)
            c_t = lax.dot_general(c_k, eye_gh, (((0,), (0,)), ((), ())),
                                  preferred_element_type=F32)
            full = jnp.concatenate([c_t] * gt, axis=0)
            wc_ref[k, half * gt * p:(half + 1) * gt * p, :] = (
                jnp.where(mask_c, full, 0.0).astype(BF16))
        g_k = wg_ref[grp].reshape(gt * h, h).astype(BF16)
        full = jnp.dot(g_k, spread_h, preferred_element_type=F32)
        wglu_ref[k] = jnp.where(mask_g, full, 0.0).astype(BF16)


def _s5_params(lam_re, lam_im, log_dt, b_re, b_im, c_re, c_im, w_glu):
    gt, h, p = S5_GROUPS_PER_TILE, S5_GROUP_CH, S5_STATE
    g_p = jax.ShapeDtypeStruct((S5_GROUPS, p), F32)
    ab_r, ab_i, wb, wc, wglu = pl.pallas_call(
        _s5_params_kernel,
        out_shape=[g_p, g_p,
                   jax.ShapeDtypeStruct((S5_LANE_TILES, gt * h, 2 * gt * p), BF16),
                   jax.ShapeDtypeStruct((S5_LANE_TILES, 2 * gt * p, gt * h), BF16),
                   jax.ShapeDtypeStruct((S5_LANE_TILES, gt * h, gt * h), BF16)],
        name="s5_params",
    )(lam_re, lam_im, log_dt[:, None], jnp.swapaxes(b_re, 1, 2), jnp.swapaxes(b_im, 1, 2),
      c_re, c_im, w_glu)
    a_re = ab_r.reshape(S5_STATE_VREGS, SUBLANES, LANES)
    a_im = ab_i.reshape(S5_STATE_VREGS, SUBLANES, LANES)
    return wb, wc, wglu, a_re, a_im


def _outproj_kernel(x_ref, ma_ref, mb_ref, wa_ref, wb_ref, g_ref, x1_ref, hn_ref):
    half = x_ref.shape[0] // 2
    for r in range(2):
        rows = slice(r * half, (r + 1) * half)
        acc = jnp.dot(ma_ref[rows, :], wa_ref[...], preferred_element_type=F32)
        acc = acc + jnp.dot(mb_ref[rows, :], wb_ref[...], preferred_element_type=F32)
        x1 = x_ref[rows, :] + acc
        x1_ref[rows, :] = x1
        hn_ref[rows, :] = (x1 * _rms_scale(x1) * g_ref[...]).astype(BF16)


def _outproj(x, mix_a, mix_b, w_out_bf16, g, tm=512):
    return pl.pallas_call(
        _outproj_kernel,
        grid=(SEQ // tm,),
        in_specs=[
            pl.BlockSpec((tm, D_MODEL), lambda i: (i, 0)),
            pl.BlockSpec((tm, A_WIDTH), lambda i: (i, 0)),
            pl.BlockSpec((tm, B_WIDTH), lambda i: (i, 0)),
            pl.BlockSpec((A_WIDTH, D_MODEL), lambda i: (0, 0)),
            pl.BlockSpec((B_WIDTH, D_MODEL), lambda i: (1, 0)),
            pl.BlockSpec((1, D_MODEL), lambda i: (0, 0)),
        ],
        out_specs=[
            pl.BlockSpec((tm, D_MODEL), lambda i: (i, 0)),
            pl.BlockSpec((tm, D_MODEL), lambda i: (i, 0)),
        ],
        out_shape=[jax.ShapeDtypeStruct((SEQ, D_MODEL), F32),
                   jax.ShapeDtypeStruct((SEQ, D_MODEL), BF16)],
        compiler_params=pltpu.CompilerParams(
            dimension_semantics=("arbitrary",), vmem_limit_bytes=VMEM_LIMIT),
        name="outproj",
    )(x, mix_a, mix_b, w_out_bf16, w_out_bf16, g)


def _ffn_up_kernel(hn_ref, wg_ref, wu_ref, wd_ref, act_ref, wdb_ref):
    wdb_ref[...] = wd_ref[...].astype(BF16)
    hn = hn_ref[...]
    for c in range(act_ref.shape[1] // MXU_COLS):
        cols = slice(c * MXU_COLS, (c + 1) * MXU_COLS)
        gate = jnp.dot(hn, wg_ref[:, cols].astype(BF16), preferred_element_type=F32)
        up = jnp.dot(hn, wu_ref[:, cols].astype(BF16), preferred_element_type=F32)
        act_ref[:, cols] = (gate * jax.nn.sigmoid(gate) * up).astype(BF16)


def _ffn_up(hn, wg, wu, wd, tm=2048, tf=512):
    n_i = SEQ // tm
    wd_rows = wd.shape[0] // (D_FF // tf * n_i)
    wd_slab = pl.BlockSpec((wd_rows, D_MODEL), lambda f, i: (f * n_i + i, 0))
    ahead = dict(pipeline_mode=pl.Buffered(2, use_lookahead=True))
    pipeline = pltpu.emit_pipeline(
        _ffn_up_kernel,
        grid=(D_FF // tf, n_i),
        in_specs=[
            pl.BlockSpec((tm, D_MODEL), lambda f, i: (i, 0)),
            pl.BlockSpec((D_MODEL, tf), lambda f, i: (0, f), **ahead),
            pl.BlockSpec((D_MODEL, tf), lambda f, i: (0, f), **ahead),
            wd_slab,
        ],
        out_specs=[pl.BlockSpec((tm, tf), lambda f, i: (i, f)), wd_slab],
    )
    hbm = pl.BlockSpec(memory_space=pl.ANY)
    return pl.pallas_call(
        lambda *refs: pipeline(*refs),
        in_specs=[hbm] * 4,
        out_specs=[hbm] * 2,
        out_shape=[jax.ShapeDtypeStruct((SEQ, D_FF), BF16),
                   jax.ShapeDtypeStruct(wd.shape, BF16)],
        compiler_params=pltpu.CompilerParams(vmem_limit_bytes=VMEM_LIMIT),
        name="ffn_up",
    )(hn, wg, wu, wd)


def _ffn_down_kernel(act_ref, wd_ref, x1_ref, g_ref, o_ref):
    n_tiles = pl.num_programs(1)
    tn = wd_ref.shape[1]
    for n in range(o_ref.shape[1] // tn):
        @pl.when(pl.program_id(1) == n)
        def _(n=n):
            o_ref[:, n * tn:(n + 1) * tn] = x1_ref[...] + jnp.dot(
                act_ref[...], wd_ref[...], preferred_element_type=F32)

    @pl.when(pl.program_id(1) == n_tiles - 1)
    def _():
        x2 = o_ref[...]
        o_ref[...] = x2 * _rms_scale(x2) * g_ref[...]


def _ffn_down(act, wd_bf16, x1, g, tm=1024, tn=512):
    return pl.pallas_call(
        _ffn_down_kernel,
        grid=(SEQ // tm, D_MODEL // tn),
        in_specs=[
            pl.BlockSpec((tm, D_FF), lambda i, n: (i, 0)),
            pl.BlockSpec((D_FF, tn), lambda i, n: (0, n)),
            pl.BlockSpec((tm, tn), lambda i, n: (i, n)),
            pl.BlockSpec((1, D_MODEL), lambda i, n: (0, 0)),
        ],
        out_specs=pl.BlockSpec((tm, D_MODEL), lambda i, n: (i, 0)),
        out_shape=jax.ShapeDtypeStruct((SEQ, D_MODEL), F32),
        compiler_params=pltpu.CompilerParams(
            dimension_semantics=("arbitrary", "arbitrary"),
            vmem_limit_bytes=FFN_DOWN_VMEM_LIMIT),
        name="ffn_down",
    )(act, wd_bf16, x1, g)


def kernel(x, norm_mix_g, w_in, a_ln_g, a_ln_b, a_w_s, a_b_s, s5_lambda_re, s5_lambda_im,
           s5_log_dt, s5_b_re, s5_b_im, s5_c_re, s5_c_im, s5_d, s5_w_glu, out_norm_a_g,
           out_norm_b_g, w_out, norm_ffn_g, w_gate, w_up, w_down, final_norm_g):
    assert x.shape == (1, SEQ, D_MODEL) and norm_mix_g.shape[0] == 1
    xs = x.reshape(SEQ, D_MODEL)
    l = 0

    bias_full = jnp.repeat(jnp.transpose(a_b_s[l]), A_HEAD_DIM, axis=1)
    wb, wc, wglu, a_re, a_im = _s5_params(
        s5_lambda_re[l], s5_lambda_im[l], s5_log_dt[l], s5_b_re[l], s5_b_im[l],
        s5_c_re[l], s5_c_im[l], s5_w_glu[l])
    mix_a, mix_b, w_out_bf16 = _mixer(
        xs, norm_mix_g[l][None], w_in[l], a_ln_g[l][None], a_ln_b[l][None],
        a_w_s[l], bias_full, out_norm_a_g[l][None], w_out[l],
        wb, wc, wglu, a_re, a_im, s5_d[l][None], out_norm_b_g[l][None])

    x1, hn = _outproj(xs, mix_a, mix_b, w_out_bf16, norm_ffn_g[l][None])

    act, w_down_bf16 = _ffn_up(hn, w_gate[l], w_up[l], w_down[l])
    out = _ffn_down(act, w_down_bf16, x1, final_norm_g[None])
    return out.reshape(1, SEQ, D_MODEL)
```

```python
import math

import jax
import jax.numpy as jnp
from jax import lax
from jax.experimental import pallas as pl
from jax.experimental.pallas import tpu as pltpu

F32 = jnp.float32
BF16 = jnp.bfloat16

D_MODEL = 2048
SEQ = 8192
CHUNK = 64
A_WIDTH = 1024
A_HEADS = 8
A_HEAD_DIM = 128
GMLP_BLOCK = 128
B_WIDTH = 1024
S5_GROUP_CH = 16
S5_GROUPS = 64
S5_STATE = 64
IN_WIDTH = 3072
D_FF = 5632
EPS = 1e-6

LANES = 128
SUBLANES = 8
MXU_COLS = 256
VMEM_LIMIT = 56 * 1024 * 1024
FFN_DOWN_VMEM_LIMIT = 60 * 1024 * 1024
MIXER_VMEM_LIMIT = 60 * 1024 * 1024
W_IN_STAGE_ROWS = 256

S5_LANE_TILES = B_WIDTH // LANES
S5_GROUPS_PER_TILE = LANES // S5_GROUP_CH
S5_STATE_TILES = S5_GROUPS * S5_STATE // LANES
S5_TILES_PER_LANE_TILE = S5_STATE_TILES // S5_LANE_TILES
S5_STATE_VREGS = S5_STATE_TILES // SUBLANES
S5_T = 512
S5_HALVES = 2
S5_PITCH = S5_T + 4


def _gelu(x):
    return 0.5 * x * (1.0 + lax.erf(x * (1.0 / math.sqrt(2.0))))


def _rms_scale(x):
    return lax.rsqrt(jnp.mean(x * x, axis=-1, keepdims=True) + EPS)


def _layernorm_bf16(z, g_ref, b_ref):
    mu = jnp.mean(z, axis=-1, keepdims=True)
    zc = z - mu
    var = jnp.mean(zc * zc, axis=-1, keepdims=True)
    return (zc * lax.rsqrt(var + EPS) * g_ref[...] + b_ref[...]).astype(BF16)


def _spatial_mix(v, ws_ref):
    n_blocks = v.shape[0] // GMLP_BLOCK
    ci = lax.broadcasted_iota(jnp.int32, (GMLP_BLOCK, GMLP_BLOCK), 0) // CHUNK
    cj = lax.broadcasted_iota(jnp.int32, (GMLP_BLOCK, GMLP_BLOCK), 1) // CHUNK
    mask = ci >= cj
    heads = []
    for h in range(A_HEADS):
        w = jnp.where(mask, ws_ref[h], 0.0).astype(BF16)
        cols = slice(h * A_HEAD_DIM, (h + 1) * A_HEAD_DIM)
        rhs = jnp.concatenate(
            [v[n * GMLP_BLOCK:(n + 1) * GMLP_BLOCK, cols] for n in range(n_blocks)], axis=1)
        heads.append(jnp.dot(w, rhs, preferred_element_type=F32))
    return jnp.concatenate(
        [jnp.concatenate([hd[:, n * A_HEAD_DIM:(n + 1) * A_HEAD_DIM] for hd in heads], axis=1)
         for n in range(n_blocks)], axis=0)


def _mixer_kernel(x_ref, g_ref, w_hbm, lng_ref, lnb_ref, ws_ref, bias_ref, oga_ref, wo_ref,
                  wb_ref, wc_ref, wglu_ref, are_ref, aim_ref, d_ref, ogb_ref,
                  mixa_ref, mixb_ref, wob_ref, st_ref, bu_ref, w_ref, stage_ref, stage_sem):
    t_rows = x_ref.shape[0]
    tile = lambda k: slice(k * LANES, (k + 1) * LANES)
    half_lane_tiles = S5_LANE_TILES // S5_HALVES
    half_state_tiles = S5_STATE_TILES // S5_HALVES
    half_vregs = S5_STATE_VREGS // S5_HALVES
    im_base = half_state_tiles * S5_PITCH

    def scratch_rows(local_tile):
        return slice(local_tile * S5_PITCH, local_tile * S5_PITCH + t_rows)

    def half_state_tiles_of(kl):
        re = [kl * S5_TILES_PER_LANE_TILE + c for c in range(S5_TILES_PER_LANE_TILE)]
        return re + [half_state_tiles + q for q in re]

    @pl.when(pl.program_id(0) == 0)
    def _():
        st_ref[...] = jnp.zeros_like(st_ref)
        chunk = stage_ref.shape[1]
        n_chunks = w_ref.shape[0] // chunk

        def chunk_copy(c):
            return pltpu.make_async_copy(w_hbm.at[c * chunk:(c + 1) * chunk, :],
                                         stage_ref.at[c % 2], stage_sem.at[c % 2])

        chunk_copy(0).start()
        for c in range(n_chunks):
            if c + 1 < n_chunks:
                chunk_copy(c + 1).start()
            chunk_copy(c).wait()
            w_ref[c * chunk:(c + 1) * chunk, :] = stage_ref[c % 2].astype(BF16)

    wob_ref[...] = wo_ref[...].astype(BF16)

    x = x_ref[...]
    xn = (x * _rms_scale(x) * g_ref[...]).astype(BF16)
    u = jnp.dot(xn, w_ref[:, 2 * A_WIDTH:], preferred_element_type=F32)

    h_pre, yc, ys, gates = [], [], [], []
    for hf in range(S5_HALVES):
        for kl in range(half_lane_tiles):
            k = hf * half_lane_tiles + kl
            res = jnp.dot(u[:, tile(k)].astype(BF16), wb_ref[k],
                          preferred_element_type=F32)
            for c, q in enumerate(half_state_tiles_of(kl)):
                bu_ref[scratch_rows(q), :] = res[:, tile(c)]

        gates += [jnp.dot(ys[k].astype(BF16), wglu_ref[k], preferred_element_type=F32)
                  for k in range(len(gates), len(ys))]

        h_pre.append(jnp.dot(xn, w_ref[:, hf * A_WIDTH:(hf + 1) * A_WIDTH],
                             preferred_element_type=F32))

        vregs = [hf * half_vregs + m for m in range(half_vregs)]
        a_re = [are_ref[gm] for gm in vregs]
        a_im = [aim_ref[gm] for gm in vregs]
        s_re = [st_ref[gm] for gm in vregs]
        s_im = [st_ref[S5_STATE_VREGS + gm] for gm in vregs]
        for t in range(t_rows):
            for m in range(half_vregs):
                re_rows = pl.ds(t + SUBLANES * m * S5_PITCH, SUBLANES, stride=S5_PITCH)
                im_rows = pl.ds(t + im_base + SUBLANES * m * S5_PITCH, SUBLANES, stride=S5_PITCH)
                nr = a_re[m] * s_re[m] - a_im[m] * s_im[m] + bu_ref[re_rows, :]
                ni = a_re[m] * s_im[m] + a_im[m] * s_re[m] + bu_ref[im_rows, :]
                bu_ref[re_rows, :] = nr
                bu_ref[im_rows, :] = ni
                s_re[m], s_im[m] = nr, ni
        for m, gm in enumerate(vregs):
            st_ref[gm] = s_re[m]
            st_ref[S5_STATE_VREGS + gm] = s_im[m]

        for kl in range(half_lane_tiles):
            k = hf * half_lane_tiles + kl
            lhs = jnp.concatenate([bu_ref[scratch_rows(q), :] for q in half_state_tiles_of(kl)],
                                  axis=1).astype(BF16)
            yc.append(jnp.dot(lhs, wc_ref[k], preferred_element_type=F32))

        ys += [_gelu(yc[k] + d_ref[:, tile(k)] * u[:, tile(k)]) for k in range(len(ys), len(yc))]
        if hf == 0:
            zu = _gelu(h_pre[0])

    gates += [jnp.dot(ys[k].astype(BF16), wglu_ref[k], preferred_element_type=F32)
              for k in range(len(gates), len(ys))]
    yb = jnp.concatenate([ys[k] * jax.nn.sigmoid(gates[k]) for k in range(S5_LANE_TILES)], axis=1)
    mixb_ref[...] = (yb * _rms_scale(yb) * ogb_ref[...]).astype(BF16)

    v = _layernorm_bf16(_gelu(h_pre[1]), lng_ref, lnb_ref)
    bias = jnp.concatenate([bias_ref[...]] * (t_rows // GMLP_BLOCK), axis=0)
    ya = zu * (_spatial_mix(v, ws_ref) + bias)
    mixa_ref[...] = (ya * _rms_scale(ya) * oga_ref[...]).astype(BF16)


def _mixer(x, g, w_in, ln_g, ln_b, w_s, bias_full, out_a_g, w_out,
           wb, wc, wglu, a_re, a_im, d, out_b_g):
    tm = S5_T
    n_steps = SEQ // tm
    wo_rows = w_out.shape[0] // n_steps
    n_state_cols = 2 * S5_GROUPS_PER_TILE * S5_STATE
    const2 = lambda i: (0, 0)
    const3 = lambda i: (0, 0, 0)
    once = dict(pipeline_mode=pl.Buffered(1))
    return pl.pallas_call(
        _mixer_kernel,
        grid=(n_steps,),
        in_specs=[
            pl.BlockSpec((tm, D_MODEL), lambda i: (i, 0)),
            pl.BlockSpec((1, D_MODEL), const2),
            pl.BlockSpec(memory_space=pl.ANY),
            pl.BlockSpec((1, A_WIDTH), const2),
            pl.BlockSpec((1, A_WIDTH), const2),
            pl.BlockSpec((A_HEADS, GMLP_BLOCK, GMLP_BLOCK), const3),
            pl.BlockSpec((GMLP_BLOCK, A_WIDTH), const2),
            pl.BlockSpec((1, A_WIDTH), const2),
            pl.BlockSpec((wo_rows, D_MODEL), lambda i: (i, 0)),
            pl.BlockSpec((S5_LANE_TILES, LANES, n_state_cols), const3, **once),
            pl.BlockSpec((S5_LANE_TILES, n_state_cols, LANES), const3, **once),
            pl.BlockSpec((S5_LANE_TILES, LANES, LANES), const3),
            pl.BlockSpec((S5_STATE_VREGS, SUBLANES, LANES), const3),
            pl.BlockSpec((S5_STATE_VREGS, SUBLANES, LANES), const3),
            pl.BlockSpec((1, B_WIDTH), const2),
            pl.BlockSpec((1, B_WIDTH), const2),
        ],
        out_specs=[
            pl.BlockSpec((tm, A_WIDTH), lambda i: (i, 0)),
            pl.BlockSpec((tm, B_WIDTH), lambda i: (i, 0)),
            pl.BlockSpec((wo_rows, D_MODEL), lambda i: (i, 0)),
        ],
        out_shape=[jax.ShapeDtypeStruct((SEQ, A_WIDTH), BF16),
                   jax.ShapeDtypeStruct((SEQ, B_WIDTH), BF16),
                   jax.ShapeDtypeStruct(w_out.shape, BF16)],
        scratch_shapes=[
            pltpu.VMEM((2 * S5_STATE_VREGS, SUBLANES, LANES), F32),
            pltpu.VMEM((2 * S5_STATE_TILES // S5_HALVES * S5_PITCH, LANES), F32),
            pltpu.VMEM((D_MODEL, IN_WIDTH), BF16),
            pltpu.VMEM((2, W_IN_STAGE_ROWS, IN_WIDTH), F32),
            pltpu.SemaphoreType.DMA((2,)),
        ],
        compiler_params=pltpu.CompilerParams(
            dimension_semantics=("arbitrary",), vmem_limit_bytes=MIXER_VMEM_LIMIT),
        name="mixer",
    )(x, g, w_in, ln_g, ln_b, w_s, bias_full, out_a_g, w_out,
      wb, wc, wglu, a_re, a_im, d, out_b_g)


def _repeat_cols_onehot(n_rows, n_cols):
    r = lax.broadcasted_iota(jnp.int32, (n_rows, n_cols), 0)
    c = lax.broadcasted_iota(jnp.int32, (n_rows, n_cols), 1)
    return jnp.where(r == c % n_rows, 1.0, 0.0).astype(BF16)


def _block_diag_mask(n_rows, n_cols, row_block, col_block):
    r = lax.broadcasted_iota(jnp.int32, (n_rows, n_cols), 0)
    c = lax.broadcasted_iota(jnp.int32, (n_rows, n_cols), 1)
    return r // row_block == c // col_block


def _s5_params_kernel(lre_ref, lim_ref, ldt_ref, bre_ref, bim_ref, cre_ref, cim_ref, wg_ref,
                      abr_ref, abi_ref, wb_ref, wc_ref, wglu_ref):
    lr, li = lre_ref[...], lim_ref[...]
    dt = jnp.exp(ldt_ref[...])
    mag = jnp.exp(lr * dt)
    ab_r = mag * jnp.cos(li * dt)
    ab_i = mag * jnp.sin(li * dt)
    den = lr * lr + li * li
    nr = ab_r - 1.0
    co_r = (nr * lr + ab_i * li) / den
    co_i = (ab_i * lr - nr * li) / den
    abr_ref[...] = ab_r
    abi_ref[...] = ab_i

    gt, h, p = S5_GROUPS_PER_TILE, S5_GROUP_CH, S5_STATE
    spread_p = _repeat_cols_onehot(p, gt * p)
    spread_h = _repeat_cols_onehot(h, gt * h)
    eye_gh = _repeat_cols_onehot(gt * h, gt * h)
    mask_b = _block_diag_mask(gt * h, gt * p, h, p)
    mask_c = _block_diag_mask(gt * p, gt * h, p, h)
    mask_g = _block_diag_mask(gt * h, gt * h, h, h)
    for k in range(S5_LANE_TILES):
        grp = slice(k * gt, (k + 1) * gt)
        cr = co_r[grp][:, None, :]
        ci = co_i[grp][:, None, :]
        br, bi = bre_ref[grp], bim_ref[grp]
        bb_r = (cr * br - ci * bi).reshape(gt * h, p).astype(BF16)
        bb_i = (cr * bi + ci * br).reshape(gt * h, p).astype(BF16)
        for half, bb in enumerate((bb_r, bb_i)):
            full = jnp.dot(bb, spread_p, preferred_element_type=F32)
            wb_ref[k, :, half * gt * p:(half + 1) * gt * p] = (
                jnp.where(mask_b, full, 0.0).astype(BF16))
        for half, (c_ref, sign) in enumerate(((cre_ref, 1.0), (cim_ref, -1.0))):
            c_k = (sign * c_ref[grp]).reshape(gt * h, p).astype(BF16)
            c_t = lax.dot_general(c_k, eye_gh, (((0,), (0,)), ((), ())),
                                  preferred_element_type=F32)
            full = jnp.concatenate([c_t] * gt, axis=0)
            wc_ref[k, half * gt * p:(half + 1) * gt * p, :] = (
                jnp.where(mask_c, full, 0.0).astype(BF16))
        g_k = wg_ref[grp].reshape(gt * h, h).astype(BF16)
        full = jnp.dot(g_k, spread_h, preferred_element_type=F32)
        wglu_ref[k] = jnp.where(mask_g, full, 0.0).astype(BF16)


def _s5_params(lam_re, lam_im, log_dt, b_re, b_im, c_re, c_im, w_glu):
    gt, h, p = S5_GROUPS_PER_TILE, S5_GROUP_CH, S5_STATE
    g_p = jax.ShapeDtypeStruct((S5_GROUPS, p), F32)
    ab_r, ab_i, wb, wc, wglu = pl.pallas_call(
        _s5_params_kernel,
        out_shape=[g_p, g_p,
                   jax.ShapeDtypeStruct((S5_LANE_TILES, gt * h, 2 * gt * p), BF16),
                   jax.ShapeDtypeStruct((S5_LANE_TILES, 2 * gt * p, gt * h), BF16),
                   jax.ShapeDtypeStruct((S5_LANE_TILES, gt * h, gt * h), BF16)],
        name="s5_params",
    )(lam_re, lam_im, log_dt[:, None], jnp.swapaxes(b_re, 1, 2), jnp.swapaxes(b_im, 1, 2),
      c_re, c_im, w_glu)
    a_re = ab_r.reshape(S5_STATE_VREGS, SUBLANES, LANES)
    a_im = ab_i.reshape(S5_STATE_VREGS, SUBLANES, LANES)
    return wb, wc, wglu, a_re, a_im


def _outproj_kernel(x_ref, ma_ref, mb_ref, wa_ref, wb_ref, g_ref, x1_ref, hn_ref):
    half = x_ref.shape[0] // 2
    for r in range(2):
        rows = slice(r * half, (r + 1) * half)
        acc = jnp.dot(ma_ref[rows, :], wa_ref[...], preferred_element_type=F32)
        acc = acc + jnp.dot(mb_ref[rows, :], wb_ref[...], preferred_element_type=F32)
        x1 = x_ref[rows, :] + acc
        x1_ref[rows, :] = x1
        hn_ref[rows, :] = (x1 * _rms_scale(x1) * g_ref[...]).astype(BF16)


def _outproj(x, mix_a, mix_b, w_out_bf16, g, tm=512):
    deep = dict(pipeline_mode=pl.Buffered(3))
    pipeline = pltpu.emit_pipeline(
        _outproj_kernel,
        grid=(SEQ // tm,),
        in_specs=[
            pl.BlockSpec((tm, D_MODEL), lambda i: (i, 0), **deep),
            pl.BlockSpec((tm, A_WIDTH), lambda i: (i, 0), **deep),
            pl.BlockSpec((tm, B_WIDTH), lambda i: (i, 0), **deep),
            pl.BlockSpec((A_WIDTH, D_MODEL), lambda i: (0, 0)),
            pl.BlockSpec((B_WIDTH, D_MODEL), lambda i: (1, 0)),
            pl.BlockSpec((1, D_MODEL), lambda i: (0, 0)),
        ],
        out_specs=[
            pl.BlockSpec((tm, D_MODEL), lambda i: (i, 0)),
            pl.BlockSpec((tm, D_MODEL), lambda i: (i, 0)),
        ],
    )
    hbm = pl.BlockSpec(memory_space=pl.ANY)
    return pl.pallas_call(
        lambda *refs: pipeline(*refs),
        in_specs=[hbm] * 6,
        out_specs=[hbm] * 2,
        out_shape=[jax.ShapeDtypeStruct((SEQ, D_MODEL), F32),
                   jax.ShapeDtypeStruct((SEQ, D_MODEL), BF16)],
        compiler_params=pltpu.CompilerParams(vmem_limit_bytes=VMEM_LIMIT),
        name="outproj",
    )(x, mix_a, mix_b, w_out_bf16, w_out_bf16, g)


def _ffn_up_kernel(hn_ref, wg_ref, wu_ref, wd_ref, act_ref, wdb_ref):
    wdb_ref[...] = wd_ref[...].astype(BF16)
    hn = hn_ref[...]
    for c in range(act_ref.shape[1] // MXU_COLS):
        cols = slice(c * MXU_COLS, (c + 1) * MXU_COLS)
        gate = jnp.dot(hn, wg_ref[:, cols].astype(BF16), preferred_element_type=F32)
        up = jnp.dot(hn, wu_ref[:, cols].astype(BF16), preferred_element_type=F32)
        act_ref[:, cols] = (gate * jax.nn.sigmoid(gate) * up).astype(BF16)


def _ffn_up(hn, wg, wu, wd, tm=2048, tf=512):
    n_i = SEQ // tm
    wd_rows = wd.shape[0] // (D_FF // tf * n_i)
    wd_slab = pl.BlockSpec((wd_rows, D_MODEL), lambda f, i: (f * n_i + i, 0))
    return pl.pallas_call(
        _ffn_up_kernel,
        grid=(D_FF // tf, n_i),
        in_specs=[
            pl.BlockSpec((tm, D_MODEL), lambda f, i: (i, 0)),
            pl.BlockSpec((D_MODEL, tf), lambda f, i: (0, f)),
            pl.BlockSpec((D_MODEL, tf), lambda f, i: (0, f)),
            wd_slab,
        ],
        out_specs=[pl.BlockSpec((tm, tf), lambda f, i: (i, f)), wd_slab],
        out_shape=[jax.ShapeDtypeStruct((SEQ, D_FF), BF16),
                   jax.ShapeDtypeStruct(wd.shape, BF16)],
        compiler_params=pltpu.CompilerParams(
            dimension_semantics=("arbitrary", "arbitrary"),
            vmem_limit_bytes=VMEM_LIMIT),
        name="ffn_up",
    )(hn, wg, wu, wd)


def _ffn_down_kernel(act_ref, wd_ref, x1_ref, g_ref, o_ref):
    n_tiles = pl.num_programs(1)
    tn = wd_ref.shape[1]
    for n in range(o_ref.shape[1] // tn):
        @pl.when(pl.program_id(1) == n)
        def _(n=n):
            o_ref[:, n * tn:(n + 1) * tn] = x1_ref[...] + jnp.dot(
                act_ref[...], wd_ref[...], preferred_element_type=F32)

    @pl.when(pl.program_id(1) == n_tiles - 1)
    def _():
        x2 = o_ref[...]
        o_ref[...] = x2 * _rms_scale(x2) * g_ref[...]


def _ffn_down(act, wd_bf16, x1, g, tm=1024, tn=512):
    return pl.pallas_call(
        _ffn_down_kernel,
        grid=(SEQ // tm, D_MODEL // tn),
        in_specs=[
            pl.BlockSpec((tm, D_FF), lambda i, n: (i, 0)),
            pl.BlockSpec((D_FF, tn), lambda i, n: (0, n)),
            pl.BlockSpec((tm, tn), lambda i, n: (i, n)),
            pl.BlockSpec((1, D_MODEL), lambda i, n: (0, 0)),
        ],
        out_specs=pl.BlockSpec((tm, D_MODEL), lambda i, n: (i, 0)),
        out_shape=jax.ShapeDtypeStruct((SEQ, D_MODEL), F32),
        compiler_params=pltpu.CompilerParams(
            dimension_semantics=("arbitrary", "arbitrary"),
            vmem_limit_bytes=FFN_DOWN_VMEM_LIMIT),
        name="ffn_down",
    )(act, wd_bf16, x1, g)


def kernel(x, norm_mix_g, w_in, a_ln_g, a_ln_b, a_w_s, a_b_s, s5_lambda_re, s5_lambda_im,
           s5_log_dt, s5_b_re, s5_b_im, s5_c_re, s5_c_im, s5_d, s5_w_glu, out_norm_a_g,
           out_norm_b_g, w_out, norm_ffn_g, w_gate, w_up, w_down, final_norm_g):
    assert x.shape == (1, SEQ, D_MODEL) and norm_mix_g.shape[0] == 1
    xs = x.reshape(SEQ, D_MODEL)
    l = 0

    bias_full = jnp.repeat(jnp.transpose(a_b_s[l]), A_HEAD_DIM, axis=1)
    wb, wc, wglu, a_re, a_im = _s5_params(
        s5_lambda_re[l], s5_lambda_im[l], s5_log_dt[l], s5_b_re[l], s5_b_im[l],
        s5_c_re[l], s5_c_im[l], s5_w_glu[l])
    mix_a, mix_b, w_out_bf16 = _mixer(
        xs, norm_mix_g[l][None], w_in[l], a_ln_g[l][None], a_ln_b[l][None],
        a_w_s[l], bias_full, out_norm_a_g[l][None], w_out[l],
        wb, wc, wglu, a_re, a_im, s5_d[l][None], out_norm_b_g[l][None])

    x1, hn = _outproj(xs, mix_a, mix_b, w_out_bf16, norm_ffn_g[l][None])

    act, w_down_bf16 = _ffn_up(hn, w_gate[l], w_up[l], w_down[l])
    out = _ffn_down(act, w_down_bf16, x1, final_norm_g[None])
    return out.reshape(1, SEQ, D_MODEL)
```

```python
import math

import jax
import jax.numpy as jnp
from jax import lax
from jax.experimental import pallas as pl
from jax.experimental.pallas import tpu as pltpu

F32 = jnp.float32
BF16 = jnp.bfloat16

D_MODEL = 2048
SEQ = 8192
CHUNK = 64
A_WIDTH = 1024
A_HEADS = 8
A_HEAD_DIM = 128
GMLP_BLOCK = 128
B_WIDTH = 1024
S5_GROUP_CH = 16
S5_GROUPS = 64
S5_STATE = 64
IN_WIDTH = 3072
D_FF = 5632
EPS = 1e-6

LANES = 128
SUBLANES = 8
MXU_COLS = 256
VMEM_LIMIT = 56 * 1024 * 1024
FFN_DOWN_VMEM_LIMIT = 60 * 1024 * 1024
MIXER_VMEM_LIMIT = 60 * 1024 * 1024
W_IN_STAGE_ROWS = 256

S5_LANE_TILES = B_WIDTH // LANES
S5_GROUPS_PER_TILE = LANES // S5_GROUP_CH
S5_STATE_TILES = S5_GROUPS * S5_STATE // LANES
S5_TILES_PER_LANE_TILE = S5_STATE_TILES // S5_LANE_TILES
S5_STATE_VREGS = S5_STATE_TILES // SUBLANES
S5_T = 512
S5_HALVES = 2
S5_PITCH = S5_T + 4


def _gelu(x):
    return 0.5 * x * (1.0 + lax.erf(x * (1.0 / math.sqrt(2.0))))


def _rms_scale(x):
    return lax.rsqrt(jnp.mean(x * x, axis=-1, keepdims=True) + EPS)


def _layernorm_bf16(z, g_ref, b_ref):
    mu = jnp.mean(z, axis=-1, keepdims=True)
    zc = z - mu
    var = jnp.mean(zc * zc, axis=-1, keepdims=True)
    return (zc * lax.rsqrt(var + EPS) * g_ref[...] + b_ref[...]).astype(BF16)


def _spatial_mix(v, ws_ref):
    n_blocks = v.shape[0] // GMLP_BLOCK
    ci = lax.broadcasted_iota(jnp.int32, (GMLP_BLOCK, GMLP_BLOCK), 0) // CHUNK
    cj = lax.broadcasted_iota(jnp.int32, (GMLP_BLOCK, GMLP_BLOCK), 1) // CHUNK
    mask = ci >= cj
    heads = []
    for h in range(A_HEADS):
        w = jnp.where(mask, ws_ref[h], 0.0).astype(BF16)
        cols = slice(h * A_HEAD_DIM, (h + 1) * A_HEAD_DIM)
        rhs = jnp.concatenate(
            [v[n * GMLP_BLOCK:(n + 1) * GMLP_BLOCK, cols] for n in range(n_blocks)], axis=1)
        heads.append(jnp.dot(w, rhs, preferred_element_type=F32))
    return jnp.concatenate(
        [jnp.concatenate([hd[:, n * A_HEAD_DIM:(n + 1) * A_HEAD_DIM] for hd in heads], axis=1)
         for n in range(n_blocks)], axis=0)


def _mixer_kernel(x_ref, g_ref, w_hbm, lng_ref, lnb_ref, ws_ref, bias_ref, oga_ref, wo_ref,
                  wb_ref, wc_ref, wglu_ref, are_ref, aim_ref, d_ref, ogb_ref,
                  mixa_ref, mixb_ref, wob_ref, st_ref, bu_ref, w_ref, stage_ref, stage_sem):
    t_rows = x_ref.shape[0]
    tile = lambda k: slice(k * LANES, (k + 1) * LANES)
    half_lane_tiles = S5_LANE_TILES // S5_HALVES
    half_state_tiles = S5_STATE_TILES // S5_HALVES
    half_vregs = S5_STATE_VREGS // S5_HALVES
    im_base = half_state_tiles * S5_PITCH

    def scratch_rows(local_tile):
        return slice(local_tile * S5_PITCH, local_tile * S5_PITCH + t_rows)

    def half_state_tiles_of(kl):
        re = [kl * S5_TILES_PER_LANE_TILE + c for c in range(S5_TILES_PER_LANE_TILE)]
        return re + [half_state_tiles + q for q in re]

    @pl.when(pl.program_id(0) == 0)
    def _():
        st_ref[...] = jnp.zeros_like(st_ref)
        chunk = stage_ref.shape[1]
        n_chunks = w_ref.shape[0] // chunk

        def chunk_copy(c):
            return pltpu.make_async_copy(w_hbm.at[c * chunk:(c + 1) * chunk, :],
                                         stage_ref.at[c % 2], stage_sem.at[c % 2])

        chunk_copy(0).start()
        for c in range(n_chunks):
            if c + 1 < n_chunks:
                chunk_copy(c + 1).start()
            chunk_copy(c).wait()
            w_ref[c * chunk:(c + 1) * chunk, :] = stage_ref[c % 2].astype(BF16)

    wob_ref[...] = wo_ref[...].astype(BF16)

    x = x_ref[...]
    xn = (x * _rms_scale(x) * g_ref[...]).astype(BF16)
    u = jnp.dot(xn, w_ref[:, 2 * A_WIDTH:], preferred_element_type=F32)

    h_pre, yc, ys, gates = [], [], [], []
    for hf in range(S5_HALVES):
        for kl in range(half_lane_tiles):
            k = hf * half_lane_tiles + kl
            res = jnp.dot(u[:, tile(k)].astype(BF16), wb_ref[k],
                          preferred_element_type=F32)
            for c, q in enumerate(half_state_tiles_of(kl)):
                bu_ref[scratch_rows(q), :] = res[:, tile(c)]

        gates += [jnp.dot(ys[k].astype(BF16), wglu_ref[k], preferred_element_type=F32)
                  for k in range(len(gates), len(ys))]

        h_pre.append(jnp.dot(xn, w_ref[:, hf * A_WIDTH:(hf + 1) * A_WIDTH],
                             preferred_element_type=F32))

        vregs = [hf * half_vregs + m for m in range(half_vregs)]
        a_re = [are_ref[gm] for gm in vregs]
        a_im = [aim_ref[gm] for gm in vregs]
        s_re = [st_ref[gm] for gm in vregs]
        s_im = [st_ref[S5_STATE_VREGS + gm] for gm in vregs]
        for t in range(t_rows):
            for m in range(half_vregs):
                re_rows = pl.ds(t + SUBLANES * m * S5_PITCH, SUBLANES, stride=S5_PITCH)
                im_rows = pl.ds(t + im_base + SUBLANES * m * S5_PITCH, SUBLANES, stride=S5_PITCH)
                nr = a_re[m] * s_re[m] - a_im[m] * s_im[m] + bu_ref[re_rows, :]
                ni = a_re[m] * s_im[m] + a_im[m] * s_re[m] + bu_ref[im_rows, :]
                bu_ref[re_rows, :] = nr
                bu_ref[im_rows, :] = ni
                s_re[m], s_im[m] = nr, ni
        for m, gm in enumerate(vregs):
            st_ref[gm] = s_re[m]
            st_ref[S5_STATE_VREGS + gm] = s_im[m]

        for kl in range(half_lane_tiles):
            k = hf * half_lane_tiles + kl
            lhs = jnp.concatenate([bu_ref[scratch_rows(q), :] for q in half_state_tiles_of(kl)],
                                  axis=1).astype(BF16)
            yc.append(jnp.dot(lhs, wc_ref[k], preferred_element_type=F32))

        ys += [_gelu(yc[k] + d_ref[:, tile(k)] * u[:, tile(k)]) for k in range(len(ys), len(yc))]
        if hf == 0:
            zu = _gelu(h_pre[0])

    gates += [jnp.dot(ys[k].astype(BF16), wglu_ref[k], preferred_element_type=F32)
              for k in range(len(gates), len(ys))]
    yb = jnp.concatenate([ys[k] * jax.nn.sigmoid(gates[k]) for k in range(S5_LANE_TILES)], axis=1)
    mixb_ref[...] = (yb * _rms_scale(yb) * ogb_ref[...]).astype(BF16)

    v = _layernorm_bf16(_gelu(h_pre[1]), lng_ref, lnb_ref)
    bias = jnp.concatenate([bias_ref[...]] * (t_rows // GMLP_BLOCK), axis=0)
    ya = zu * (_spatial_mix(v, ws_ref) + bias)
    mixa_ref[...] = (ya * _rms_scale(ya) * oga_ref[...]).astype(BF16)


def _mixer(x, g, w_in, ln_g, ln_b, w_s, bias_full, out_a_g, w_out,
           wb, wc, wglu, a_re, a_im, d, out_b_g):
    tm = S5_T
    n_steps = SEQ // tm
    wo_rows = w_out.shape[0] // n_steps
    n_state_cols = 2 * S5_GROUPS_PER_TILE * S5_STATE
    const2 = lambda i: (0, 0)
    const3 = lambda i: (0, 0, 0)
    once = dict(pipeline_mode=pl.Buffered(1))
    return pl.pallas_call(
        _mixer_kernel,
        grid=(n_steps,),
        in_specs=[
            pl.BlockSpec((tm, D_MODEL), lambda i: (i, 0)),
            pl.BlockSpec((1, D_MODEL), const2),
            pl.BlockSpec(memory_space=pl.ANY),
            pl.BlockSpec((1, A_WIDTH), const2),
            pl.BlockSpec((1, A_WIDTH), const2),
            pl.BlockSpec((A_HEADS, GMLP_BLOCK, GMLP_BLOCK), const3),
            pl.BlockSpec((GMLP_BLOCK, A_WIDTH), const2),
            pl.BlockSpec((1, A_WIDTH), const2),
            pl.BlockSpec((wo_rows, D_MODEL), lambda i: (i, 0)),
            pl.BlockSpec((S5_LANE_TILES, LANES, n_state_cols), const3, **once),
            pl.BlockSpec((S5_LANE_TILES, n_state_cols, LANES), const3, **once),
            pl.BlockSpec((S5_LANE_TILES, LANES, LANES), const3),
            pl.BlockSpec((S5_STATE_VREGS, SUBLANES, LANES), const3),
            pl.BlockSpec((S5_STATE_VREGS, SUBLANES, LANES), const3),
            pl.BlockSpec((1, B_WIDTH), const2),
            pl.BlockSpec((1, B_WIDTH), const2),
        ],
        out_specs=[
            pl.BlockSpec((tm, A_WIDTH), lambda i: (i, 0)),
            pl.BlockSpec((tm, B_WIDTH), lambda i: (i, 0)),
            pl.BlockSpec((wo_rows, D_MODEL), lambda i: (i, 0)),
        ],
        out_shape=[jax.ShapeDtypeStruct((SEQ, A_WIDTH), BF16),
                   jax.ShapeDtypeStruct((SEQ, B_WIDTH), BF16),
                   jax.ShapeDtypeStruct(w_out.shape, BF16)],
        scratch_shapes=[
            pltpu.VMEM((2 * S5_STATE_VREGS, SUBLANES, LANES), F32),
            pltpu.VMEM((2 * S5_STATE_TILES // S5_HALVES * S5_PITCH, LANES), F32),
            pltpu.VMEM((D_MODEL, IN_WIDTH), BF16),
            pltpu.VMEM((2, W_IN_STAGE_ROWS, IN_WIDTH), F32),
            pltpu.SemaphoreType.DMA((2,)),
        ],
        compiler_params=pltpu.CompilerParams(
            dimension_semantics=("arbitrary",), vmem_limit_bytes=MIXER_VMEM_LIMIT),
        name="mixer",
    )(x, g, w_in, ln_g, ln_b, w_s, bias_full, out_a_g, w_out,
      wb, wc, wglu, a_re, a_im, d, out_b_g)


def _repeat_cols_onehot(n_rows, n_cols):
    r = lax.broadcasted_iota(jnp.int32, (n_rows, n_cols), 0)
    c = lax.broadcasted_iota(jnp.int32, (n_rows, n_cols), 1)
    return jnp.where(r == c % n_rows, 1.0, 0.0).astype(BF16)


def _block_diag_mask(n_rows, n_cols, row_block, col_block):
    r = lax.broadcasted_iota(jnp.int32, (n_rows, n_cols), 0)
    c = lax.broadcasted_iota(jnp.int32, (n_rows, n_cols), 1)
    return r // row_block == c // col_block


def _s5_params_kernel(lre_ref, lim_ref, ldt_ref, bre_ref, bim_ref, cre_ref, cim_ref, wg_ref,
                      abr_ref, abi_ref, wb_ref, wc_ref, wglu_ref):
    lr, li = lre_ref[...], lim_ref[...]
    dt = jnp.exp(ldt_ref[...])
    mag = jnp.exp(lr * dt)
    ab_r = mag * jnp.cos(li * dt)
    ab_i = mag * jnp.sin(li * dt)
    den = lr * lr + li * li
    nr = ab_r - 1.0
    co_r = (nr * lr + ab_i * li) / den
    co_i = (ab_i * lr - nr * li) / den
    abr_ref[...] = ab_r
    abi_ref[...] = ab_i

    gt, h, p = S5_GROUPS_PER_TILE, S5_GROUP_CH, S5_STATE
    spread_p = _repeat_cols_onehot(p, gt * p)
    spread_h = _repeat_cols_onehot(h, gt * h)
    eye_gh = _repeat_cols_onehot(gt * h, gt * h)
    mask_b = _block_diag_mask(gt * h, gt * p, h, p)
    mask_c = _block_diag_mask(gt * p, gt * h, p, h)
    mask_g = _block_diag_mask(gt * h, gt * h, h, h)
    for k in range(S5_LANE_TILES):
        grp = slice(k * gt, (k + 1) * gt)
        cr = co_r[grp][:, None, :]
        ci = co_i[grp][:, None, :]
        br, bi = bre_ref[grp], bim_ref[grp]
        bb_r = (cr * br - ci * bi).reshape(gt * h, p).astype(BF16)
        bb_i = (cr * bi + ci * br).reshape(gt * h, p).astype(BF16)
        for half, bb in enumerate((bb_r, bb_i)):
            full = jnp.dot(bb, spread_p, preferred_element_type=F32)
            wb_ref[k, :, half * gt * p:(half + 1) * gt * p] = (
                jnp.where(mask_b, full, 0.0).astype(BF16))
        for half, (c_ref, sign) in enumerate(((cre_ref, 1.0), (cim_ref, -1.0))):
            c_k = (sign * c_ref[grp]).reshape(gt * h, p).astype(BF16)
            c_t = lax.dot_general(c_k, eye_gh, (((0,), (0,)), ((), ())),
                                  preferred_element_type=F32)
            full = jnp.concatenate([c_t] * gt, axis=0)
            wc_ref[k, half * gt * p:(half + 1) * gt * p, :] = (
                jnp.where(mask_c, full, 0.0).astype(BF16))
        g_k = wg_ref[grp].reshape(gt * h, h).astype(BF16)
        full = jnp.dot(g_k, spread_h, preferred_element_type=F32)
        wglu_ref[k] = jnp.where(mask_g, full, 0.0).astype(BF16)


def _s5_params(lam_re, lam_im, log_dt, b_re, b_im, c_re, c_im, w_glu):
    gt, h, p = S5_GROUPS_PER_TILE, S5_GROUP_CH, S5_STATE
    g_p = jax.ShapeDtypeStruct((S5_GROUPS, p), F32)
    ab_r, ab_i, wb, wc, wglu = pl.pallas_call(
        _s5_params_kernel,
        out_shape=[g_p, g_p,
                   jax.ShapeDtypeStruct((S5_LANE_TILES, gt * h, 2 * gt * p), BF16),
                   jax.ShapeDtypeStruct((S5_LANE_TILES, 2 * gt * p, gt * h), BF16),
                   jax.ShapeDtypeStruct((S5_LANE_TILES, gt * h, gt * h), BF16)],
        name="s5_params",
    )(lam_re, lam_im, log_dt[:, None], jnp.swapaxes(b_re, 1, 2), jnp.swapaxes(b_im, 1, 2),
      c_re, c_im, w_glu)
    a_re = ab_r.reshape(S5_STATE_VREGS, SUBLANES, LANES)
    a_im = ab_i.reshape(S5_STATE_VREGS, SUBLANES, LANES)
    return wb, wc, wglu, a_re, a_im


def _outproj_kernel(x_ref, ma_ref, mb_ref, wa_ref, wb_ref, g_ref, x1_ref, hn_ref):
    half = x_ref.shape[0] // 2
    for r in range(2):
        rows = slice(r * half, (r + 1) * half)
        acc = jnp.dot(ma_ref[rows, :], wa_ref[...], preferred_element_type=F32)
        acc = acc + jnp.dot(mb_ref[rows, :], wb_ref[...], preferred_element_type=F32)
        x1 = x_ref[rows, :] + acc
        x1_ref[rows, :] = x1
        hn_ref[rows, :] = (x1 * _rms_scale(x1) * g_ref[...]).astype(BF16)


def _outproj(x, mix_a, mix_b, w_out_bf16, g, tm=512):
    deep = dict(pipeline_mode=pl.Buffered(3))
    pipeline = pltpu.emit_pipeline(
        _outproj_kernel,
        grid=(SEQ // tm,),
        in_specs=[
            pl.BlockSpec((tm, D_MODEL), lambda i: (i, 0), **deep),
            pl.BlockSpec((tm, A_WIDTH), lambda i: (i, 0), **deep),
            pl.BlockSpec((tm, B_WIDTH), lambda i: (i, 0), **deep),
            pl.BlockSpec((A_WIDTH, D_MODEL), lambda i: (0, 0)),
            pl.BlockSpec((B_WIDTH, D_MODEL), lambda i: (1, 0)),
            pl.BlockSpec((1, D_MODEL), lambda i: (0, 0)),
        ],
        out_specs=[
            pl.BlockSpec((tm, D_MODEL), lambda i: (i, 0)),
            pl.BlockSpec((tm, D_MODEL), lambda i: (i, 0)),
        ],
    )
    hbm = pl.BlockSpec(memory_space=pl.ANY)
    return pl.pallas_call(
        lambda *refs: pipeline(*refs),
        in_specs=[hbm] * 6,
        out_specs=[hbm] * 2,
        out_shape=[jax.ShapeDtypeStruct((SEQ, D_MODEL), F32),
                   jax.ShapeDtypeStruct((SEQ, D_MODEL), BF16)],
        compiler_params=pltpu.CompilerParams(vmem_limit_bytes=VMEM_LIMIT),
        name="outproj",
    )(x, mix_a, mix_b, w_out_bf16, w_out_bf16, g)


def _ffn_up_kernel(hn_ref, wg_ref, wu_ref, wd_ref, act_ref, wdb_ref):
    wdb_ref[...] = wd_ref[...].astype(BF16)
    hn = hn_ref[...]
    for c in range(act_ref.shape[1] // MXU_COLS):
        cols = slice(c * MXU_COLS, (c + 1) * MXU_COLS)
        gate = jnp.dot(hn, wg_ref[:, cols].astype(BF16), preferred_element_type=F32)
        up = jnp.dot(hn, wu_ref[:, cols].astype(BF16), preferred_element_type=F32)
        act_ref[:, cols] = (gate * jax.nn.sigmoid(gate) * up).astype(BF16)


def _ffn_up(hn, wg, wu, wd, tm=2048, tf=512):
    n_i = SEQ // tm
    wd_rows = wd.shape[0] // (D_FF // tf * n_i)
    wd_slab = pl.BlockSpec((wd_rows, D_MODEL), lambda f, i: (f * n_i + i, 0))
    return pl.pallas_call(
        _ffn_up_kernel,
        grid=(D_FF // tf, n_i),
        in_specs=[
            pl.BlockSpec((tm, D_MODEL), lambda f, i: (i, 0)),
            pl.BlockSpec((D_MODEL, tf), lambda f, i: (0, f)),
            pl.BlockSpec((D_MODEL, tf), lambda f, i: (0, f)),
            wd_slab,
        ],
        out_specs=[pl.BlockSpec((tm, tf), lambda f, i: (i, f)), wd_slab],
        out_shape=[jax.ShapeDtypeStruct((SEQ, D_FF), BF16),
                   jax.ShapeDtypeStruct(wd.shape, BF16)],
        compiler_params=pltpu.CompilerParams(
            dimension_semantics=("arbitrary", "arbitrary"),
            vmem_limit_bytes=VMEM_LIMIT),
        name="ffn_up",
    )(hn, wg, wu, wd)


def _ffn_down_kernel(act_ref, wd_ref, x1_ref, g_ref, o_ref):
    n_tiles = pl.num_programs(1)
    tn = wd_ref.shape[1]
    for n in range(o_ref.shape[1] // tn):
        @pl.when(pl.program_id(1) == n)
        def _(n=n):
            o_ref[:, n * tn:(n + 1) * tn] = x1_ref[...] + jnp.dot(
                act_ref[...], wd_ref[...], preferred_element_type=F32)

    @pl.when(pl.program_id(1) == n_tiles - 1)
    def _():
        x2 = o_ref[...]
        o_ref[...] = x2 * _rms_scale(x2) * g_ref[...]


def _ffn_down(act, wd_bf16, x1, g, tm=1024, tn=512):
    ahead = dict(pipeline_mode=pl.Buffered(2, use_lookahead=True))
    pipeline = pltpu.emit_pipeline(
        _ffn_down_kernel,
        grid=(SEQ // tm, D_MODEL // tn),
        in_specs=[
            pl.BlockSpec((tm, D_FF), lambda i, n: (i, 0), **ahead),
            pl.BlockSpec((D_FF, tn), lambda i, n: (0, n)),
            pl.BlockSpec((tm, tn), lambda i, n: (i, n)),
            pl.BlockSpec((1, D_MODEL), lambda i, n: (0, 0)),
        ],
        out_specs=[pl.BlockSpec((tm, D_MODEL), lambda i, n: (i, 0))],
    )
    hbm = pl.BlockSpec(memory_space=pl.ANY)
    return pl.pallas_call(
        lambda *refs: pipeline(*refs),
        in_specs=[hbm] * 4,
        out_specs=hbm,
        out_shape=jax.ShapeDtypeStruct((SEQ, D_MODEL), F32),
        compiler_params=pltpu.CompilerParams(vmem_limit_bytes=FFN_DOWN_VMEM_LIMIT),
        name="ffn_down",
    )(act, wd_bf16, x1, g)


def kernel(x, norm_mix_g, w_in, a_ln_g, a_ln_b, a_w_s, a_b_s, s5_lambda_re, s5_lambda_im,
           s5_log_dt, s5_b_re, s5_b_im, s5_c_re, s5_c_im, s5_d, s5_w_glu, out_norm_a_g,
           out_norm_b_g, w_out, norm_ffn_g, w_gate, w_up, w_down, final_norm_g):
    assert x.shape == (1, SEQ, D_MODEL) and norm_mix_g.shape[0] == 1
    xs = x.reshape(SEQ, D_MODEL)
    l = 0

    bias_full = jnp.repeat(jnp.transpose(a_b_s[l]), A_HEAD_DIM, axis=1)
    wb, wc, wglu, a_re, a_im = _s5_params(
        s5_lambda_re[l], s5_lambda_im[l], s5_log_dt[l], s5_b_re[l], s5_b_im[l],
        s5_c_re[l], s5_c_im[l], s5_w_glu[l])
    mix_a, mix_b, w_out_bf16 = _mixer(
        xs, norm_mix_g[l][None], w_in[l], a_ln_g[l][None], a_ln_b[l][None],
        a_w_s[l], bias_full, out_norm_a_g[l][None], w_out[l],
        wb, wc, wglu, a_re, a_im, s5_d[l][None], out_norm_b_g[l][None])

    x1, hn = _outproj(xs, mix_a, mix_b, w_out_bf16, norm_ffn_g[l][None])

    act, w_down_bf16 = _ffn_up(hn, w_gate[l], w_up[l], w_down[l])
    out = _ffn_down(act, w_down_bf16, x1, final_norm_g[None])
    return out.reshape(1, SEQ, D_MODEL)
```

```python
import math

import jax
import jax.numpy as jnp
from jax import lax
from jax.experimental import pallas as pl
from jax.experimental.pallas import tpu as pltpu

F32 = jnp.float32
BF16 = jnp.bfloat16

D_MODEL = 2048
SEQ = 8192
CHUNK = 64
A_WIDTH = 1024
A_HEADS = 8
A_HEAD_DIM = 128
GMLP_BLOCK = 128
B_WIDTH = 1024
S5_GROUP_CH = 16
S5_GROUPS = 64
S5_STATE = 64
IN_WIDTH = 3072
D_FF = 5632
EPS = 1e-6

LANES = 128
SUBLANES = 8
MXU_COLS = 256
VMEM_LIMIT = 56 * 1024 * 1024
FFN_DOWN_VMEM_LIMIT = 60 * 1024 * 1024
MIXER_VMEM_LIMIT = 60 * 1024 * 1024
W_IN_STAGE_ROWS = 256

S5_LANE_TILES = B_WIDTH // LANES
S5_GROUPS_PER_TILE = LANES // S5_GROUP_CH
S5_STATE_TILES = S5_GROUPS * S5_STATE // LANES
S5_TILES_PER_LANE_TILE = S5_STATE_TILES // S5_LANE_TILES
S5_STATE_VREGS = S5_STATE_TILES // SUBLANES
S5_T = 512
S5_HALVES = 2
S5_PITCH = S5_T + 4


def _gelu(x):
    return 0.5 * x * (1.0 + lax.erf(x * (1.0 / math.sqrt(2.0))))


def _rms_scale(x):
    return lax.rsqrt(jnp.mean(x * x, axis=-1, keepdims=True) + EPS)


def _layernorm_bf16(z, g_ref, b_ref):
    mu = jnp.mean(z, axis=-1, keepdims=True)
    zc = z - mu
    var = jnp.mean(zc * zc, axis=-1, keepdims=True)
    return (zc * lax.rsqrt(var + EPS) * g_ref[...] + b_ref[...]).astype(BF16)


def _spatial_mix(v, ws_ref):
    n_blocks = v.shape[0] // GMLP_BLOCK
    ci = lax.broadcasted_iota(jnp.int32, (GMLP_BLOCK, GMLP_BLOCK), 0) // CHUNK
    cj = lax.broadcasted_iota(jnp.int32, (GMLP_BLOCK, GMLP_BLOCK), 1) // CHUNK
    mask = ci >= cj
    heads = []
    for h in range(A_HEADS):
        w = jnp.where(mask, ws_ref[h], 0.0).astype(BF16)
        cols = slice(h * A_HEAD_DIM, (h + 1) * A_HEAD_DIM)
        rhs = jnp.concatenate(
            [v[n * GMLP_BLOCK:(n + 1) * GMLP_BLOCK, cols] for n in range(n_blocks)], axis=1)
        heads.append(jnp.dot(w, rhs, preferred_element_type=F32))
    return jnp.concatenate(
        [jnp.concatenate([hd[:, n * A_HEAD_DIM:(n + 1) * A_HEAD_DIM] for hd in heads], axis=1)
         for n in range(n_blocks)], axis=0)


def _mixer_kernel(x_ref, g_ref, w_hbm, lng_ref, lnb_ref, ws_ref, bias_ref, oga_ref, wo_ref,
                  wb_ref, wc_ref, wglu_ref, are_ref, aim_ref, d_ref, ogb_ref,
                  mixa_ref, mixb_ref, wob_ref, st_ref, bu_ref, w_ref, stage_ref, stage_sem):
    t_rows = x_ref.shape[0]
    tile = lambda k: slice(k * LANES, (k + 1) * LANES)
    half_lane_tiles = S5_LANE_TILES // S5_HALVES
    half_state_tiles = S5_STATE_TILES // S5_HALVES
    half_vregs = S5_STATE_VREGS // S5_HALVES
    im_base = half_state_tiles * S5_PITCH

    def scratch_rows(local_tile):
        return slice(local_tile * S5_PITCH, local_tile * S5_PITCH + t_rows)

    def half_state_tiles_of(kl):
        re = [kl * S5_TILES_PER_LANE_TILE + c for c in range(S5_TILES_PER_LANE_TILE)]
        return re + [half_state_tiles + q for q in re]

    @pl.when(pl.program_id(0) == 0)
    def _():
        st_ref[...] = jnp.zeros_like(st_ref)
        chunk = stage_ref.shape[1]
        n_chunks = w_ref.shape[0] // chunk

        def chunk_copy(c):
            return pltpu.make_async_copy(w_hbm.at[c * chunk:(c + 1) * chunk, :],
                                         stage_ref.at[c % 2], stage_sem.at[c % 2])

        chunk_copy(0).start()
        for c in range(n_chunks):
            if c + 1 < n_chunks:
                chunk_copy(c + 1).start()
            chunk_copy(c).wait()
            w_ref[c * chunk:(c + 1) * chunk, :] = stage_ref[c % 2].astype(BF16)

    wob_ref[...] = wo_ref[...].astype(BF16)

    x = x_ref[...]
    xn = (x * _rms_scale(x) * g_ref[...]).astype(BF16)
    u = jnp.dot(xn, w_ref[:, 2 * A_WIDTH:], preferred_element_type=F32)

    h_pre, yc, ys, gates = [], [], [], []
    for hf in range(S5_HALVES):
        for kl in range(half_lane_tiles):
            k = hf * half_lane_tiles + kl
            res = jnp.dot(u[:, tile(k)].astype(BF16), wb_ref[k],
                          preferred_element_type=F32)
            for c, q in enumerate(half_state_tiles_of(kl)):
                bu_ref[scratch_rows(q), :] = res[:, tile(c)]

        gates += [jnp.dot(ys[k].astype(BF16), wglu_ref[k], preferred_element_type=F32)
                  for k in range(len(gates), len(ys))]

        h_pre.append(jnp.dot(xn, w_ref[:, hf * A_WIDTH:(hf + 1) * A_WIDTH],
                             preferred_element_type=F32))

        vregs = [hf * half_vregs + m for m in range(half_vregs)]
        a_re = [are_ref[gm] for gm in vregs]
        a_im = [aim_ref[gm] for gm in vregs]
        s_re = [st_ref[gm] for gm in vregs]
        s_im = [st_ref[S5_STATE_VREGS + gm] for gm in vregs]
        for t in range(t_rows):
            for m in range(half_vregs):
                re_rows = pl.ds(t + SUBLANES * m * S5_PITCH, SUBLANES, stride=S5_PITCH)
                im_rows = pl.ds(t + im_base + SUBLANES * m * S5_PITCH, SUBLANES, stride=S5_PITCH)
                nr = a_re[m] * s_re[m] - a_im[m] * s_im[m] + bu_ref[re_rows, :]
                ni = a_re[m] * s_im[m] + a_im[m] * s_re[m] + bu_ref[im_rows, :]
                bu_ref[re_rows, :] = nr
                bu_ref[im_rows, :] = ni
                s_re[m], s_im[m] = nr, ni
        for m, gm in enumerate(vregs):
            st_ref[gm] = s_re[m]
            st_ref[S5_STATE_VREGS + gm] = s_im[m]

        for kl in range(half_lane_tiles):
            k = hf * half_lane_tiles + kl
            lhs = jnp.concatenate([bu_ref[scratch_rows(q), :] for q in half_state_tiles_of(kl)],
                                  axis=1).astype(BF16)
            yc.append(jnp.dot(lhs, wc_ref[k], preferred_element_type=F32))

        ys += [_gelu(yc[k] + d_ref[:, tile(k)] * u[:, tile(k)]) for k in range(len(ys), len(yc))]
        if hf == 0:
            zu = _gelu(h_pre[0])

    gates += [jnp.dot(ys[k].astype(BF16), wglu_ref[k], preferred_element_type=F32)
              for k in range(len(gates), len(ys))]
    yb = jnp.concatenate([ys[k] * jax.nn.sigmoid(gates[k]) for k in range(S5_LANE_TILES)], axis=1)
    mixb_ref[...] = (yb * _rms_scale(yb) * ogb_ref[...]).astype(BF16)

    v = _layernorm_bf16(_gelu(h_pre[1]), lng_ref, lnb_ref)
    bias = jnp.concatenate([bias_ref[...]] * (t_rows // GMLP_BLOCK), axis=0)
    ya = zu * (_spatial_mix(v, ws_ref) + bias)
    mixa_ref[...] = (ya * _rms_scale(ya) * oga_ref[...]).astype(BF16)


def _mixer(x, g, w_in, ln_g, ln_b, w_s, bias_full, out_a_g, w_out,
           wb, wc, wglu, a_re, a_im, d, out_b_g):
    tm = S5_T
    n_steps = SEQ // tm
    wo_rows = w_out.shape[0] // n_steps
    n_state_cols = 2 * S5_GROUPS_PER_TILE * S5_STATE
    const2 = lambda i: (0, 0)
    const3 = lambda i: (0, 0, 0)
    once = dict(pipeline_mode=pl.Buffered(1))
    return pl.pallas_call(
        _mixer_kernel,
        grid=(n_steps,),
        in_specs=[
            pl.BlockSpec((tm, D_MODEL), lambda i: (i, 0)),
            pl.BlockSpec((1, D_MODEL), const2),
            pl.BlockSpec(memory_space=pl.ANY),
            pl.BlockSpec((1, A_WIDTH), const2),
            pl.BlockSpec((1, A_WIDTH), const2),
            pl.BlockSpec((A_HEADS, GMLP_BLOCK, GMLP_BLOCK), const3),
            pl.BlockSpec((GMLP_BLOCK, A_WIDTH), const2),
            pl.BlockSpec((1, A_WIDTH), const2),
            pl.BlockSpec((wo_rows, D_MODEL), lambda i: (i, 0)),
            pl.BlockSpec((S5_LANE_TILES, LANES, n_state_cols), const3, **once),
            pl.BlockSpec((S5_LANE_TILES, n_state_cols, LANES), const3, **once),
            pl.BlockSpec((S5_LANE_TILES, LANES, LANES), const3),
            pl.BlockSpec((S5_STATE_VREGS, SUBLANES, LANES), const3),
            pl.BlockSpec((S5_STATE_VREGS, SUBLANES, LANES), const3),
            pl.BlockSpec((1, B_WIDTH), const2),
            pl.BlockSpec((1, B_WIDTH), const2),
        ],
        out_specs=[
            pl.BlockSpec((tm, A_WIDTH), lambda i: (i, 0)),
            pl.BlockSpec((tm, B_WIDTH), lambda i: (i, 0)),
            pl.BlockSpec((wo_rows, D_MODEL), lambda i: (i, 0)),
        ],
        out_shape=[jax.ShapeDtypeStruct((SEQ, A_WIDTH), BF16),
                   jax.ShapeDtypeStruct((SEQ, B_WIDTH), BF16),
                   jax.ShapeDtypeStruct(w_out.shape, BF16)],
        scratch_shapes=[
            pltpu.VMEM((2 * S5_STATE_VREGS, SUBLANES, LANES), F32),
            pltpu.VMEM((2 * S5_STATE_TILES // S5_HALVES * S5_PITCH, LANES), F32),
            pltpu.VMEM((D_MODEL, IN_WIDTH), BF16),
            pltpu.VMEM((2, W_IN_STAGE_ROWS, IN_WIDTH), F32),
            pltpu.SemaphoreType.DMA((2,)),
        ],
        compiler_params=pltpu.CompilerParams(
            dimension_semantics=("arbitrary",), vmem_limit_bytes=MIXER_VMEM_LIMIT),
        name="mixer",
    )(x, g, w_in, ln_g, ln_b, w_s, bias_full, out_a_g, w_out,
      wb, wc, wglu, a_re, a_im, d, out_b_g)


def _repeat_cols_onehot(n_rows, n_cols):
    r = lax.broadcasted_iota(jnp.int32, (n_rows, n_cols), 0)
    c = lax.broadcasted_iota(jnp.int32, (n_rows, n_cols), 1)
    return jnp.where(r == c % n_rows, 1.0, 0.0).astype(BF16)


def _block_diag_mask(n_rows, n_cols, row_block, col_block):
    r = lax.broadcasted_iota(jnp.int32, (n_rows, n_cols), 0)
    c = lax.broadcasted_iota(jnp.int32, (n_rows, n_cols), 1)
    return r // row_block == c // col_block


def _s5_params_kernel(lre_ref, lim_ref, ldt_ref, bre_ref, bim_ref, cre_ref, cim_ref, wg_ref,
                      are_ref, aim_ref, wb_ref, wc_ref, wglu_ref, abr_ref, abi_ref):
    lr, li = lre_ref[...], lim_ref[...]
    dt = jnp.exp(ldt_ref[...])
    mag = jnp.exp(lr * dt)
    ab_r = mag * jnp.cos(li * dt)
    ab_i = mag * jnp.sin(li * dt)
    den = lr * lr + li * li
    nr = ab_r - 1.0
    co_r = (nr * lr + ab_i * li) / den
    co_i = (ab_i * lr - nr * li) / den
    abr_ref[...] = ab_r
    abi_ref[...] = ab_i
    even, odd = pl.ds(0, S5_GROUPS // 2, stride=2), pl.ds(1, S5_GROUPS // 2, stride=2)
    are_ref[...] = jnp.concatenate([abr_ref[even, :], abr_ref[odd, :]], axis=1)
    aim_ref[...] = jnp.concatenate([abi_ref[even, :], abi_ref[odd, :]], axis=1)

    gt, h, p = S5_GROUPS_PER_TILE, S5_GROUP_CH, S5_STATE
    spread_p = _repeat_cols_onehot(p, gt * p)
    spread_h = _repeat_cols_onehot(h, gt * h)
    eye_gh = _repeat_cols_onehot(gt * h, gt * h)
    mask_b = _block_diag_mask(gt * h, gt * p, h, p)
    mask_c = _block_diag_mask(gt * p, gt * h, p, h)
    mask_g = _block_diag_mask(gt * h, gt * h, h, h)
    for k in range(S5_LANE_TILES):
        grp = slice(k * gt, (k + 1) * gt)
        cr = co_r[grp][:, None, :]
        ci = co_i[grp][:, None, :]
        br = jnp.swapaxes(bre_ref[grp], 1, 2)
        bi = jnp.swapaxes(bim_ref[grp], 1, 2)
        bb_r = (cr * br - ci * bi).reshape(gt * h, p).astype(BF16)
        bb_i = (cr * bi + ci * br).reshape(gt * h, p).astype(BF16)
        for half, bb in enumerate((bb_r, bb_i)):
            full = jnp.dot(bb, spread_p, preferred_element_type=F32)
            wb_ref[k, :, half * gt * p:(half + 1) * gt * p] = (
                jnp.where(mask_b, full, 0.0).astype(BF16))
        for half, (c_ref, sign) in enumerate(((cre_ref, 1.0), (cim_ref, -1.0))):
            c_k = (sign * c_ref[grp]).reshape(gt * h, p).astype(BF16)
            c_t = lax.dot_general(c_k, eye_gh, (((0,), (0,)), ((), ())),
                                  preferred_element_type=F32)
            full = jnp.concatenate([c_t] * gt, axis=0)
            wc_ref[k, half * gt * p:(half + 1) * gt * p, :] = (
                jnp.where(mask_c, full, 0.0).astype(BF16))
        g_k = wg_ref[grp].reshape(gt * h, h).astype(BF16)
        full = jnp.dot(g_k, spread_h, preferred_element_type=F32)
        wglu_ref[k] = jnp.where(mask_g, full, 0.0).astype(BF16)


def _s5_params(lam_re, lam_im, log_dt, b_re, b_im, c_re, c_im, w_glu):
    gt, h, p = S5_GROUPS_PER_TILE, S5_GROUP_CH, S5_STATE
    a_tiles = jax.ShapeDtypeStruct((S5_STATE_TILES, LANES), F32)
    a_re, a_im, wb, wc, wglu = pl.pallas_call(
        _s5_params_kernel,
        out_shape=[a_tiles, a_tiles,
                   jax.ShapeDtypeStruct((S5_LANE_TILES, gt * h, 2 * gt * p), BF16),
                   jax.ShapeDtypeStruct((S5_LANE_TILES, 2 * gt * p, gt * h), BF16),
                   jax.ShapeDtypeStruct((S5_LANE_TILES, gt * h, gt * h), BF16)],
        scratch_shapes=[pltpu.VMEM((S5_GROUPS, p), F32), pltpu.VMEM((S5_GROUPS, p), F32)],
        name="s5_params",
    )(lam_re, lam_im, log_dt[:, None], b_re, b_im, c_re, c_im, w_glu)
    a_re = a_re.reshape(S5_STATE_VREGS, SUBLANES, LANES)
    a_im = a_im.reshape(S5_STATE_VREGS, SUBLANES, LANES)
    return wb, wc, wglu, a_re, a_im


def _outproj_kernel(x_ref, ma_ref, mb_ref, wa_ref, wb_ref, g_ref, x1_ref, hn_ref):
    half = x_ref.shape[0] // 2
    for r in range(2):
        rows = slice(r * half, (r + 1) * half)
        acc = jnp.dot(ma_ref[rows, :], wa_ref[...], preferred_element_type=F32)
        acc = acc + jnp.dot(mb_ref[rows, :], wb_ref[...], preferred_element_type=F32)
        x1 = x_ref[rows, :] + acc
        x1_ref[rows, :] = x1
        hn_ref[rows, :] = (x1 * _rms_scale(x1) * g_ref[...]).astype(BF16)


def _outproj(x, mix_a, mix_b, w_out_bf16, g, tm=512):
    deep = dict(pipeline_mode=pl.Buffered(3))
    pipeline = pltpu.emit_pipeline(
        _outproj_kernel,
        grid=(SEQ // tm,),
        in_specs=[
            pl.BlockSpec((tm, D_MODEL), lambda i: (i, 0), **deep),
            pl.BlockSpec((tm, A_WIDTH), lambda i: (i, 0), **deep),
            pl.BlockSpec((tm, B_WIDTH), lambda i: (i, 0), **deep),
            pl.BlockSpec((A_WIDTH, D_MODEL), lambda i: (0, 0)),
            pl.BlockSpec((B_WIDTH, D_MODEL), lambda i: (1, 0)),
            pl.BlockSpec((1, D_MODEL), lambda i: (0, 0)),
        ],
        out_specs=[
            pl.BlockSpec((tm, D_MODEL), lambda i: (i, 0)),
            pl.BlockSpec((tm, D_MODEL), lambda i: (i, 0)),
        ],
    )
    hbm = pl.BlockSpec(memory_space=pl.ANY)
    return pl.pallas_call(
        lambda *refs: pipeline(*refs),
        in_specs=[hbm] * 6,
        out_specs=[hbm] * 2,
        out_shape=[jax.ShapeDtypeStruct((SEQ, D_MODEL), F32),
                   jax.ShapeDtypeStruct((SEQ, D_MODEL), BF16)],
        compiler_params=pltpu.CompilerParams(vmem_limit_bytes=VMEM_LIMIT),
        name="outproj",
    )(x, mix_a, mix_b, w_out_bf16, w_out_bf16, g)


def _ffn_up_kernel(hn_ref, wg_ref, wu_ref, wd_ref, act_ref, wdb_ref):
    wdb_ref[...] = wd_ref[...].astype(BF16)
    hn = hn_ref[...]
    for c in range(act_ref.shape[1] // MXU_COLS):
        cols = slice(c * MXU_COLS, (c + 1) * MXU_COLS)
        gate = jnp.dot(hn, wg_ref[:, cols].astype(BF16), preferred_element_type=F32)
        up = jnp.dot(hn, wu_ref[:, cols].astype(BF16), preferred_element_type=F32)
        act_ref[:, cols] = (gate * jax.nn.sigmoid(gate) * up).astype(BF16)


def _ffn_up(hn, wg, wu, wd, tm=2048, tf=512):
    n_i = SEQ // tm
    wd_rows = wd.shape[0] // (D_FF // tf * n_i)
    wd_slab = pl.BlockSpec((wd_rows, D_MODEL), lambda f, i: (f * n_i + i, 0))
    return pl.pallas_call(
        _ffn_up_kernel,
        grid=(D_FF // tf, n_i),
        in_specs=[
            pl.BlockSpec((tm, D_MODEL), lambda f, i: (i, 0)),
            pl.BlockSpec((D_MODEL, tf), lambda f, i: (0, f)),
            pl.BlockSpec((D_MODEL, tf), lambda f, i: (0, f)),
            wd_slab,
        ],
        out_specs=[pl.BlockSpec((tm, tf), lambda f, i: (i, f)), wd_slab],
        out_shape=[jax.ShapeDtypeStruct((SEQ, D_FF), BF16),
                   jax.ShapeDtypeStruct(wd.shape, BF16)],
        compiler_params=pltpu.CompilerParams(
            dimension_semantics=("arbitrary", "arbitrary"),
            vmem_limit_bytes=VMEM_LIMIT),
        name="ffn_up",
    )(hn, wg, wu, wd)


def _ffn_down_kernel(act_ref, wd_ref, x1_ref, g_ref, o_ref):
    n_tiles = pl.num_programs(1)
    tn = wd_ref.shape[1]
    for n in range(o_ref.shape[1] // tn):
        @pl.when(pl.program_id(1) == n)
        def _(n=n):
            o_ref[:, n * tn:(n + 1) * tn] = x1_ref[...] + jnp.dot(
                act_ref[...], wd_ref[...], preferred_element_type=F32)

    @pl.when(pl.program_id(1) == n_tiles - 1)
    def _():
        x2 = o_ref[...]
        o_ref[...] = x2 * _rms_scale(x2) * g_ref[...]


def _ffn_down(act, wd_bf16, x1, g, tm=1024, tn=512):
    return pl.pallas_call(
        _ffn_down_kernel,
        grid=(SEQ // tm, D_MODEL // tn),
        in_specs=[
            pl.BlockSpec((tm, D_FF), lambda i, n: (i, 0)),
            pl.BlockSpec((D_FF, tn), lambda i, n: (0, n)),
            pl.BlockSpec((tm, tn), lambda i, n: (i, n)),
            pl.BlockSpec((1, D_MODEL), lambda i, n: (0, 0)),
        ],
        out_specs=pl.BlockSpec((tm, D_MODEL), lambda i, n: (i, 0)),
        out_shape=jax.ShapeDtypeStruct((SEQ, D_MODEL), F32),
        compiler_params=pltpu.CompilerParams(
            dimension_semantics=("arbitrary", "arbitrary"),
            vmem_limit_bytes=FFN_DOWN_VMEM_LIMIT),
        name="ffn_down",
    )(act, wd_bf16, x1, g)


def kernel(x, norm_mix_g, w_in, a_ln_g, a_ln_b, a_w_s, a_b_s, s5_lambda_re, s5_lambda_im,
           s5_log_dt, s5_b_re, s5_b_im, s5_c_re, s5_c_im, s5_d, s5_w_glu, out_norm_a_g,
           out_norm_b_g, w_out, norm_ffn_g, w_gate, w_up, w_down, final_norm_g):
    assert x.shape == (1, SEQ, D_MODEL) and norm_mix_g.shape[0] == 1
    xs = x.reshape(SEQ, D_MODEL)
    l = 0

    bias_full = jnp.repeat(jnp.transpose(a_b_s[l]), A_HEAD_DIM, axis=1)
    wb, wc, wglu, a_re, a_im = _s5_params(
        s5_lambda_re[l], s5_lambda_im[l], s5_log_dt[l], s5_b_re[l], s5_b_im[l],
        s5_c_re[l], s5_c_im[l], s5_w_glu[l])
    mix_a, mix_b, w_out_bf16 = _mixer(
        xs, norm_mix_g[l][None], w_in[l], a_ln_g[l][None], a_ln_b[l][None],
        a_w_s[l], bias_full, out_norm_a_g[l][None], w_out[l],
        wb, wc, wglu, a_re, a_im, s5_d[l][None], out_norm_b_g[l][None])

    x1, hn = _outproj(xs, mix_a, mix_b, w_out_bf16, norm_ffn_g[l][None])

    act, w_down_bf16 = _ffn_up(hn, w_gate[l], w_up[l], w_down[l])
    out = _ffn_down(act, w_down_bf16, x1, final_norm_g[None])
    return out.reshape(1, SEQ, D_MODEL)
```

```python
import math

import jax
import jax.numpy as jnp
from jax import lax
from jax.experimental import pallas as pl
from jax.experimental.pallas import tpu as pltpu

F32 = jnp.float32
BF16 = jnp.bfloat16

D_MODEL = 2048
SEQ = 8192
CHUNK = 64
A_WIDTH = 1024
A_HEADS = 8
A_HEAD_DIM = 128
GMLP_BLOCK = 128
B_WIDTH = 1024
S5_GROUP_CH = 16
S5_GROUPS = 64
S5_STATE = 64
IN_WIDTH = 3072
D_FF = 5632
EPS = 1e-6

LANES = 128
SUBLANES = 8
MXU_COLS = 256
VMEM_LIMIT = 56 * 1024 * 1024
FFN_DOWN_VMEM_LIMIT = 60 * 1024 * 1024
MIXER_VMEM_LIMIT = 60 * 1024 * 1024
W_IN_STAGE_ROWS = 256

S5_LANE_TILES = B_WIDTH // LANES
S5_GROUPS_PER_TILE = LANES // S5_GROUP_CH
S5_STATE_TILES = S5_GROUPS * S5_STATE // LANES
S5_TILES_PER_LANE_TILE = S5_STATE_TILES // S5_LANE_TILES
S5_STATE_VREGS = S5_STATE_TILES // SUBLANES
S5_T = 512
S5_HALVES = 2
S5_PITCH = S5_T + 4


def _gelu(x):
    return 0.5 * x * (1.0 + lax.erf(x * (1.0 / math.sqrt(2.0))))


def _rms_scale(x):
    return lax.rsqrt(jnp.mean(x * x, axis=-1, keepdims=True) + EPS)


def _layernorm_bf16(z, g_ref, b_ref):
    mu = jnp.mean(z, axis=-1, keepdims=True)
    zc = z - mu
    var = jnp.mean(zc * zc, axis=-1, keepdims=True)
    return (zc * lax.rsqrt(var + EPS) * g_ref[...] + b_ref[...]).astype(BF16)


def _spatial_mix(v, ws_ref):
    n_blocks = v.shape[0] // GMLP_BLOCK
    ci = lax.broadcasted_iota(jnp.int32, (GMLP_BLOCK, GMLP_BLOCK), 0) // CHUNK
    cj = lax.broadcasted_iota(jnp.int32, (GMLP_BLOCK, GMLP_BLOCK), 1) // CHUNK
    mask = ci >= cj
    heads = []
    for h in range(A_HEADS):
        w = jnp.where(mask, ws_ref[h], 0.0).astype(BF16)
        cols = slice(h * A_HEAD_DIM, (h + 1) * A_HEAD_DIM)
        rhs = jnp.concatenate(
            [v[n * GMLP_BLOCK:(n + 1) * GMLP_BLOCK, cols] for n in range(n_blocks)], axis=1)
        heads.append(jnp.dot(w, rhs, preferred_element_type=F32))
    return jnp.concatenate(
        [jnp.concatenate([hd[:, n * A_HEAD_DIM:(n + 1) * A_HEAD_DIM] for hd in heads], axis=1)
         for n in range(n_blocks)], axis=0)


def _mixer_kernel(x_ref, g_ref, w_hbm, lng_ref, lnb_ref, ws_ref, bias_ref, oga_ref, wo_ref,
                  wb_ref, wc_ref, wglu_ref, are_ref, aim_ref, d_ref, ogb_ref,
                  mixa_ref, mixb_ref, wob_ref, st_ref, bu_ref, w_ref, stage_ref, stage_sem):
    t_rows = x_ref.shape[0]
    tile = lambda k: slice(k * LANES, (k + 1) * LANES)
    half_lane_tiles = S5_LANE_TILES // S5_HALVES
    half_state_tiles = S5_STATE_TILES // S5_HALVES
    half_vregs = S5_STATE_VREGS // S5_HALVES
    im_base = half_state_tiles * S5_PITCH

    def scratch_rows(local_tile):
        return slice(local_tile * S5_PITCH, local_tile * S5_PITCH + t_rows)

    def half_state_tiles_of(kl):
        re = [kl * S5_TILES_PER_LANE_TILE + c for c in range(S5_TILES_PER_LANE_TILE)]
        return re + [half_state_tiles + q for q in re]

    @pl.when(pl.program_id(0) == 0)
    def _():
        st_ref[...] = jnp.zeros_like(st_ref)
        chunk = stage_ref.shape[1]
        n_chunks = w_ref.shape[0] // chunk

        def chunk_copy(c):
            return pltpu.make_async_copy(w_hbm.at[c * chunk:(c + 1) * chunk, :],
                                         stage_ref.at[c % 2], stage_sem.at[c % 2])

        chunk_copy(0).start()
        for c in range(n_chunks):
            if c + 1 < n_chunks:
                chunk_copy(c + 1).start()
            chunk_copy(c).wait()
            w_ref[c * chunk:(c + 1) * chunk, :] = stage_ref[c % 2].astype(BF16)

    wob_ref[...] = wo_ref[...].astype(BF16)

    x = x_ref[...]
    xn = (x * _rms_scale(x) * g_ref[...]).astype(BF16)
    u = jnp.dot(xn, w_ref[:, 2 * A_WIDTH:], preferred_element_type=F32)

    h_pre, yc, ys, gates = [], [], [], []
    for hf in range(S5_HALVES):
        for kl in range(half_lane_tiles):
            k = hf * half_lane_tiles + kl
            res = jnp.dot(u[:, tile(k)].astype(BF16), wb_ref[k],
                          preferred_element_type=F32)
            for c, q in enumerate(half_state_tiles_of(kl)):
                bu_ref[scratch_rows(q), :] = res[:, tile(c)]

        gates += [jnp.dot(ys[k].astype(BF16), wglu_ref[k], preferred_element_type=F32)
                  for k in range(len(gates), len(ys))]

        h_pre.append(jnp.dot(xn, w_ref[:, hf * A_WIDTH:(hf + 1) * A_WIDTH],
                             preferred_element_type=F32))

        vregs = [hf * half_vregs + m for m in range(half_vregs)]
        a_re = [are_ref[gm] for gm in vregs]
        a_im = [aim_ref[gm] for gm in vregs]
        s_re = [st_ref[gm] for gm in vregs]
        s_im = [st_ref[S5_STATE_VREGS + gm] for gm in vregs]
        for t in range(t_rows):
            for m in range(half_vregs):
                re_rows = pl.ds(t + SUBLANES * m * S5_PITCH, SUBLANES, stride=S5_PITCH)
                im_rows = pl.ds(t + im_base + SUBLANES * m * S5_PITCH, SUBLANES, stride=S5_PITCH)
                nr = a_re[m] * s_re[m] - a_im[m] * s_im[m] + bu_ref[re_rows, :]
                ni = a_re[m] * s_im[m] + a_im[m] * s_re[m] + bu_ref[im_rows, :]
                bu_ref[re_rows, :] = nr
                bu_ref[im_rows, :] = ni
                s_re[m], s_im[m] = nr, ni
        for m, gm in enumerate(vregs):
            st_ref[gm] = s_re[m]
            st_ref[S5_STATE_VREGS + gm] = s_im[m]

        for kl in range(half_lane_tiles):
            k = hf * half_lane_tiles + kl
            lhs = jnp.concatenate([bu_ref[scratch_rows(q), :] for q in half_state_tiles_of(kl)],
                                  axis=1).astype(BF16)
            yc.append(jnp.dot(lhs, wc_ref[k], preferred_element_type=F32))

        ys += [_gelu(yc[k] + d_ref[:, tile(k)] * u[:, tile(k)]) for k in range(len(ys), len(yc))]
        if hf == 0:
            zu = _gelu(h_pre[0])

    gates += [jnp.dot(ys[k].astype(BF16), wglu_ref[k], preferred_element_type=F32)
              for k in range(len(gates), len(ys))]
    yb = jnp.concatenate([ys[k] * jax.nn.sigmoid(gates[k]) for k in range(S5_LANE_TILES)], axis=1)
    mixb_ref[...] = (yb * _rms_scale(yb) * ogb_ref[...]).astype(BF16)

    v = _layernorm_bf16(_gelu(h_pre[1]), lng_ref, lnb_ref)
    bias = jnp.concatenate([bias_ref[...]] * (t_rows // GMLP_BLOCK), axis=0)
    ya = zu * (_spatial_mix(v, ws_ref) + bias)
    mixa_ref[...] = (ya * _rms_scale(ya) * oga_ref[...]).astype(BF16)


def _mixer(x, g, w_in, ln_g, ln_b, w_s, bias_full, out_a_g, w_out,
           wb, wc, wglu, a_re, a_im, d, out_b_g):
    tm = S5_T
    n_steps = SEQ // tm
    wo_rows = w_out.shape[0] // n_steps
    n_state_cols = 2 * S5_GROUPS_PER_TILE * S5_STATE
    const2 = lambda i: (0, 0)
    const3 = lambda i: (0, 0, 0)
    once = dict(pipeline_mode=pl.Buffered(1))
    pipeline_in_specs = [
        pl.BlockSpec((tm, D_MODEL), lambda i: (i, 0), pipeline_mode=pl.Buffered(3)),
        pl.BlockSpec((1, D_MODEL), const2, **once),
        pl.BlockSpec((1, A_WIDTH), const2, **once),
        pl.BlockSpec((1, A_WIDTH), const2, **once),
        pl.BlockSpec((A_HEADS, GMLP_BLOCK, GMLP_BLOCK), const3, **once),
        pl.BlockSpec((GMLP_BLOCK, A_WIDTH), const2, **once),
        pl.BlockSpec((1, A_WIDTH), const2, **once),
        pl.BlockSpec((wo_rows, D_MODEL), lambda i: (i, 0)),
        pl.BlockSpec((S5_LANE_TILES, LANES, n_state_cols), const3, **once),
        pl.BlockSpec((S5_LANE_TILES, n_state_cols, LANES), const3, **once),
        pl.BlockSpec((S5_LANE_TILES, LANES, LANES), const3, **once),
        pl.BlockSpec((S5_STATE_VREGS, SUBLANES, LANES), const3, **once),
        pl.BlockSpec((S5_STATE_VREGS, SUBLANES, LANES), const3, **once),
        pl.BlockSpec((1, B_WIDTH), const2, **once),
        pl.BlockSpec((1, B_WIDTH), const2, **once),
    ]
    pipeline_out_specs = [
        pl.BlockSpec((tm, A_WIDTH), lambda i: (i, 0)),
        pl.BlockSpec((tm, B_WIDTH), lambda i: (i, 0)),
        pl.BlockSpec((wo_rows, D_MODEL), lambda i: (i, 0)),
    ]

    def outer(x_hbm, g_hbm, w_hbm, *rest):
        streamed, scratches = rest[:-5], rest[-5:]

        def body(x_ref, g_ref, *refs):
            _mixer_kernel(x_ref, g_ref, w_hbm, *refs)

        pltpu.emit_pipeline(body, grid=(n_steps,), in_specs=pipeline_in_specs,
                            out_specs=pipeline_out_specs)(
            x_hbm, g_hbm, *streamed, scratches=scratches)

    hbm = pl.BlockSpec(memory_space=pl.ANY)
    return pl.pallas_call(
        outer,
        in_specs=[hbm] * 16,
        out_specs=[hbm] * 3,
        out_shape=[jax.ShapeDtypeStruct((SEQ, A_WIDTH), BF16),
                   jax.ShapeDtypeStruct((SEQ, B_WIDTH), BF16),
                   jax.ShapeDtypeStruct(w_out.shape, BF16)],
        scratch_shapes=[
            pltpu.VMEM((2 * S5_STATE_VREGS, SUBLANES, LANES), F32),
            pltpu.VMEM((2 * S5_STATE_TILES // S5_HALVES * S5_PITCH, LANES), F32),
            pltpu.VMEM((D_MODEL, IN_WIDTH), BF16),
            pltpu.VMEM((2, W_IN_STAGE_ROWS, IN_WIDTH), F32),
            pltpu.SemaphoreType.DMA((2,)),
        ],
        compiler_params=pltpu.CompilerParams(vmem_limit_bytes=MIXER_VMEM_LIMIT),
        name="mixer",
    )(x, g, w_in, ln_g, ln_b, w_s, bias_full, out_a_g, w_out,
      wb, wc, wglu, a_re, a_im, d, out_b_g)


def _repeat_cols_onehot(n_rows, n_cols):
    r = lax.broadcasted_iota(jnp.int32, (n_rows, n_cols), 0)
    c = lax.broadcasted_iota(jnp.int32, (n_rows, n_cols), 1)
    return jnp.where(r == c % n_rows, 1.0, 0.0).astype(BF16)


def _block_diag_mask(n_rows, n_cols, row_block, col_block):
    r = lax.broadcasted_iota(jnp.int32, (n_rows, n_cols), 0)
    c = lax.broadcasted_iota(jnp.int32, (n_rows, n_cols), 1)
    return r // row_block == c // col_block


def _s5_params_kernel(lre_ref, lim_ref, ldt_ref, bre_ref, bim_ref, cre_ref, cim_ref, wg_ref,
                      abr_ref, abi_ref, wb_ref, wc_ref, wglu_ref):
    lr, li = lre_ref[...], lim_ref[...]
    dt = jnp.exp(ldt_ref[...])
    mag = jnp.exp(lr * dt)
    ab_r = mag * jnp.cos(li * dt)
    ab_i = mag * jnp.sin(li * dt)
    den = lr * lr + li * li
    nr = ab_r - 1.0
    co_r = (nr * lr + ab_i * li) / den
    co_i = (ab_i * lr - nr * li) / den
    abr_ref[...] = ab_r
    abi_ref[...] = ab_i

    gt, h, p = S5_GROUPS_PER_TILE, S5_GROUP_CH, S5_STATE
    spread_p = _repeat_cols_onehot(p, gt * p)
    spread_h = _repeat_cols_onehot(h, gt * h)
    eye_gh = _repeat_cols_onehot(gt * h, gt * h)
    mask_b = _block_diag_mask(gt * h, gt * p, h, p)
    mask_c = _block_diag_mask(gt * p, gt * h, p, h)
    mask_g = _block_diag_mask(gt * h, gt * h, h, h)
    for k in range(S5_LANE_TILES):
        grp = slice(k * gt, (k + 1) * gt)
        cr = co_r[grp][:, None, :]
        ci = co_i[grp][:, None, :]
        br, bi = bre_ref[grp], bim_ref[grp]
        bb_r = (cr * br - ci * bi).reshape(gt * h, p).astype(BF16)
        bb_i = (cr * bi + ci * br).reshape(gt * h, p).astype(BF16)
        for half, bb in enumerate((bb_r, bb_i)):
            full = jnp.dot(bb, spread_p, preferred_element_type=F32)
            wb_ref[k, :, half * gt * p:(half + 1) * gt * p] = (
                jnp.where(mask_b, full, 0.0).astype(BF16))
        for half, (c_ref, sign) in enumerate(((cre_ref, 1.0), (cim_ref, -1.0))):
            c_k = (sign * c_ref[grp]).reshape(gt * h, p).astype(BF16)
            c_t = lax.dot_general(c_k, eye_gh, (((0,), (0,)), ((), ())),
                                  preferred_element_type=F32)
            full = jnp.concatenate([c_t] * gt, axis=0)
            wc_ref[k, half * gt * p:(half + 1) * gt * p, :] = (
                jnp.where(mask_c, full, 0.0).astype(BF16))
        g_k = wg_ref[grp].reshape(gt * h, h).astype(BF16)
        full = jnp.dot(g_k, spread_h, preferred_element_type=F32)
        wglu_ref[k] = jnp.where(mask_g, full, 0.0).astype(BF16)


def _s5_params(lam_re, lam_im, log_dt, b_re, b_im, c_re, c_im, w_glu):
    gt, h, p = S5_GROUPS_PER_TILE, S5_GROUP_CH, S5_STATE
    g_p = jax.ShapeDtypeStruct((S5_GROUPS, p), F32)
    ab_r, ab_i, wb, wc, wglu = pl.pallas_call(
        _s5_params_kernel,
        out_shape=[g_p, g_p,
                   jax.ShapeDtypeStruct((S5_LANE_TILES, gt * h, 2 * gt * p), BF16),
                   jax.ShapeDtypeStruct((S5_LANE_TILES, 2 * gt * p, gt * h), BF16),
                   jax.ShapeDtypeStruct((S5_LANE_TILES, gt * h, gt * h), BF16)],
        name="s5_params",
    )(lam_re, lam_im, log_dt[:, None], jnp.swapaxes(b_re, 1, 2), jnp.swapaxes(b_im, 1, 2),
      c_re, c_im, w_glu)
    a_re = ab_r.reshape(S5_STATE_VREGS, SUBLANES, LANES)
    a_im = ab_i.reshape(S5_STATE_VREGS, SUBLANES, LANES)
    return wb, wc, wglu, a_re, a_im


def _outproj_kernel(x_ref, ma_ref, mb_ref, wa_ref, wb_ref, g_ref, x1_ref, hn_ref):
    half = x_ref.shape[0] // 2
    for r in range(2):
        rows = slice(r * half, (r + 1) * half)
        acc = jnp.dot(ma_ref[rows, :], wa_ref[...], preferred_element_type=F32)
        acc = acc + jnp.dot(mb_ref[rows, :], wb_ref[...], preferred_element_type=F32)
        x1 = x_ref[rows, :] + acc
        x1_ref[rows, :] = x1
        hn_ref[rows, :] = (x1 * _rms_scale(x1) * g_ref[...]).astype(BF16)


def _outproj(x, mix_a, mix_b, w_out_bf16, g, tm=512):
    deep = dict(pipeline_mode=pl.Buffered(3))
    pipeline = pltpu.emit_pipeline(
        _outproj_kernel,
        grid=(SEQ // tm,),
        in_specs=[
            pl.BlockSpec((tm, D_MODEL), lambda i: (i, 0), **deep),
            pl.BlockSpec((tm, A_WIDTH), lambda i: (i, 0), **deep),
            pl.BlockSpec((tm, B_WIDTH), lambda i: (i, 0), **deep),
            pl.BlockSpec((A_WIDTH, D_MODEL), lambda i: (0, 0)),
            pl.BlockSpec((B_WIDTH, D_MODEL), lambda i: (1, 0)),
            pl.BlockSpec((1, D_MODEL), lambda i: (0, 0)),
        ],
        out_specs=[
            pl.BlockSpec((tm, D_MODEL), lambda i: (i, 0)),
            pl.BlockSpec((tm, D_MODEL), lambda i: (i, 0)),
        ],
    )
    hbm = pl.BlockSpec(memory_space=pl.ANY)
    return pl.pallas_call(
        lambda *refs: pipeline(*refs),
        in_specs=[hbm] * 6,
        out_specs=[hbm] * 2,
        out_shape=[jax.ShapeDtypeStruct((SEQ, D_MODEL), F32),
                   jax.ShapeDtypeStruct((SEQ, D_MODEL), BF16)],
        compiler_params=pltpu.CompilerParams(vmem_limit_bytes=VMEM_LIMIT),
        name="outproj",
    )(x, mix_a, mix_b, w_out_bf16, w_out_bf16, g)


def _ffn_up_kernel(hn_ref, wg_ref, wu_ref, wd_ref, act_ref, wdb_ref):
    wdb_ref[...] = wd_ref[...].astype(BF16)
    hn = hn_ref[...]
    for c in range(act_ref.shape[1] // MXU_COLS):
        cols = slice(c * MXU_COLS, (c + 1) * MXU_COLS)
        gate = jnp.dot(hn, wg_ref[:, cols].astype(BF16), preferred_element_type=F32)
        up = jnp.dot(hn, wu_ref[:, cols].astype(BF16), preferred_element_type=F32)
        act_ref[:, cols] = (gate * jax.nn.sigmoid(gate) * up).astype(BF16)


def _ffn_up(hn, wg, wu, wd, tm=2048, tf=512):
    n_i = SEQ // tm
    wd_rows = wd.shape[0] // (D_FF // tf * n_i)
    wd_slab = pl.BlockSpec((wd_rows, D_MODEL), lambda f, i: (f * n_i + i, 0))
    return pl.pallas_call(
        _ffn_up_kernel,
        grid=(D_FF // tf, n_i),
        in_specs=[
            pl.BlockSpec((tm, D_MODEL), lambda f, i: (i, 0)),
            pl.BlockSpec((D_MODEL, tf), lambda f, i: (0, f)),
            pl.BlockSpec((D_MODEL, tf), lambda f, i: (0, f)),
            wd_slab,
        ],
        out_specs=[pl.BlockSpec((tm, tf), lambda f, i: (i, f)), wd_slab],
        out_shape=[jax.ShapeDtypeStruct((SEQ, D_FF), BF16),
                   jax.ShapeDtypeStruct(wd.shape, BF16)],
        compiler_params=pltpu.CompilerParams(
            dimension_semantics=("arbitrary", "arbitrary"),
            vmem_limit_bytes=VMEM_LIMIT),
        name="ffn_up",
    )(hn, wg, wu, wd)


def _ffn_down_kernel(act_ref, wd_ref, x1_ref, g_ref, o_ref):
    n_tiles = pl.num_programs(1)
    tn = wd_ref.shape[1]
    for n in range(o_ref.shape[1] // tn):
        @pl.when(pl.program_id(1) == n)
        def _(n=n):
            o_ref[:, n * tn:(n + 1) * tn] = x1_ref[...] + jnp.dot(
                act_ref[...], wd_ref[...], preferred_element_type=F32)

    @pl.when(pl.program_id(1) == n_tiles - 1)
    def _():
        x2 = o_ref[...]
        o_ref[...] = x2 * _rms_scale(x2) * g_ref[...]


def _ffn_down(act, wd_bf16, x1, g, tm=1024, tn=512):
    return pl.pallas_call(
        _ffn_down_kernel,
        grid=(SEQ // tm, D_MODEL // tn),
        in_specs=[
            pl.BlockSpec((tm, D_FF), lambda i, n: (i, 0)),
            pl.BlockSpec((D_FF, tn), lambda i, n: (0, n)),
            pl.BlockSpec((tm, tn), lambda i, n: (i, n)),
            pl.BlockSpec((1, D_MODEL), lambda i, n: (0, 0)),
        ],
        out_specs=pl.BlockSpec((tm, D_MODEL), lambda i, n: (i, 0)),
        out_shape=jax.ShapeDtypeStruct((SEQ, D_MODEL), F32),
        compiler_params=pltpu.CompilerParams(
            dimension_semantics=("arbitrary", "arbitrary"),
            vmem_limit_bytes=FFN_DOWN_VMEM_LIMIT),
        name="ffn_down",
    )(act, wd_bf16, x1, g)


def kernel(x, norm_mix_g, w_in, a_ln_g, a_ln_b, a_w_s, a_b_s, s5_lambda_re, s5_lambda_im,
           s5_log_dt, s5_b_re, s5_b_im, s5_c_re, s5_c_im, s5_d, s5_w_glu, out_norm_a_g,
           out_norm_b_g, w_out, norm_ffn_g, w_gate, w_up, w_down, final_norm_g):
    assert x.shape == (1, SEQ, D_MODEL) and norm_mix_g.shape[0] == 1
    xs = x.reshape(SEQ, D_MODEL)
    l = 0

    bias_full = jnp.repeat(jnp.transpose(a_b_s[l]), A_HEAD_DIM, axis=1)
    wb, wc, wglu, a_re, a_im = _s5_params(
        s5_lambda_re[l], s5_lambda_im[l], s5_log_dt[l], s5_b_re[l], s5_b_im[l],
        s5_c_re[l], s5_c_im[l], s5_w_glu[l])
    mix_a, mix_b, w_out_bf16 = _mixer(
        xs, norm_mix_g[l][None], w_in[l], a_ln_g[l][None], a_ln_b[l][None],
        a_w_s[l], bias_full, out_norm_a_g[l][None], w_out[l],
        wb, wc, wglu, a_re, a_im, s5_d[l][None], out_norm_b_g[l][None])

    x1, hn = _outproj(xs, mix_a, mix_b, w_out_bf16, norm_ffn_g[l][None])

    act, w_down_bf16 = _ffn_up(hn, w_gate[l], w_up[l], w_down[l])
    out = _ffn_down(act, w_down_bf16, x1, final_norm_g[None])
    return out.reshape(1, SEQ, D_MODEL)
```
